```python
import functools
import jax, jax.numpy as jnp
from jax import lax
import numpy as np

D_MODEL = 1024
BATCH = 1
SEQ = 16384
DEPTH = 2
DEC_BATCH = 32
DEC_SEQ = 1
PAST_LEN = 16384
PAGE_SIZE = 128

A_HEADS = 8
A_HEAD_DIM = 64
A_WIDTH = A_HEADS * A_HEAD_DIM
A_PATTERNS = ((128, 1), (512, 4), (2048, 16))
A_MAX_WINDOW = 2048
A_QBLK = 128
A_SCALE = A_HEAD_DIM ** -0.5
ROPE_THETA = 500000.0
ROPE_DIM = A_HEAD_DIM // 4
B_HEADS = 8
B_HEAD_DIM = 128
B_WIDTH = B_HEADS * B_HEAD_DIM
B_CHUNK = 64
B_SCALE = B_HEAD_DIM ** -0.5
D_FF = 2816
EPS = 1e-6
IN_COLS = 3 * A_WIDTH + 4 * B_WIDTH + 2 * D_MODEL

kernel_name = "hybrid_dilated_swa_hgrn2_macaron_decode_step"


def rmsnorm(x, g):
    xf = x.astype(jnp.float32)
    y = xf * lax.rsqrt(jnp.mean(xf * xf, axis=-1, keepdims=True) + EPS)
    return (y * g.astype(jnp.float32)).astype(x.dtype)


def swiglu(h, w_gate, w_up, w_down):
    return (jax.nn.silu(h @ w_gate) * (h @ w_up)) @ w_down


def partial_rope(x, pos):
    half = ROPE_DIM // 2
    inv = ROPE_THETA ** (-jnp.arange(half, dtype=jnp.float32) / half)
    ang = pos.astype(jnp.float32)[:, None] * inv[None, :]
    cos, sin = jnp.cos(ang)[:, None, :], jnp.sin(ang)[:, None, :]
    xr = x[..., :ROPE_DIM].astype(jnp.float32)
    x1, x2 = xr[..., :half], xr[..., half:]
    rot = jnp.concatenate([x1 * cos - x2 * sin, x2 * cos + x1 * sin], axis=-1)
    return jnp.concatenate([rot.astype(x.dtype), x[..., ROPE_DIM:]], axis=-1)


def _window_attn_prompt(q, k, v, window, dilation):
    B, S, H, Dh = q.shape
    span = window // dilation
    unit = dilation * A_QBLK
    s_pad = -(-S // unit) * unit
    L = s_pad // dilation
    nb = L // A_QBLK
    pad = ((0, 0), (0, s_pad - S), (0, 0), (0, 0))

    def to_blocks(t):
        t = jnp.pad(t, pad).reshape(B, L, dilation, H, Dh)
        return t.transpose(0, 2, 1, 3, 4).reshape(B, dilation, nb, A_QBLK, H, Dh)

    def with_prev(t):
        prev = jnp.pad(t[:, :, :-1], ((0, 0), (0, 0), (1, 0), (0, 0), (0, 0), (0, 0)))
        return jnp.concatenate([prev, t], axis=3)

    qb = to_blocks(q)
    kc, vc = with_prev(to_blocks(k)), with_prev(to_blocks(v))
    s = jnp.einsum('brnqhd,brnkhd->brnhqk', qb, kc).astype(jnp.float32) * A_SCALE
    qi = jnp.arange(A_QBLK)[:, None] + A_QBLK
    ki = jnp.arange(2 * A_QBLK)[None, :]
    rel = qi - ki
    band = (rel >= 0) & (rel <= span)
    has_prev = (jnp.arange(nb) > 0)[:, None, None] | (ki >= A_QBLK)[None]
    mask = band[None] & has_prev
    s = jnp.where(mask[None, None, :, None], s, -jnp.inf)
    lse = jax.nn.logsumexp(s, axis=-1)
    p = jnp.exp(s - lse[..., None])
    o = jnp.einsum('brnhqk,brnkhd->brnqhd', p.astype(vc.dtype), vc)
    o = o.reshape(B, dilation, L, H, Dh).transpose(0, 2, 1, 3, 4).reshape(B, s_pad, H, Dh)[:, :S]
    lse = lse.transpose(0, 1, 2, 4, 3).reshape(B, dilation, L, H).transpose(0, 2, 1, 3).reshape(B, s_pad, H)[:, :S]
    return o, lse


def _window_attn_sample(q, k_all, v_all, n_buf, window, dilation):
    T = q.shape[1]
    span = window // dilation
    idx = n_buf + jnp.arange(T)[:, None] - dilation * jnp.arange(span + 1)[None, :]
    valid = idx >= 0
    idx = jnp.maximum(idx, 0)
    kg, vg = k_all[:, idx], v_all[:, idx]
    s = jnp.einsum('bthd,btjhd->bthj', q, kg).astype(jnp.float32) * A_SCALE
    s = jnp.where(valid[None, :, None, :], s, -jnp.inf)
    lse = jax.nn.logsumexp(s, axis=-1)
    p = jnp.exp(s - lse[..., None])
    o = jnp.einsum('bthj,btjhd->bthd', p.astype(vg.dtype), vg)
    return o, lse


def _merge_patterns(outs, lses):
    w = jax.nn.softmax(jnp.stack(lses), axis=0)
    o = jnp.stack(outs).astype(jnp.float32)
    return jnp.sum(w[..., None] * o, axis=0)


def _attn_prompt(q, k, v):
    outs, lses = zip(*[_window_attn_prompt(q, k, v, w, d) for (w, d) in A_PATTERNS])
    w_p = min(A_MAX_WINDOW, k.shape[1])
    return _merge_patterns(outs, lses), (k[:, -w_p:], v[:, -w_p:])


def _attn_sample(k_cache, v_cache, q, k, v):
    n_buf = k_cache.shape[1]
    k_all = jnp.concatenate([k_cache.astype(k.dtype), k], axis=1)
    v_all = jnp.concatenate([v_cache.astype(v.dtype), v], axis=1)
    outs, lses = zip(*[_window_attn_sample(q, k_all, v_all, n_buf, w, d) for (w, d) in A_PATTERNS])
    return _merge_patterns(outs, lses), (k, v)


def _hgrn2_gates(q_raw, f_raw, i_raw, lb):
    q = q_raw.astype(jnp.float32) * B_SCALE
    f = lb + (1.0 - lb) * jax.nn.sigmoid(f_raw.astype(jnp.float32))
    log_f = jnp.log(f)
    k = 1.0 - f
    v = jax.nn.silu(i_raw.astype(jnp.float32))
    return q, k, v, log_f


def _hgrn2_prompt(q, k, v, log_f):
    B, S, H, K = q.shape
    V = v.shape[-1]
    s_pad = -(-S // B_CHUNK) * B_CHUNK
    nc = s_pad // B_CHUNK

    def chunks(t):
        t = jnp.pad(t, ((0, 0), (0, s_pad - S), (0, 0), (0, 0)))
        return t.reshape(B, nc, B_CHUNK, H, t.shape[-1]).transpose(1, 0, 2, 3, 4)

    causal = jnp.tril(jnp.ones((B_CHUNK, B_CHUNK), dtype=bool))

    def step(state, xs):
        qc, kc, vc, gc = xs
        b = jnp.cumsum(gc, axis=1)
        o_inter = jnp.einsum('bthk,bhkv->bthv', qc * jnp.exp(b), state)
        diff = b[:, :, None] - b[:, None]
        decay = jnp.exp(jnp.where(causal[None, :, :, None, None], diff, -jnp.inf))
        a = jnp.einsum('bthk,btshk,bshk->bths', qc, decay, kc)
        o = o_inter + jnp.einsum('bths,bshv->bthv', a, vc)
        b_last = b[:, -1]
        state = jnp.exp(b_last)[..., None] * state + jnp.einsum(
            'bshk,bshv->bhkv', kc * jnp.exp(b_last[:, None] - b), vc)
        return state, o

    s0 = jnp.zeros((B, H, K, V), jnp.float32)
    s_final, o = lax.scan(step, s0, (chunks(q), chunks(k), chunks(v), chunks(log_f)))
    o = o.transpose(1, 0, 2, 3, 4).reshape(B, s_pad, H, V)[:, :S]
    return o, s_final


def _hgrn2_sample(state, q, k, v, log_f):
    def step(s, xs):
        qt, kt, vt, gt = xs
        s = jnp.exp(gt)[..., None] * s + kt[..., None] * vt[..., None, :]
        return s, jnp.einsum('bhk,bhkv->bhv', qt, s)
    xs = tuple(t.transpose(1, 0, 2, 3) for t in (q, k, v, log_f))
    s_final, o = lax.scan(step, state.astype(jnp.float32), xs)
    return o.transpose(1, 0, 2, 3), s_final


def _layer(x, pos, lp, attn, recur):
    (f1_pre, f1_post, f1_g, f1_u, f1_d, m_pre, m_post, w_in, lb, hg_norm,
     w_a, w_b, w_o, f2_pre, f2_post, f2_g, f2_u, f2_d) = lp
    Bt, T, _ = x.shape
    x = x + 0.5 * rmsnorm(swiglu(rmsnorm(x, f1_pre), f1_g, f1_u, f1_d), f1_post)
    h = rmsnorm(x, m_pre)
    sizes = (A_WIDTH,) * 3 + (B_WIDTH,) * 4 + (D_MODEL,) * 2
    splits = tuple(int(i) for i in np.cumsum(sizes)[:-1])
    q_a, k_a, v_a, q_b, f_b, i_b, g_b, gate_a, gate_b = jnp.split(h @ w_in, splits, axis=-1)
    q_a = partial_rope(q_a.reshape(Bt, T, A_HEADS, A_HEAD_DIM), pos)
    k_a = partial_rope(k_a.reshape(Bt, T, A_HEADS, A_HEAD_DIM), pos)
    v_a = v_a.reshape(Bt, T, A_HEADS, A_HEAD_DIM)
    a_out, kv_rows = attn(q_a, k_a, v_a)
    heads = lambda t: t.reshape(Bt, T, B_HEADS, B_HEAD_DIM)
    qh, kh, vh, gh = _hgrn2_gates(heads(q_b), heads(f_b), heads(i_b), lb.reshape(B_HEADS, B_HEAD_DIM))
    b_heads, state = recur(qh, kh, vh, gh)
    b_out = rmsnorm(b_heads, hg_norm.reshape(B_HEADS, B_HEAD_DIM)) * jax.nn.silu(heads(g_b).astype(jnp.float32))
    a_proj = a_out.reshape(Bt, T, A_WIDTH).astype(x.dtype) @ w_a
    b_proj = b_out.reshape(Bt, T, B_WIDTH).astype(x.dtype) @ w_b
    mix = (jax.nn.sigmoid(gate_a) * a_proj + jax.nn.sigmoid(gate_b) * b_proj) @ w_o
    x = x + rmsnorm(mix, m_post)
    x = x + 0.5 * rmsnorm(swiglu(rmsnorm(x, f2_pre), f2_g, f2_u, f2_d), f2_post)
    return x, kv_rows, state


def setup_inputs(seed: int = 0) -> dict:
    key = jax.random.key(seed)
    ks = iter(jax.random.split(key, 32))
    f32 = jnp.float32
    nrm = lambda shape, scale: jax.random.normal(next(ks), shape, f32) * scale
    gain = lambda shape: 1.0 + 0.05 * jax.random.normal(next(ks), shape, f32)
    w_c = min(A_MAX_WINDOW, PAST_LEN)
    return {
        "x_prompt": nrm((BATCH, SEQ, D_MODEL), 1.0),
        "x_sample": nrm((DEC_BATCH, DEC_SEQ, D_MODEL), 1.0),
        "cache_k": nrm((DEPTH, DEC_BATCH, w_c, A_HEADS, A_HEAD_DIM), 1.0),
        "cache_v": nrm((DEPTH, DEC_BATCH, w_c, A_HEADS, A_HEAD_DIM), 1.0),
        "state_hgrn": nrm((DEPTH, DEC_BATCH, B_HEADS, B_HEAD_DIM, B_HEAD_DIM), 0.5),
        "ffn1_norm_pre": gain((DEPTH, D_MODEL)),
        "ffn1_norm_post": gain((DEPTH, D_MODEL)),
        "ffn1_w_gate": nrm((DEPTH, D_MODEL, D_FF), D_MODEL ** -0.5),
        "ffn1_w_up": nrm((DEPTH, D_MODEL, D_FF), D_MODEL ** -0.5),
        "ffn1_w_down": nrm((DEPTH, D_FF, D_MODEL), D_FF ** -0.5),
        "mix_norm_pre": gain((DEPTH, D_MODEL)),
        "mix_norm_post": gain((DEPTH, D_MODEL)),
        "w_in": nrm((DEPTH, D_MODEL, IN_COLS), D_MODEL ** -0.5),
        "hgrn_lb_logits": nrm((DEPTH, B_WIDTH), 1.0),
        "hgrn_out_norm": gain((DEPTH, B_WIDTH)),
        "w_a_out": nrm((DEPTH, A_WIDTH, D_MODEL), A_WIDTH ** -0.5),
        "w_b_out": nrm((DEPTH, B_WIDTH, D_MODEL), B_WIDTH ** -0.5),
        "w_mix_out": nrm((DEPTH, D_MODEL, D_MODEL), D_MODEL ** -0.5),
        "ffn2_norm_pre": gain((DEPTH, D_MODEL)),
        "ffn2_norm_post": gain((DEPTH, D_MODEL)),
        "ffn2_w_gate": nrm((DEPTH, D_MODEL, D_FF), D_MODEL ** -0.5),
        "ffn2_w_up": nrm((DEPTH, D_MODEL, D_FF), D_MODEL ** -0.5),
        "ffn2_w_down": nrm((DEPTH, D_FF, D_MODEL), D_FF ** -0.5),
    }


def reference(x_prompt, x_sample, cache_k, cache_v, state_hgrn,
              ffn1_norm_pre, ffn1_norm_post, ffn1_w_gate, ffn1_w_up, ffn1_w_down,
              mix_norm_pre, mix_norm_post, w_in, hgrn_lb_logits, hgrn_out_norm,
              w_a_out, w_b_out, w_mix_out,
              ffn2_norm_pre, ffn2_norm_post, ffn2_w_gate, ffn2_w_up, ffn2_w_down):
    lb_cum = jnp.cumsum(jax.nn.softmax(hgrn_lb_logits.astype(jnp.float32), axis=0), axis=0)
    lower_bounds = lb_cum - lb_cum[0:1]
    pos_p = jnp.arange(x_prompt.shape[1])
    pos_s = PAST_LEN + jnp.arange(x_sample.shape[1])
    yp, ys = x_prompt, x_sample
    kp, vp, sp, ksn, vsn, ssn = [], [], [], [], [], []
    for l in range(DEPTH):
        lp = (ffn1_norm_pre[l], ffn1_norm_post[l], ffn1_w_gate[l], ffn1_w_up[l], ffn1_w_down[l],
              mix_norm_pre[l], mix_norm_post[l], w_in[l], lower_bounds[l], hgrn_out_norm[l],
              w_a_out[l], w_b_out[l], w_mix_out[l],
              ffn2_norm_pre[l], ffn2_norm_post[l], ffn2_w_gate[l], ffn2_w_up[l], ffn2_w_down[l])
        yp, (k_rows, v_rows), st_p = _layer(yp, pos_p, lp, _attn_prompt, _hgrn2_prompt)
        kp.append(k_rows); vp.append(v_rows); sp.append(st_p)
        ys, (k_new, v_new), st_s = _layer(ys, pos_s, lp,
                                          functools.partial(_attn_sample, cache_k[l], cache_v[l]),
                                          functools.partial(_hgrn2_sample, state_hgrn[l]))
        ksn.append(k_new); vsn.append(v_new); ssn.append(st_s)
    return (yp, ys, jnp.stack(kp), jnp.stack(vp), jnp.stack(sp), jnp.stack(ksn), jnp.stack(vsn), jnp.stack(ssn))
```

```python
import functools

import jax
import jax.numpy as jnp
import numpy as np
from jax import lax
from jax.experimental import pallas as pl
from jax.experimental.pallas import tpu as pltpu

F32 = jnp.float32
BF16 = jnp.bfloat16

D_MODEL = 1024
DEPTH = 2
PAST_LEN = 16384
A_HEADS = 8
A_HEAD_DIM = 64
A_WIDTH = A_HEADS * A_HEAD_DIM
A_DILATIONS = (1, 4, 16)
A_SPAN = 128
A_MAX_WINDOW = 2048
A_SCALE = A_HEAD_DIM ** -0.5
ROPE_THETA = 500000.0
ROPE_DIM = A_HEAD_DIM // 4
B_HEADS = 8
B_HEAD_DIM = 128
B_WIDTH = B_HEADS * B_HEAD_DIM
B_CHUNK = 64
B_SUB = 16
B_SCALE = B_HEAD_DIM ** -0.5
D_FF = 2816
EPS = 1e-6

LANES = 128
N_PAIRS = A_WIDTH // LANES
ATTN_TILE = A_SPAN * max(A_DILATIONS)
VMEM_LIMIT = 56 * 1024 * 1024


def _params(sem, vmem=VMEM_LIMIT):
    return pltpu.CompilerParams(dimension_semantics=sem, vmem_limit_bytes=vmem)


def _resident(shape):
    nd = len(shape)
    return pl.BlockSpec(shape, lambda *_: (0,) * nd, pipeline_mode=pl.Buffered(1))


def _rms(x, g):
    y = x * lax.rsqrt(jnp.mean(x * x, axis=-1, keepdims=True) + EPS)
    return y * g


def _dot(a, b):
    return jnp.dot(a, b, preferred_element_type=F32)


def _dot_nt(a, b):
    return lax.dot_general(a, b, (((1,), (1,)), ((), ())), preferred_element_type=F32)


def _dot_tn(a, b):
    return lax.dot_general(a, b, (((0,), (0,)), ((), ())), preferred_element_type=F32)


def _ffn_kernel(x_ref, pre_ref, post_ref, wg_ref, wu_ref, wd_ref, o_ref):
    x = x_ref[...]
    h = _rms(x, pre_ref[...]).astype(BF16)
    g = _dot(h, wg_ref[...])
    u = _dot(h, wu_ref[...])
    a = (g * jax.nn.sigmoid(g) * u).astype(BF16)
    y = _dot(a, wd_ref[...])
    o_ref[...] = x + 0.5 * _rms(y, post_ref[...])


def _ffn(x, pre, post, wg, wu, wd, tm):
    n = x.shape[0]
    row = pl.BlockSpec((tm, D_MODEL), lambda i: (i, 0))
    return pl.pallas_call(
        _ffn_kernel,
        grid=(n // tm,),
        in_specs=[row, _resident((1, D_MODEL)), _resident((1, D_MODEL)),
                  _resident((D_MODEL, D_FF)), _resident((D_MODEL, D_FF)), _resident((D_FF, D_MODEL))],
        out_specs=row,
        out_shape=jax.ShapeDtypeStruct((n, D_MODEL), F32),
        compiler_params=_params(("parallel",)),
        name="ffn",
    )(x, pre, post, wg, wu, wd)


def _lower_bound(logits, layer):
    e = jnp.exp(logits - jnp.max(logits, axis=0, keepdims=True))
    sm = e / jnp.sum(e, axis=0, keepdims=True)
    lb = jnp.zeros((1, B_WIDTH), F32)
    for i in range(1, layer + 1):
        lb = lb + sm[i:i + 1, :]
    return lb


def _rope_slab(t, cos, sin_lo, sin_hi):
    return t * cos + pltpu.roll(t, LANES - ROPE_DIM // 2, axis=1) * sin_lo + pltpu.roll(t, ROPE_DIM // 2, axis=1) * sin_hi


def _hgrn_gates(zb, lb):
    q = zb[:, :B_WIDTH] * B_SCALE
    f = lb + (1.0 - lb) * jax.nn.sigmoid(zb[:, B_WIDTH:2 * B_WIDTH])
    i_raw = zb[:, 2 * B_WIDTH:]
    return q, 1.0 - f, i_raw * jax.nn.sigmoid(i_raw), jnp.log(f)


def _inproj_prompt_kernel(layer, tm, x_ref, pre_ref, cos_ref, slo_ref, shi_ref, lbl_ref, wa_ref, wb_ref,
                          q1_ref, q4_ref, q16_ref, k1_ref, k4_ref, k16_ref, v1_ref, v4_ref, v16_ref,
                          kf_ref, vf_ref, hq_ref, hk_ref, hv_ref, hg_ref, stage_ref):
    h = _rms(x_ref[...], pre_ref[...]).astype(BF16)
    za = _dot(h, wa_ref[...])
    cos, slo, shi = cos_ref[...], slo_ref[...], shi_ref[...]
    outs = ((q1_ref, q4_ref, q16_ref), (k1_ref, k4_ref, k16_ref), (v1_ref, v4_ref, v16_ref))
    for s in range(3 * N_PAIRS):
        kind, hp = divmod(s, N_PAIRS)
        t = za[:, s * LANES:(s + 1) * LANES]
        if kind < 2:
            t = _rope_slab(t, cos, slo, shi)
        if kind == 0:
            t = t * A_SCALE
        if kind == 1:
            kf_ref[:, hp * LANES:(hp + 1) * LANES] = t
        if kind == 2:
            vf_ref[:, hp * LANES:(hp + 1) * LANES] = t
        stage_ref[s] = t
        o1, o4, o16 = outs[kind]
        o1[hp] = t.astype(BF16)
        for r in range(4):
            o4[hp, r] = stage_ref[s, pl.ds(r, tm // 4, stride=4), :].astype(BF16)
        for r in range(16):
            o16[hp, r] = stage_ref[s, pl.ds(r, tm // 16, stride=16), :].astype(BF16)
    zb = _dot(h, wb_ref[...])
    q, k, v, g = _hgrn_gates(zb, _lower_bound(lbl_ref[...], layer))
    for hd in range(B_HEADS):
        sl = slice(hd * B_HEAD_DIM, (hd + 1) * B_HEAD_DIM)
        hq_ref[hd] = q[:, sl]
        hk_ref[hd] = k[:, sl]
        hv_ref[hd] = v[:, sl]
        hg_ref[hd] = g[:, sl]


def _inproj_prompt(x, pre, tabs, lbl, wa, wb, layer, tm):
    s_len = x.shape[0]
    w_len = min(A_MAX_WINDOW, s_len)
    first = (s_len - w_len) // tm
    row = pl.BlockSpec((tm, D_MODEL), lambda i: (i, 0))
    tab = pl.BlockSpec((tm, LANES), lambda i: (i, 0))
    l1 = pl.BlockSpec((N_PAIRS, tm, LANES), lambda i: (0, i, 0))
    l4 = pl.BlockSpec((N_PAIRS, 4, tm // 4, LANES), lambda i: (0, 0, i, 0))
    l16 = pl.BlockSpec((N_PAIRS, 16, tm // 16, LANES), lambda i: (0, 0, i, 0))
    win = pl.BlockSpec((tm, A_WIDTH), lambda i: (jnp.maximum(i - first, 0), 0))
    hd = pl.BlockSpec((B_HEADS, tm, B_HEAD_DIM), lambda i: (0, i, 0))
    s1 = jax.ShapeDtypeStruct((N_PAIRS, s_len, LANES), BF16)
    s4 = jax.ShapeDtypeStruct((N_PAIRS, 4, s_len // 4, LANES), BF16)
    s16 = jax.ShapeDtypeStruct((N_PAIRS, 16, s_len // 16, LANES), BF16)
    sw = jax.ShapeDtypeStruct((w_len, A_WIDTH), F32)
    sh = jax.ShapeDtypeStruct((B_HEADS, s_len, B_HEAD_DIM), F32)
    return pl.pallas_call(
        functools.partial(_inproj_prompt_kernel, layer, tm),
        grid=(s_len // tm,),
        in_specs=[row, _resident((1, D_MODEL)), tab, tab, tab, _resident((DEPTH, B_WIDTH)),
                  _resident((D_MODEL, 3 * A_WIDTH)), _resident((D_MODEL, 3 * B_WIDTH))],
        out_specs=[l1, l4, l16, l1, l4, l16, l1, l4, l16, win, win, hd, hd, hd, hd],
        out_shape=[s1, s4, s16, s1, s4, s16, s1, s4, s16, sw, sw, sh, sh, sh, sh],
        scratch_shapes=[pltpu.VMEM((3 * N_PAIRS, tm, LANES), F32)],
        compiler_params=_params(("arbitrary",)),
        name="inproj_prompt",
    )(x, pre, *tabs, lbl, wa, wb)


def _inproj_sample_kernel(layer, x_ref, pre_ref, cos_ref, slo_ref, shi_ref, lbl_ref, wa_ref, wb_ref,
                          qa_ref, ka_ref, va_ref, hq_ref, hk_ref, hv_ref, hg_ref):
    h = _rms(x_ref[...], pre_ref[...]).astype(BF16)
    za = _dot(h, wa_ref[...])
    cos, slo, shi = cos_ref[...], slo_ref[...], shi_ref[...]
    outs = (qa_ref, ka_ref, va_ref)
    for s in range(3 * N_PAIRS):
        kind, hp = divmod(s, N_PAIRS)
        t = za[:, s * LANES:(s + 1) * LANES]
        if kind < 2:
            t = _rope_slab(t, cos, slo, shi)
        if kind == 0:
            t = t * A_SCALE
        outs[kind][:, hp * LANES:(hp + 1) * LANES] = t
    zb = _dot(h, wb_ref[...])
    q, k, v, g = _hgrn_gates(zb, _lower_bound(lbl_ref[...], layer))
    hq_ref[...] = q
    hk_ref[...] = k
    hv_ref[...] = v
    hg_ref[...] = g


def _inproj_sample(x, pre, tabs, lbl, wa, wb, layer):
    n = x.shape[0]
    full = lambda *shape: pl.BlockSpec(shape, lambda i: (0,) * len(shape))
    sa = jax.ShapeDtypeStruct((n, A_WIDTH), F32)
    sb = jax.ShapeDtypeStruct((n, B_WIDTH), F32)
    return pl.pallas_call(
        functools.partial(_inproj_sample_kernel, layer),
        grid=(1,),
        in_specs=[full(n, D_MODEL), full(1, D_MODEL), full(n, LANES), full(n, LANES), full(n, LANES),
                  full(DEPTH, B_WIDTH), _resident((D_MODEL, 3 * A_WIDTH)), _resident((D_MODEL, 3 * B_WIDTH))],
        out_specs=[full(n, A_WIDTH)] * 3 + [full(n, B_WIDTH)] * 4,
        out_shape=[sa] * 3 + [sb] * 4,
        compiler_params=_params(("arbitrary",)),
        name="inproj_sample",
    )(x, pre, *tabs, lbl, wa, wb)


def _attn_block(q, kp, kc, vp, vc, has_prev):
    lane = lax.broadcasted_iota(jnp.int32, (A_SPAN, LANES), 1)
    qi = lax.broadcasted_iota(jnp.int32, (A_SPAN, LANES), 0)
    mask_p = jnp.logical_and(lane >= qi, has_prev)
    mask_c = lane <= qi
    first = lane < A_HEAD_DIM
    res = []
    for sel in (first, jnp.logical_not(first)):
        qh = jnp.where(sel, q, jnp.zeros_like(q))
        sp = jnp.where(mask_p, _dot_nt(qh, kp), -jnp.inf)
        sc = jnp.where(mask_c, _dot_nt(qh, kc), -jnp.inf)
        m = jnp.maximum(jnp.max(sp, axis=-1, keepdims=True), jnp.max(sc, axis=-1, keepdims=True))
        pp = jnp.exp(sp - m)
        pc = jnp.exp(sc - m)
        den = jnp.sum(pp, axis=-1, keepdims=True) + jnp.sum(pc, axis=-1, keepdims=True)
        o = (_dot(pp.astype(BF16), vp) + _dot(pc.astype(BF16), vc)) / den
        res.append((o, jnp.broadcast_to(m + jnp.log(den), (A_SPAN, LANES))))
    return jnp.where(first, res[0][0], res[1][0]), jnp.where(first, res[0][1], res[1][1])


def _attn_prompt_kernel(q1_ref, q4_ref, q16_ref, k1_ref, k4_ref, k16_ref, v1_ref, v4_ref, v16_ref,
                        o_ref, ks1, ks4, ks16, vs1, vs4, vs16, o_scr, l_scr):
    j = pl.program_id(1)
    q_refs = (q1_ref, q4_ref, q16_ref)
    k_in = (k1_ref, k4_ref, k16_ref)
    v_in = (v1_ref, v4_ref, v16_ref)
    k_scr = (ks1, ks4, ks16)
    v_scr = (vs1, vs4, vs16)

    @pl.when(j == 0)
    def _():
        for scr in k_scr + v_scr:
            scr[:, 0:A_SPAN, :] = jnp.zeros((scr.shape[0], A_SPAN, LANES), BF16)

    for p, d in enumerate(A_DILATIONS):
        k_scr[p][:, A_SPAN:, :] = k_in[p][...]
        v_scr[p][:, A_SPAN:, :] = v_in[p][...]

    for p, d in enumerate(A_DILATIONS):
        per_class = ATTN_TILE // d // A_SPAN

        def body(b, carry, p=p, d=d, per_class=per_class):
            r = b // per_class
            n = b % per_class
            lo = pl.multiple_of(n * A_SPAN, A_SPAN)
            hi = pl.multiple_of(n * A_SPAN + A_SPAN, A_SPAN)
            q = q_refs[p][r, pl.ds(lo, A_SPAN), :]
            kp = k_scr[p][r, pl.ds(lo, A_SPAN), :]
            kc = k_scr[p][r, pl.ds(hi, A_SPAN), :]
            vp = v_scr[p][r, pl.ds(lo, A_SPAN), :]
            vc = v_scr[p][r, pl.ds(hi, A_SPAN), :]
            o, lse = _attn_block(q, kp, kc, vp, vc, jnp.logical_or(j > 0, n > 0))
            rows = pl.ds(n * (A_SPAN * d) + r, A_SPAN, stride=d) if d > 1 else pl.ds(lo, A_SPAN)
            o_scr[p, rows, :] = o
            l_scr[p, rows, :] = lse
            return carry

        lax.fori_loop(0, ATTN_TILE // A_SPAN, body, 0)

    for p, d in enumerate(A_DILATIONS):
        tail = ATTN_TILE // d
        k_scr[p][:, 0:A_SPAN, :] = k_scr[p][:, tail:tail + A_SPAN, :]
        v_scr[p][:, 0:A_SPAN, :] = v_scr[p][:, tail:tail + A_SPAN, :]

    l0, l1, l2 = l_scr[0], l_scr[1], l_scr[2]
    m = jnp.maximum(jnp.maximum(l0, l1), l2)
    w0, w1, w2 = jnp.exp(l0 - m), jnp.exp(l1 - m), jnp.exp(l2 - m)
    o = (w0 * o_scr[0] + w1 * o_scr[1] + w2 * o_scr[2]) / (w0 + w1 + w2)
    o_ref[...] = o.astype(BF16)


def _attn_prompt(q1, q4, q16, k1, k4, k16, v1, v4, v16):
    s_len = q1.shape[1]
    t = ATTN_TILE
    b1 = pl.BlockSpec((None, 1, t, LANES), lambda hp, j: (hp, 0, j, 0))
    b4 = pl.BlockSpec((None, 4, t // 4, LANES), lambda hp, j: (hp, 0, j, 0))
    b16 = pl.BlockSpec((None, 16, t // 16, LANES), lambda hp, j: (hp, 0, j, 0))
    as4 = lambda a: a.reshape(N_PAIRS, 1, s_len, LANES)
    scr = lambda d: pltpu.VMEM((d, A_SPAN + t // d, LANES), BF16)
    return pl.pallas_call(
        _attn_prompt_kernel,
        grid=(N_PAIRS, s_len // t),
        in_specs=[b1, b4, b16] * 3,
        out_specs=pl.BlockSpec((t, LANES), lambda hp, j: (j, hp)),
        out_shape=jax.ShapeDtypeStruct((s_len, A_WIDTH), BF16),
        scratch_shapes=[scr(1), scr(4), scr(16), scr(1), scr(4), scr(16),
                        pltpu.VMEM((3, t, LANES), F32), pltpu.VMEM((3, t, LANES), F32)],
        compiler_params=_params(("arbitrary", "arbitrary")),
        name="attn_prompt",
    )(as4(q1), q4, q16, as4(k1), k4, k16, as4(v1), v4, v16)


def _decode_kernel(bb, w_c, q_ref, k_ref, v_ref, kt_ref, vt_ref,
                   hq_ref, hk_ref, hv_ref, hg_ref, st_ref, nrm_ref,
                   a_ref, bn_ref, so_ref):
    n_pat = len(A_DILATIONS)
    dist = w_c - lax.broadcasted_iota(jnp.int32, (A_HEADS, w_c), 1)
    cnt = jnp.zeros((A_HEADS, w_c), F32)
    for d in A_DILATIONS:
        cnt = cnt + jnp.logical_and(dist % d == 0, dist <= A_SPAN * d).astype(F32)
    used = cnt > 0.0
    head_s = lax.broadcasted_iota(jnp.int32, (A_HEADS, w_c), 0)
    head_o = lax.broadcasted_iota(jnp.int32, (A_HEADS, A_HEAD_DIM), 0)
    for b in range(bb):
        q = q_ref[b]
        k_new, v_new = k_ref[b], v_ref[b]
        q16 = q.astype(BF16)
        s_new = jnp.sum(q * k_new, axis=-1, keepdims=True)
        s = jnp.zeros((A_HEADS, w_c), F32)
        for h in range(A_HEADS):
            s = jnp.where(head_s == h, _dot(q16, kt_ref[b, h].astype(BF16)), s)
        m = jnp.maximum(jnp.max(jnp.where(used, s, -jnp.inf), axis=-1, keepdims=True), s_new)
        w = jnp.where(used, jnp.exp(s - m), 0.0) * cnt
        p_new = n_pat * jnp.exp(s_new - m)
        den = jnp.sum(w, axis=-1, keepdims=True) + p_new
        acc = p_new * v_new
        w16 = w.astype(BF16)
        for h in range(A_HEADS):
            acc = acc + jnp.where(head_o == h, _dot_nt(w16, vt_ref[b, h].astype(BF16)), 0.0)
        a_ref[b] = acc / den
        v_rows = hv_ref[b]
        packed = jnp.concatenate([hq_ref[b], hk_ref[b], jnp.exp(hg_ref[b]),
                                  jnp.zeros((LANES - 3 * B_HEADS, B_HEAD_DIM), F32)], axis=0)
        cols = packed.T
        outs = []
        for hd in range(B_HEADS):
            col = lambda i: cols[:, i * B_HEADS + hd:i * B_HEADS + hd + 1]
            st = col(2) * st_ref[b, hd] + col(1) * v_rows[hd:hd + 1, :]
            so_ref[b, hd] = st
            o = jnp.sum(col(0) * st, axis=0, keepdims=True)
            outs.append(_rms(o, nrm_ref[hd:hd + 1, :]))
        bn_ref[b] = jnp.concatenate(outs, axis=0)


def _decode(layer, q, k, v, cache_kt, cache_vt, hq, hk, hv, hg, state, nrm, bb):
    n = q.shape[0]
    w_c = cache_kt.shape[-1]
    assert w_c >= A_SPAN * max(A_DILATIONS)
    a_heads = lambda a: a.reshape(n, A_HEADS, A_HEAD_DIM)
    heads = lambda a: a.reshape(n, B_HEADS, B_HEAD_DIM)
    a_spec = pl.BlockSpec((bb, A_HEADS, A_HEAD_DIM), lambda i: (i, 0, 0))
    headspec = pl.BlockSpec((bb, B_HEADS, B_HEAD_DIM), lambda i: (i, 0, 0))

    c_spec = pl.BlockSpec((None, bb, A_HEADS, A_HEAD_DIM, w_c), lambda i: (layer, i, 0, 0, 0))
    st_spec = pl.BlockSpec((None, bb, B_HEADS, B_HEAD_DIM, B_HEAD_DIM), lambda i: (layer, i, 0, 0, 0))
    so_spec = pl.BlockSpec((bb, B_HEADS, B_HEAD_DIM, B_HEAD_DIM), lambda i: (i, 0, 0, 0))
    a_out, bn, st = pl.pallas_call(
        functools.partial(_decode_kernel, bb, w_c),
        grid=(n // bb,),
        in_specs=[a_spec] * 3 + [c_spec] * 2 + [headspec] * 4
                 + [st_spec, pl.BlockSpec((B_HEADS, B_HEAD_DIM), lambda i: (0, 0))],
        out_specs=[a_spec, headspec, so_spec],
        out_shape=[jax.ShapeDtypeStruct((n, A_HEADS, A_HEAD_DIM), F32),
                   jax.ShapeDtypeStruct((n, B_HEADS, B_HEAD_DIM), F32),
                   jax.ShapeDtypeStruct((n, B_HEADS, B_HEAD_DIM, B_HEAD_DIM), F32)],
        compiler_params=_params(("parallel",)),
        name="decode",
    )(a_heads(q), a_heads(k), a_heads(v), cache_kt, cache_vt, heads(hq), heads(hk), heads(hv), heads(hg), state,
      nrm.reshape(B_HEADS, B_HEAD_DIM))
    return a_out.reshape(n, A_WIDTH).astype(BF16), bn.reshape(n, B_WIDTH), st


def _hgrn_prompt_kernel(th, q_ref, k_ref, v_ref, g_ref, nrm_ref, o_ref, st_ref, state_scr, b_scr):
    j = pl.program_id(1)

    @pl.when(j == 0)
    def _():
        state_scr[...] = jnp.zeros_like(state_scr)

    rin = lax.broadcasted_iota(jnp.int32, (th, B_HEAD_DIM), 0) & (B_CHUNK - 1)
    b = g_ref[...]
    shift = 1
    while shift < B_CHUNK:
        b = b + jnp.where(rin >= shift, pltpu.roll(b, shift, axis=0), 0.0)
        shift *= 2
    b_scr[...] = b

    rc = lax.broadcasted_iota(jnp.int32, (B_CHUNK, B_HEAD_DIM), 0)
    rsub = rc & (B_SUB - 1)
    nrm = nrm_ref[...]

    def chunk(c, carry):
        rows = pl.ds(pl.multiple_of(c * B_CHUNK, B_CHUNK), B_CHUNK)
        q, k, v, bc = q_ref[rows, :], k_ref[rows, :], v_ref[rows, :], b_scr[rows, :]
        st = state_scr[...]
        v16 = v.astype(BF16)
        o = _dot_nt((q * jnp.exp(bc)).astype(BF16), st.astype(BF16))
        b_last = bc[B_CHUNK - 1:B_CHUNK, :]
        k_dec = (k * jnp.exp(b_last - bc)).astype(BF16)
        state_scr[...] = st * jnp.exp(b_last) + _dot_tn(v16, k_dec)
        parts = [jnp.zeros((B_SUB, B_HEAD_DIM), F32)]
        for i in range(1, B_CHUNK // B_SUB):
            lo = i * B_SUB
            b_ref_row = bc[lo - 1:lo, :]
            qi = (q[lo:lo + B_SUB, :] * jnp.exp(bc[lo:lo + B_SUB, :] - b_ref_row)).astype(BF16)
            kt = jnp.where(rc < lo, k * jnp.exp(jnp.minimum(b_ref_row - bc, 0.0)), 0.0).astype(BF16)
            a = _dot_nt(qi, kt)
            parts.append(_dot(a.astype(BF16), v16))
        o = o + jnp.concatenate(parts, axis=0)
        for delta in range(B_SUB):
            if delta:
                kd, bd, vd = (pltpu.roll(t, delta, axis=0) for t in (k, bc, v))
            else:
                kd, bd, vd = k, bc, v
            w = q * kd * jnp.exp(jnp.minimum(bc - bd, 0.0))
            a = jnp.where(rsub >= delta, jnp.sum(w, axis=-1, keepdims=True), 0.0)
            o = o + a * vd
        o_ref[rows, :] = _rms(o, nrm)
        return carry

    lax.fori_loop(0, th // B_CHUNK, chunk, 0)

    @pl.when(j == pl.num_programs(1) - 1)
    def _():
        st_ref[...] = state_scr[...].T


def _hgrn_prompt(q, k, v, g, nrm, th):
    s_len = q.shape[1]
    blk = pl.BlockSpec((None, th, B_HEAD_DIM), lambda h, j: (h, j, 0))
    return pl.pallas_call(
        functools.partial(_hgrn_prompt_kernel, th),
        grid=(B_HEADS, s_len // th),
        in_specs=[blk] * 4 + [pl.BlockSpec((1, B_HEAD_DIM), lambda h, j: (0, h))],
        out_specs=[pl.BlockSpec((th, B_HEAD_DIM), lambda h, j: (j, h)),
                   pl.BlockSpec((None, B_HEAD_DIM, B_HEAD_DIM), lambda h, j: (h, 0, 0))],
        out_shape=[jax.ShapeDtypeStruct((s_len, B_WIDTH), F32),
                   jax.ShapeDtypeStruct((B_HEADS, B_HEAD_DIM, B_HEAD_DIM), F32)],
        scratch_shapes=[pltpu.VMEM((B_HEAD_DIM, B_HEAD_DIM), F32), pltpu.VMEM((th, B_HEAD_DIM), F32)],
        compiler_params=_params(("arbitrary", "arbitrary")),
        name="hgrn_prompt",
    )(q, k, v, g, nrm)


def _mixout_kernel(x_ref, a_ref, bn_ref, pre_ref, post_ref, wg_ref, wa_ref, wb_ref, wo_ref, o_ref):
    x = x_ref[...]
    h = _rms(x, pre_ref[...]).astype(BF16)
    zz = _dot(h, wg_ref[...])
    g_b = zz[:, :B_WIDTH]
    b_out = (bn_ref[...] * (g_b * jax.nn.sigmoid(g_b))).astype(BF16)
    a_proj = _dot(a_ref[...], wa_ref[...])
    b_proj = _dot(b_out, wb_ref[...])
    mix = (jax.nn.sigmoid(zz[:, B_WIDTH:B_WIDTH + D_MODEL]) * a_proj
           + jax.nn.sigmoid(zz[:, B_WIDTH + D_MODEL:]) * b_proj).astype(BF16)
    o_ref[...] = x + _rms(_dot(mix, wo_ref[...]), post_ref[...])


def _mixout(x, a_out, bn, pre, post, wg, wa, wb, wo, tm):
    n = x.shape[0]
    row = lambda width: pl.BlockSpec((tm, width), lambda i: (i, 0))
    return pl.pallas_call(
        _mixout_kernel,
        grid=(n // tm,),
        in_specs=[row(D_MODEL), row(A_WIDTH), row(B_WIDTH), _resident((1, D_MODEL)), _resident((1, D_MODEL)),
                  _resident((D_MODEL, B_WIDTH + 2 * D_MODEL)), _resident((A_WIDTH, D_MODEL)),
                  _resident((B_WIDTH, D_MODEL)), _resident((D_MODEL, D_MODEL))],
        out_specs=row(D_MODEL),
        out_shape=jax.ShapeDtypeStruct((n, D_MODEL), F32),
        compiler_params=_params(("parallel",)),
        name="mixout",
    )(x, a_out, bn, pre, post, wg, wa, wb, wo)


def _rope_tables(pos):
    half = ROPE_DIM // 2
    inv = ROPE_THETA ** (-jnp.arange(half, dtype=F32) / half)
    ang = pos.astype(F32)[:, None] * inv[None, :]
    cos, sin = jnp.cos(ang), jnp.sin(ang)
    t = pos.shape[0]
    one = jnp.ones((t, A_HEAD_DIM - ROPE_DIM), F32)
    zero = jnp.zeros_like(one)
    z8 = jnp.zeros((t, half), F32)
    per_head = lambda parts: jnp.tile(jnp.concatenate(parts, axis=-1), (1, LANES // A_HEAD_DIM))
    return per_head([cos, cos, one]), per_head([-sin, z8, zero]), per_head([z8, sin, zero])


def kernel(x_prompt, x_sample, cache_k, cache_v, state_hgrn, ffn1_norm_pre, ffn1_norm_post, ffn1_w_gate, ffn1_w_up, ffn1_w_down, mix_norm_pre, mix_norm_post, w_in, hgrn_lb_logits, hgrn_out_norm, w_a_out, w_b_out, w_mix_out, ffn2_norm_pre, ffn2_norm_post, ffn2_w_gate, ffn2_w_up, ffn2_w_down):
    batch, s_len, _ = x_prompt.shape
    n_dec, t_dec, _ = x_sample.shape
    assert batch == 1 and t_dec == 1 and s_len % ATTN_TILE == 0
    tm = 512
    tm_in = 256
    th = 1024
    bb = 2
    cache_kt = jnp.transpose(cache_k, (0, 1, 3, 4, 2))
    cache_vt = jnp.transpose(cache_v, (0, 1, 3, 4, 2))
    yp = x_prompt.reshape(s_len, D_MODEL)
    ys = x_sample.reshape(n_dec, D_MODEL)
    tabs_p = _rope_tables(jnp.arange(s_len))
    tabs_s = _rope_tables(jnp.full((n_dec,), PAST_LEN))
    lbl = hgrn_lb_logits.astype(F32)
    vec = lambda a: a.reshape(1, -1).astype(F32)
    w_len = min(A_MAX_WINDOW, s_len)
    c0, c1, c2 = 3 * A_WIDTH, 3 * A_WIDTH + 3 * B_WIDTH, 3 * A_WIDTH + 4 * B_WIDTH
    kp, vp, sp, ksn, vsn, ssn = [], [], [], [], [], []
    for l in range(DEPTH):
        w16 = lambda w: w[l].astype(BF16)
        f1 = (vec(ffn1_norm_pre[l]), vec(ffn1_norm_post[l]), w16(ffn1_w_gate), w16(ffn1_w_up), w16(ffn1_w_down))
        f2 = (vec(ffn2_norm_pre[l]), vec(ffn2_norm_post[l]), w16(ffn2_w_gate), w16(ffn2_w_up), w16(ffn2_w_down))
        win = w16(w_in)
        w_att, w_rec, w_gate = win[:, :c0], win[:, c0:c1], win[:, c1:]
        m_pre, m_post, nrm = vec(mix_norm_pre[l]), vec(mix_norm_post[l]), vec(hgrn_out_norm[l])
        mo = (m_pre, m_post, w_gate, w16(w_a_out), w16(w_b_out), w16(w_mix_out))

        yp = _ffn(yp, *f1, tm)
        (q1, q4, q16, k1, k4, k16, v1, v4, v16, kf, vf, hq, hk, hv, hg) = _inproj_prompt(
            yp, m_pre, tabs_p, lbl, w_att, w_rec, l, tm_in)
        a_out = _attn_prompt(q1, q4, q16, k1, k4, k16, v1, v4, v16)
        bn, st_p = _hgrn_prompt(hq, hk, hv, hg, nrm, th)
        yp = _mixout(yp, a_out, bn, *mo, tm)
        yp = _ffn(yp, *f2, tm)
        kp.append(kf.reshape(batch, w_len, A_HEADS, A_HEAD_DIM))
        vp.append(vf.reshape(batch, w_len, A_HEADS, A_HEAD_DIM))
        sp.append(st_p.reshape(batch, B_HEADS, B_HEAD_DIM, B_HEAD_DIM))

        ys = _ffn(ys, *f1, n_dec)
        qa, ka, va, sq, sk, sv, sg = _inproj_sample(ys, m_pre, tabs_s, lbl, w_att, w_rec, l)
        a_s, bn_s, st_s = _decode(l, qa, ka, va, cache_kt, cache_vt, sq, sk, sv, sg, state_hgrn, nrm, bb)
        ys = _mixout(ys, a_s, bn_s, *mo, n_dec)
        ys = _ffn(ys, *f2, n_dec)
        ksn.append(ka.reshape(n_dec, t_dec, A_HEADS, A_HEAD_DIM))
        vsn.append(va.reshape(n_dec, t_dec, A_HEADS, A_HEAD_DIM))
        ssn.append(st_s)
    return (yp.reshape(batch, s_len, D_MODEL), ys.reshape(n_dec, t_dec, D_MODEL),
            jnp.stack(kp), jnp.stack(vp), jnp.stack(sp), jnp.stack(ksn), jnp.stack(vsn), jnp.stack(ssn))
```

```python
import functools

import jax
import jax.numpy as jnp
import numpy as np
from jax import lax
from jax.experimental import pallas as pl
from jax.experimental.pallas import tpu as pltpu

F32 = jnp.float32
BF16 = jnp.bfloat16

D_MODEL = 1024
DEPTH = 2
PAST_LEN = 16384
A_HEADS = 8
A_HEAD_DIM = 64
A_WIDTH = A_HEADS * A_HEAD_DIM
A_DILATIONS = (1, 4, 16)
A_SPAN = 128
A_MAX_WINDOW = 2048
A_SCALE = A_HEAD_DIM ** -0.5
ROPE_THETA = 500000.0
ROPE_DIM = A_HEAD_DIM // 4
B_HEADS = 8
B_HEAD_DIM = 128
B_WIDTH = B_HEADS * B_HEAD_DIM
B_CHUNK = 64
B_SUB = 4
B_UNROLL = 8
B_SCALE = B_HEAD_DIM ** -0.5
D_FF = 2816
EPS = 1e-6
LOG2_E = 1.4426950408889634

LANES = 128
N_PAIRS = A_WIDTH // LANES
ATTN_TILE = A_SPAN * max(A_DILATIONS)
A_UNROLL = 4
VMEM_LIMIT = 56 * 1024 * 1024


def _params(sem, vmem=VMEM_LIMIT):
    return pltpu.CompilerParams(dimension_semantics=sem, vmem_limit_bytes=vmem)


def _resident(shape):
    nd = len(shape)
    return pl.BlockSpec(shape, lambda *_: (0,) * nd, pipeline_mode=pl.Buffered(1))


def _rms(x, g):
    y = x * lax.rsqrt(jnp.mean(x * x, axis=-1, keepdims=True) + EPS)
    return y * g


def _dot(a, b):
    return jnp.dot(a, b, preferred_element_type=F32)


def _dot_nt(a, b):
    return lax.dot_general(a, b, (((1,), (1,)), ((), ())), preferred_element_type=F32)


def _dot_tn(a, b):
    return lax.dot_general(a, b, (((0,), (0,)), ((), ())), preferred_element_type=F32)


def _ffn_kernel(x_ref, pre_ref, post_ref, wg_ref, wu_ref, wd_ref, o_ref):
    x = x_ref[...]
    h = _rms(x, pre_ref[...]).astype(BF16)
    g = _dot(h, wg_ref[...])
    u = _dot(h, wu_ref[...])
    a = (g * jax.nn.sigmoid(g) * u).astype(BF16)
    y = _dot(a, wd_ref[...])
    o_ref[...] = x + 0.5 * _rms(y, post_ref[...])


def _ffn(x, pre, post, wg, wu, wd, tm):
    n = x.shape[0]
    row = pl.BlockSpec((tm, D_MODEL), lambda i: (i, 0))
    return pl.pallas_call(
        _ffn_kernel,
        grid=(n // tm,),
        in_specs=[row, _resident((1, D_MODEL)), _resident((1, D_MODEL)),
                  _resident((D_MODEL, D_FF)), _resident((D_MODEL, D_FF)), _resident((D_FF, D_MODEL))],
        out_specs=row,
        out_shape=jax.ShapeDtypeStruct((n, D_MODEL), F32),
        compiler_params=_params(("parallel",)),
        name="ffn",
    )(x, pre, post, wg, wu, wd)


def _lower_bound(logits, layer):
    e = jnp.exp(logits - jnp.max(logits, axis=0, keepdims=True))
    sm = e / jnp.sum(e, axis=0, keepdims=True)
    lb = jnp.zeros((1, B_WIDTH), F32)
    for i in range(1, layer + 1):
        lb = lb + sm[i:i + 1, :]
    return lb


def _rope_slab(t, cos, sin_lo, sin_hi):
    return t * cos + pltpu.roll(t, LANES - ROPE_DIM // 2, axis=1) * sin_lo + pltpu.roll(t, ROPE_DIM // 2, axis=1) * sin_hi


def _hgrn_gates(zb, lb):
    q = zb[:, :B_WIDTH] * B_SCALE
    f = lb + (1.0 - lb) * jax.nn.sigmoid(zb[:, B_WIDTH:2 * B_WIDTH])
    i_raw = zb[:, 2 * B_WIDTH:]
    return q, 1.0 - f, i_raw * jax.nn.sigmoid(i_raw), jnp.log(f)


def _inproj_prompt_kernel(layer, tm, x_ref, pre_ref, cos_ref, slo_ref, shi_ref, lbl_ref, wa_ref, wb_ref,
                          q1_ref, q4_ref, q16_ref, k1_ref, k4_ref, k16_ref, v1_ref, v4_ref, v16_ref,
                          kf_ref, vf_ref, hq_ref, hk_ref, hv_ref, hg_ref, stage_ref):
    h = _rms(x_ref[...], pre_ref[...]).astype(BF16)
    za = _dot(h, wa_ref[...])
    cos, slo, shi = cos_ref[...], slo_ref[...], shi_ref[...]
    outs = ((q1_ref, q4_ref, q16_ref), (k1_ref, k4_ref, k16_ref), (v1_ref, v4_ref, v16_ref))
    for s in range(3 * N_PAIRS):
        kind, hp = divmod(s, N_PAIRS)
        t = za[:, s * LANES:(s + 1) * LANES]
        if kind < 2:
            t = _rope_slab(t, cos, slo, shi)
        if kind == 0:
            t = t * A_SCALE
        if kind == 1:
            kf_ref[:, hp * LANES:(hp + 1) * LANES] = t
        if kind == 2:
            vf_ref[:, hp * LANES:(hp + 1) * LANES] = t
        stage_ref[s] = t
        o1, o4, o16 = outs[kind]
        o1[hp] = t.astype(BF16)
        for r in range(4):
            o4[hp, r] = stage_ref[s, pl.ds(r, tm // 4, stride=4), :].astype(BF16)
        for r in range(16):
            o16[hp, r] = stage_ref[s, pl.ds(r, tm // 16, stride=16), :].astype(BF16)
    zb = _dot(h, wb_ref[...])
    q, k, v, g = _hgrn_gates(zb, _lower_bound(lbl_ref[...], layer))
    for hd in range(B_HEADS):
        sl = slice(hd * B_HEAD_DIM, (hd + 1) * B_HEAD_DIM)
        hq_ref[hd] = q[:, sl]
        hk_ref[hd] = k[:, sl]
        hv_ref[hd] = v[:, sl]
        hg_ref[hd] = g[:, sl]


def _inproj_prompt(x, pre, tabs, lbl, wa, wb, layer, tm):
    s_len = x.shape[0]
    w_len = min(A_MAX_WINDOW, s_len)
    first = (s_len - w_len) // tm
    row = pl.BlockSpec((tm, D_MODEL), lambda i: (i, 0))
    tab = pl.BlockSpec((tm, LANES), lambda i: (i, 0))
    l1 = pl.BlockSpec((N_PAIRS, tm, LANES), lambda i: (0, i, 0))
    l4 = pl.BlockSpec((N_PAIRS, 4, tm // 4, LANES), lambda i: (0, 0, i, 0))
    l16 = pl.BlockSpec((N_PAIRS, 16, tm // 16, LANES), lambda i: (0, 0, i, 0))
    win = pl.BlockSpec((tm, A_WIDTH), lambda i: (jnp.maximum(i - first, 0), 0))
    hd = pl.BlockSpec((B_HEADS, tm, B_HEAD_DIM), lambda i: (0, i, 0))
    s1 = jax.ShapeDtypeStruct((N_PAIRS, s_len, LANES), BF16)
    s4 = jax.ShapeDtypeStruct((N_PAIRS, 4, s_len // 4, LANES), BF16)
    s16 = jax.ShapeDtypeStruct((N_PAIRS, 16, s_len // 16, LANES), BF16)
    sw = jax.ShapeDtypeStruct((w_len, A_WIDTH), F32)
    sh = jax.ShapeDtypeStruct((B_HEADS, s_len, B_HEAD_DIM), F32)
    return pl.pallas_call(
        functools.partial(_inproj_prompt_kernel, layer, tm),
        grid=(s_len // tm,),
        in_specs=[row, _resident((1, D_MODEL)), tab, tab, tab, _resident((DEPTH, B_WIDTH)),
                  _resident((D_MODEL, 3 * A_WIDTH)), _resident((D_MODEL, 3 * B_WIDTH))],
        out_specs=[l1, l4, l16, l1, l4, l16, l1, l4, l16, win, win, hd, hd, hd, hd],
        out_shape=[s1, s4, s16, s1, s4, s16, s1, s4, s16, sw, sw, sh, sh, sh, sh],
        scratch_shapes=[pltpu.VMEM((3 * N_PAIRS, tm, LANES), F32)],
        compiler_params=_params(("arbitrary",)),
        name="inproj_prompt",
    )(x, pre, *tabs, lbl, wa, wb)


def _inproj_sample_kernel(layer, x_ref, pre_ref, cos_ref, slo_ref, shi_ref, lbl_ref, wa_ref, wb_ref,
                          qa_ref, ka_ref, va_ref, hq_ref, hk_ref, hv_ref, hg_ref):
    h = _rms(x_ref[...], pre_ref[...]).astype(BF16)
    za = _dot(h, wa_ref[...])
    cos, slo, shi = cos_ref[...], slo_ref[...], shi_ref[...]
    outs = (qa_ref, ka_ref, va_ref)
    for s in range(3 * N_PAIRS):
        kind, hp = divmod(s, N_PAIRS)
        t = za[:, s * LANES:(s + 1) * LANES]
        if kind < 2:
            t = _rope_slab(t, cos, slo, shi)
        if kind == 0:
            t = t * A_SCALE
        outs[kind][:, hp * LANES:(hp + 1) * LANES] = t
    zb = _dot(h, wb_ref[...])
    q, k, v, g = _hgrn_gates(zb, _lower_bound(lbl_ref[...], layer))
    hq_ref[...] = q
    hk_ref[...] = k
    hv_ref[...] = v
    hg_ref[...] = g


def _inproj_sample(x, pre, tabs, lbl, wa, wb, layer):
    n = x.shape[0]
    full = lambda *shape: pl.BlockSpec(shape, lambda i: (0,) * len(shape))
    sa = jax.ShapeDtypeStruct((n, A_WIDTH), F32)
    sb = jax.ShapeDtypeStruct((n, B_WIDTH), F32)
    return pl.pallas_call(
        functools.partial(_inproj_sample_kernel, layer),
        grid=(1,),
        in_specs=[full(n, D_MODEL), full(1, D_MODEL), full(n, LANES), full(n, LANES), full(n, LANES),
                  full(DEPTH, B_WIDTH), _resident((D_MODEL, 3 * A_WIDTH)), _resident((D_MODEL, 3 * B_WIDTH))],
        out_specs=[full(n, A_WIDTH)] * 3 + [full(n, B_WIDTH)] * 4,
        out_shape=[sa] * 3 + [sb] * 4,
        compiler_params=_params(("arbitrary",)),
        name="inproj_sample",
    )(x, pre, *tabs, lbl, wa, wb)


def _attn_block(q, k, v_ext, has_prev):
    lane = lax.broadcasted_iota(jnp.int32, (A_SPAN, LANES), 1)
    first = lane < A_HEAD_DIM
    zero = jnp.zeros_like(q)
    q2 = jnp.concatenate([jnp.where(first, q, zero), jnp.where(first, zero, q)], axis=0)
    qi = lax.broadcasted_iota(jnp.int32, (2 * A_SPAN, 2 * A_SPAN), 0) % A_SPAN
    ki = lax.broadcasted_iota(jnp.int32, (2 * A_SPAN, 2 * A_SPAN), 1)
    valid = jnp.logical_and(jnp.logical_and(ki >= qi, ki <= qi + A_SPAN),
                            jnp.logical_or(has_prev, ki >= A_SPAN))
    s = jnp.where(valid, _dot_nt(q2, k), -jnp.inf)
    m = jnp.max(jnp.maximum(s[:, :A_SPAN], s[:, A_SPAN:]), axis=-1, keepdims=True)
    oe = _dot(jnp.exp(s - m).astype(BF16), v_ext)
    den = oe[:, LANES:]
    o = oe[:, :LANES] / den
    lse = m + jnp.log(den)
    return jnp.where(first, o[:A_SPAN], o[A_SPAN:]), jnp.where(first, lse[:A_SPAN], lse[A_SPAN:])


def _attn_prompt_kernel(q1_ref, q4_ref, q16_ref, k1_ref, k4_ref, k16_ref, v1_ref, v4_ref, v16_ref,
                        o_ref, ks1, ks4, ks16, vs1, vs4, vs16, o_scr, l_scr):
    j = pl.program_id(1)
    q_refs = (q1_ref, q4_ref, q16_ref)
    k_in = (k1_ref, k4_ref, k16_ref)
    v_in = (v1_ref, v4_ref, v16_ref)
    k_scr = (ks1, ks4, ks16)
    v_scr = (vs1, vs4, vs16)

    @pl.when(j == 0)
    def _():
        for scr in k_scr:
            scr[:, 0:A_SPAN, :] = jnp.zeros((scr.shape[0], A_SPAN, LANES), BF16)
        for scr in v_scr:
            scr[:, 0:A_SPAN, 0:LANES] = jnp.zeros((scr.shape[0], A_SPAN, LANES), BF16)
            scr[:, :, LANES:] = jnp.ones((scr.shape[0], scr.shape[1], LANES), BF16)

    for p, d in enumerate(A_DILATIONS):
        k_scr[p][:, A_SPAN:, :] = k_in[p][...]
        v_scr[p][:, A_SPAN:, 0:LANES] = v_in[p][...]

    def block(p, b):
        d = A_DILATIONS[p]
        per_class = ATTN_TILE // d // A_SPAN
        r = b // per_class
        n = b % per_class
        lo = pl.multiple_of(n * A_SPAN, A_SPAN)
        o, lse = _attn_block(q_refs[p][r, pl.ds(lo, A_SPAN), :], k_scr[p][r, pl.ds(lo, 2 * A_SPAN), :],
                             v_scr[p][r, pl.ds(lo, 2 * A_SPAN), :], jnp.logical_or(j > 0, n > 0))
        rows = pl.ds(n * (A_SPAN * d) + r, A_SPAN, stride=d) if d > 1 else pl.ds(lo, A_SPAN)
        return o, lse, rows

    n_iter = ATTN_TILE // A_SPAN // A_UNROLL
    for p in range(1, len(A_DILATIONS)):

        def dilated(i, carry, p=p):
            for u in range(A_UNROLL):
                o, lse, rows = block(p, i * A_UNROLL + u)
                o_scr[p - 1, rows, :] = o
                l_scr[p - 1, rows, :] = lse
            return carry

        lax.fori_loop(0, n_iter, dilated, 0)

    def dense(i, carry):
        for u in range(A_UNROLL):
            o0, l0, rows = block(0, i * A_UNROLL + u)
            l1, l2 = l_scr[0, rows, :], l_scr[1, rows, :]
            m = jnp.maximum(jnp.maximum(l0, l1), l2)
            w0, w1, w2 = jnp.exp(l0 - m), jnp.exp(l1 - m), jnp.exp(l2 - m)
            o = (w0 * o0 + w1 * o_scr[0, rows, :] + w2 * o_scr[1, rows, :]) / (w0 + w1 + w2)
            o_ref[rows, :] = o.astype(BF16)
        return carry

    lax.fori_loop(0, n_iter, dense, 0)

    for p, d in enumerate(A_DILATIONS):
        tail = ATTN_TILE // d
        k_scr[p][:, 0:A_SPAN, :] = k_scr[p][:, tail:tail + A_SPAN, :]
        v_scr[p][:, 0:A_SPAN, 0:LANES] = v_scr[p][:, tail:tail + A_SPAN, 0:LANES]


def _attn_prompt(q1, q4, q16, k1, k4, k16, v1, v4, v16):
    s_len = q1.shape[1]
    t = ATTN_TILE
    b1 = pl.BlockSpec((None, 1, t, LANES), lambda hp, j: (hp, 0, j, 0))
    b4 = pl.BlockSpec((None, 4, t // 4, LANES), lambda hp, j: (hp, 0, j, 0))
    b16 = pl.BlockSpec((None, 16, t // 16, LANES), lambda hp, j: (hp, 0, j, 0))
    as4 = lambda a: a.reshape(N_PAIRS, 1, s_len, LANES)
    scr = lambda d, width: pltpu.VMEM((d, A_SPAN + t // d, width), BF16)
    n_dil = len(A_DILATIONS) - 1
    return pl.pallas_call(
        _attn_prompt_kernel,
        grid=(N_PAIRS, s_len // t),
        in_specs=[b1, b4, b16] * 3,
        out_specs=pl.BlockSpec((t, LANES), lambda hp, j: (j, hp)),
        out_shape=jax.ShapeDtypeStruct((s_len, A_WIDTH), BF16),
        scratch_shapes=[scr(d, LANES) for d in A_DILATIONS] + [scr(d, 2 * LANES) for d in A_DILATIONS]
                       + [pltpu.VMEM((n_dil, t, LANES), F32), pltpu.VMEM((n_dil, t, LANES), F32)],
        compiler_params=_params(("arbitrary", "arbitrary")),
        name="attn_prompt",
    )(as4(q1), q4, q16, as4(k1), k4, k16, as4(v1), v4, v16)


def _decode_kernel(bb, w_c, q_ref, k_ref, v_ref, kt_ref, vt_ref,
                   hq_ref, hk_ref, hv_ref, hg_ref, st_ref, nrm_ref,
                   a_ref, bn_ref, so_ref):
    n_pat = len(A_DILATIONS)
    dist = w_c - lax.broadcasted_iota(jnp.int32, (A_HEADS, w_c), 1)
    cnt = jnp.zeros((A_HEADS, w_c), F32)
    for d in A_DILATIONS:
        cnt = cnt + jnp.logical_and(dist % d == 0, dist <= A_SPAN * d).astype(F32)
    used = cnt > 0.0
    head_s = lax.broadcasted_iota(jnp.int32, (A_HEADS, w_c), 0)
    head_o = lax.broadcasted_iota(jnp.int32, (A_HEADS, A_HEAD_DIM), 0)
    for b in range(bb):
        q = q_ref[b]
        k_new, v_new = k_ref[b], v_ref[b]
        q16 = q.astype(BF16)
        s_new = jnp.sum(q * k_new, axis=-1, keepdims=True)
        s = jnp.zeros((A_HEADS, w_c), F32)
        for h in range(A_HEADS):
            s = jnp.where(head_s == h, _dot(q16, kt_ref[b, h].astype(BF16)), s)
        m = jnp.maximum(jnp.max(jnp.where(used, s, -jnp.inf), axis=-1, keepdims=True), s_new)
        w = jnp.where(used, jnp.exp(s - m), 0.0) * cnt
        p_new = n_pat * jnp.exp(s_new - m)
        den = jnp.sum(w, axis=-1, keepdims=True) + p_new
        acc = p_new * v_new
        w16 = w.astype(BF16)
        for h in range(A_HEADS):
            acc = acc + jnp.where(head_o == h, _dot_nt(w16, vt_ref[b, h].astype(BF16)), 0.0)
        a_ref[b] = acc / den
        v_rows = hv_ref[b]
        packed = jnp.concatenate([hq_ref[b], hk_ref[b], jnp.exp(hg_ref[b]),
                                  jnp.zeros((LANES - 3 * B_HEADS, B_HEAD_DIM), F32)], axis=0)
        cols = packed.T
        outs = []
        for hd in range(B_HEADS):
            col = lambda i: cols[:, i * B_HEADS + hd:i * B_HEADS + hd + 1]
            st = col(2) * st_ref[b, hd] + col(1) * v_rows[hd:hd + 1, :]
            so_ref[b, hd] = st
            o = jnp.sum(col(0) * st, axis=0, keepdims=True)
            outs.append(_rms(o, nrm_ref[hd:hd + 1, :]))
        bn_ref[b] = jnp.concatenate(outs, axis=0)


def _decode(layer, q, k, v, cache_kt, cache_vt, hq, hk, hv, hg, state, nrm, bb):
    n = q.shape[0]
    w_c = cache_kt.shape[-1]
    assert w_c >= A_SPAN * max(A_DILATIONS)
    a_heads = lambda a: a.reshape(n, A_HEADS, A_HEAD_DIM)
    heads = lambda a: a.reshape(n, B_HEADS, B_HEAD_DIM)
    a_spec = pl.BlockSpec((bb, A_HEADS, A_HEAD_DIM), lambda i: (i, 0, 0))
    headspec = pl.BlockSpec((bb, B_HEADS, B_HEAD_DIM), lambda i: (i, 0, 0))

    c_spec = pl.BlockSpec((None, bb, A_HEADS, A_HEAD_DIM, w_c), lambda i: (layer, i, 0, 0, 0))
    st_spec = pl.BlockSpec((None, bb, B_HEADS, B_HEAD_DIM, B_HEAD_DIM), lambda i: (layer, i, 0, 0, 0))
    so_spec = pl.BlockSpec((bb, B_HEADS, B_HEAD_DIM, B_HEAD_DIM), lambda i: (i, 0, 0, 0))
    a_out, bn, st = pl.pallas_call(
        functools.partial(_decode_kernel, bb, w_c),
        grid=(n // bb,),
        in_specs=[a_spec] * 3 + [c_spec] * 2 + [headspec] * 4
                 + [st_spec, pl.BlockSpec((B_HEADS, B_HEAD_DIM), lambda i: (0, 0))],
        out_specs=[a_spec, headspec, so_spec],
        out_shape=[jax.ShapeDtypeStruct((n, A_HEADS, A_HEAD_DIM), F32),
                   jax.ShapeDtypeStruct((n, B_HEADS, B_HEAD_DIM), F32),
                   jax.ShapeDtypeStruct((n, B_HEADS, B_HEAD_DIM, B_HEAD_DIM), F32)],
        compiler_params=_params(("parallel",)),
        name="decode",
    )(a_heads(q), a_heads(k), a_heads(v), cache_kt, cache_vt, heads(hq), heads(hk), heads(hv), heads(hg), state,
      nrm.reshape(B_HEADS, B_HEAD_DIM))
    return a_out.reshape(n, A_WIDTH).astype(BF16), bn.reshape(n, B_WIDTH), st


def _hgrn_prompt_kernel(th, q_ref, k_ref, v_ref, g_ref, nrm_ref, o_ref, st_ref, state_scr, b_scr):
    j = pl.program_id(1)

    @pl.when(j == 0)
    def _():
        state_scr[...] = jnp.zeros_like(state_scr)

    rin = lax.broadcasted_iota(jnp.int32, (th, B_HEAD_DIM), 0) & (B_CHUNK - 1)
    b = g_ref[...] * LOG2_E
    shift = 1
    while shift < B_CHUNK:
        b = b + jnp.where(rin >= shift, pltpu.roll(b, shift, axis=0), 0.0)
        shift *= 2
    b_scr[...] = b

    row = lax.broadcasted_iota(jnp.int32, (B_CHUNK, B_HEAD_DIM), 0)
    a_t = lax.broadcasted_iota(jnp.int32, (B_CHUNK, B_CHUNK), 0)
    a_s = lax.broadcasted_iota(jnp.int32, (B_CHUNK, B_CHUNK), 1)
    levels = []
    half = B_CHUNK // 2
    while half >= B_SUB:
        seg = 2 * half
        pair = jnp.logical_and(a_t // seg == a_s // seg,
                               jnp.logical_and(a_t % seg >= half, a_s % seg < half))
        levels.append((half, row % seg >= half, pair))
        half //= 2
    n_grp = B_CHUNK // 8
    sub_row = lax.broadcasted_iota(jnp.int32, (n_grp, 8, B_HEAD_DIM), 1) % B_SUB
    nrm = nrm_ref[...]

    def intra_chunk(q, k, v, v16, bc):
        a = jnp.zeros((B_CHUNK, B_CHUNK), F32)
        for half, upper, pair in levels:
            seg = 2 * half
            b_mid = jnp.concatenate(
                [jnp.broadcast_to(bc[s0 + half - 1:s0 + half, :], (seg, B_HEAD_DIM))
                 for s0 in range(0, B_CHUNK, seg)], axis=0)
            d = bc - b_mid
            x = (jnp.where(upper, q, k) * jnp.exp2(jnp.where(upper, d, -d))).astype(BF16)
            a = jnp.where(pair, _dot_nt(x, x), a)
        o = _dot(a.astype(BF16), v16)
        q3, k3, v3, b3 = (t.reshape(n_grp, 8, B_HEAD_DIM) for t in (q, k, v, bc))
        o3 = jnp.sum(q3 * k3, axis=-1, keepdims=True) * v3
        for delta in range(1, B_SUB):
            kd, bd, vd = (pltpu.roll(t, delta, axis=1) for t in (k3, b3, v3))
            w = jnp.sum(q3 * kd * jnp.exp2(b3 - bd), axis=-1, keepdims=True)
            o3 = o3 + jnp.where(sub_row >= delta, w, 0.0) * vd
        return o + o3.reshape(B_CHUNK, B_HEAD_DIM)

    def chunks(i, carry):
        st = state_scr[...]
        for u in range(B_UNROLL):
            rows = pl.ds(pl.multiple_of((i * B_UNROLL + u) * B_CHUNK, B_CHUNK), B_CHUNK)
            q, k, v, bc = q_ref[rows, :], k_ref[rows, :], v_ref[rows, :], b_scr[rows, :]
            v16 = v.astype(BF16)
            o = _dot_nt((q * jnp.exp2(bc)).astype(BF16), st.astype(BF16))
            b_last = bc[B_CHUNK - 1:B_CHUNK, :]
            k_dec = (k * jnp.exp2(b_last - bc)).astype(BF16)
            st = st * jnp.exp2(b_last) + _dot_tn(v16, k_dec)
            o_ref[rows, :] = _rms(o + intra_chunk(q, k, v, v16, bc), nrm)
        state_scr[...] = st
        return carry

    lax.fori_loop(0, th // (B_CHUNK * B_UNROLL), chunks, 0)

    @pl.when(j == pl.num_programs(1) - 1)
    def _():
        st_ref[...] = state_scr[...].T


def _hgrn_prompt(q, k, v, g, nrm, th):
    s_len = q.shape[1]
    blk = pl.BlockSpec((None, th, B_HEAD_DIM), lambda h, j: (h, j, 0))
    return pl.pallas_call(
        functools.partial(_hgrn_prompt_kernel, th),
        grid=(B_HEADS, s_len // th),
        in_specs=[blk] * 4 + [pl.BlockSpec((1, B_HEAD_DIM), lambda h, j: (0, h))],
        out_specs=[pl.BlockSpec((th, B_HEAD_DIM), lambda h, j: (j, h)),
                   pl.BlockSpec((None, B_HEAD_DIM, B_HEAD_DIM), lambda h, j: (h, 0, 0))],
        out_shape=[jax.ShapeDtypeStruct((s_len, B_WIDTH), F32),
                   jax.ShapeDtypeStruct((B_HEADS, B_HEAD_DIM, B_HEAD_DIM), F32)],
        scratch_shapes=[pltpu.VMEM((B_HEAD_DIM, B_HEAD_DIM), F32), pltpu.VMEM((th, B_HEAD_DIM), F32)],
        compiler_params=_params(("arbitrary", "arbitrary")),
        name="hgrn_prompt",
    )(q, k, v, g, nrm)


def _mixout_kernel(x_ref, a_ref, bn_ref, pre_ref, post_ref, wg_ref, wa_ref, wb_ref, wo_ref, o_ref):
    x = x_ref[...]
    h = _rms(x, pre_ref[...]).astype(BF16)
    zz = _dot(h, wg_ref[...])
    g_b = zz[:, :B_WIDTH]
    b_out = (bn_ref[...] * (g_b * jax.nn.sigmoid(g_b))).astype(BF16)
    a_proj = _dot(a_ref[...], wa_ref[...])
    b_proj = _dot(b_out, wb_ref[...])
    mix = (jax.nn.sigmoid(zz[:, B_WIDTH:B_WIDTH + D_MODEL]) * a_proj
           + jax.nn.sigmoid(zz[:, B_WIDTH + D_MODEL:]) * b_proj).astype(BF16)
    o_ref[...] = x + _rms(_dot(mix, wo_ref[...]), post_ref[...])


def _mixout(x, a_out, bn, pre, post, wg, wa, wb, wo, tm):
    n = x.shape[0]
    row = lambda width: pl.BlockSpec((tm, width), lambda i: (i, 0))
    return pl.pallas_call(
        _mixout_kernel,
        grid=(n // tm,),
        in_specs=[row(D_MODEL), row(A_WIDTH), row(B_WIDTH), _resident((1, D_MODEL)), _resident((1, D_MODEL)),
                  _resident((D_MODEL, B_WIDTH + 2 * D_MODEL)), _resident((A_WIDTH, D_MODEL)),
                  _resident((B_WIDTH, D_MODEL)), _resident((D_MODEL, D_MODEL))],
        out_specs=row(D_MODEL),
        out_shape=jax.ShapeDtypeStruct((n, D_MODEL), F32),
        compiler_params=_params(("parallel",)),
        name="mixout",
    )(x, a_out, bn, pre, post, wg, wa, wb, wo)


def _rope_tables(pos):
    half = ROPE_DIM // 2
    inv = ROPE_THETA ** (-jnp.arange(half, dtype=F32) / half)
    ang = pos.astype(F32)[:, None] * inv[None, :]
    cos, sin = jnp.cos(ang), jnp.sin(ang)
    t = pos.shape[0]
    one = jnp.ones((t, A_HEAD_DIM - ROPE_DIM), F32)
    zero = jnp.zeros_like(one)
    z8 = jnp.zeros((t, half), F32)
    per_head = lambda parts: jnp.tile(jnp.concatenate(parts, axis=-1), (1, LANES // A_HEAD_DIM))
    return per_head([cos, cos, one]), per_head([-sin, z8, zero]), per_head([z8, sin, zero])


def kernel(x_prompt, x_sample, cache_k, cache_v, state_hgrn, ffn1_norm_pre, ffn1_norm_post, ffn1_w_gate, ffn1_w_up, ffn1_w_down, mix_norm_pre, mix_norm_post, w_in, hgrn_lb_logits, hgrn_out_norm, w_a_out, w_b_out, w_mix_out, ffn2_norm_pre, ffn2_norm_post, ffn2_w_gate, ffn2_w_up, ffn2_w_down):
    batch, s_len, _ = x_prompt.shape
    n_dec, t_dec, _ = x_sample.shape
    assert batch == 1 and t_dec == 1 and s_len % ATTN_TILE == 0
    tm = 512
    tm_in = 256
    th = 2048
    bb = 2
    cache_kt = jnp.transpose(cache_k, (0, 1, 3, 4, 2))
    cache_vt = jnp.transpose(cache_v, (0, 1, 3, 4, 2))
    yp = x_prompt.reshape(s_len, D_MODEL)
    ys = x_sample.reshape(n_dec, D_MODEL)
    tabs_p = _rope_tables(jnp.arange(s_len))
    tabs_s = _rope_tables(jnp.full((n_dec,), PAST_LEN))
    lbl = hgrn_lb_logits.astype(F32)
    vec = lambda a: a.reshape(1, -1).astype(F32)
    w_len = min(A_MAX_WINDOW, s_len)
    c0, c1, c2 = 3 * A_WIDTH, 3 * A_WIDTH + 3 * B_WIDTH, 3 * A_WIDTH + 4 * B_WIDTH
    kp, vp, sp, ksn, vsn, ssn = [], [], [], [], [], []
    for l in range(DEPTH):
        w16 = lambda w: w[l].astype(BF16)
        f1 = (vec(ffn1_norm_pre[l]), vec(ffn1_norm_post[l]), w16(ffn1_w_gate), w16(ffn1_w_up), w16(ffn1_w_down))
        f2 = (vec(ffn2_norm_pre[l]), vec(ffn2_norm_post[l]), w16(ffn2_w_gate), w16(ffn2_w_up), w16(ffn2_w_down))
        win = w16(w_in)
        w_att, w_rec, w_gate = win[:, :c0], win[:, c0:c1], win[:, c1:]
        m_pre, m_post, nrm = vec(mix_norm_pre[l]), vec(mix_norm_post[l]), vec(hgrn_out_norm[l])
        mo = (m_pre, m_post, w_gate, w16(w_a_out), w16(w_b_out), w16(w_mix_out))

        yp = _ffn(yp, *f1, tm)
        (q1, q4, q16, k1, k4, k16, v1, v4, v16, kf, vf, hq, hk, hv, hg) = _inproj_prompt(
            yp, m_pre, tabs_p, lbl, w_att, w_rec, l, tm_in)
        a_out = _attn_prompt(q1, q4, q16, k1, k4, k16, v1, v4, v16)
        bn, st_p = _hgrn_prompt(hq, hk, hv, hg, nrm, th)
        yp = _mixout(yp, a_out, bn, *mo, tm)
        yp = _ffn(yp, *f2, tm)
        kp.append(kf.reshape(batch, w_len, A_HEADS, A_HEAD_DIM))
        vp.append(vf.reshape(batch, w_len, A_HEADS, A_HEAD_DIM))
        sp.append(st_p.reshape(batch, B_HEADS, B_HEAD_DIM, B_HEAD_DIM))

        ys = _ffn(ys, *f1, n_dec)
        qa, ka, va, sq, sk, sv, sg = _inproj_sample(ys, m_pre, tabs_s, lbl, w_att, w_rec, l)
        a_s, bn_s, st_s = _decode(l, qa, ka, va, cache_kt, cache_vt, sq, sk, sv, sg, state_hgrn, nrm, bb)
        ys = _mixout(ys, a_s, bn_s, *mo, n_dec)
        ys = _ffn(ys, *f2, n_dec)
        ksn.append(ka.reshape(n_dec, t_dec, A_HEADS, A_HEAD_DIM))
        vsn.append(va.reshape(n_dec, t_dec, A_HEADS, A_HEAD_DIM))
        ssn.append(st_s)
    return (yp.reshape(batch, s_len, D_MODEL), ys.reshape(n_dec, t_dec, D_MODEL),
            jnp.stack(kp), jnp.stack(vp), jnp.stack(sp), jnp.stack(ksn), jnp.stack(vsn), jnp.stack(ssn))
```

```python
import functools

import jax
import jax.numpy as jnp
import numpy as np
from jax import lax
from jax.experimental import pallas as pl
from jax.experimental.pallas import tpu as pltpu

F32 = jnp.float32
BF16 = jnp.bfloat16

D_MODEL = 1024
DEPTH = 2
PAST_LEN = 16384
A_HEADS = 8
A_HEAD_DIM = 64
A_WIDTH = A_HEADS * A_HEAD_DIM
A_DILATIONS = (1, 4, 16)
A_SPAN = 128
A_MAX_WINDOW = 2048
A_SCALE = A_HEAD_DIM ** -0.5
ROPE_THETA = 500000.0
ROPE_DIM = A_HEAD_DIM // 4
B_HEADS = 8
B_HEAD_DIM = 128
B_WIDTH = B_HEADS * B_HEAD_DIM
B_CHUNK = 64
B_SUB = 4
B_UNROLL = 4
B_SCALE = B_HEAD_DIM ** -0.5
D_FF = 2816
EPS = 1e-6
LOG2_E = 1.4426950408889634

W_BLOCK = 3 * A_WIDTH
assert 3 * B_WIDTH == 2 * W_BLOCK and B_WIDTH + 2 * D_MODEL == 2 * W_BLOCK
LANES = 128
N_PAIRS = A_WIDTH // LANES
ATTN_TILE = A_SPAN * max(A_DILATIONS)
A_UNROLL = 4
VMEM_LIMIT = 56 * 1024 * 1024


def _params(sem, vmem=VMEM_LIMIT):
    return pltpu.CompilerParams(dimension_semantics=sem, vmem_limit_bytes=vmem)


def _resident(shape):
    nd = len(shape)
    return pl.BlockSpec(shape, lambda *_: (0,) * nd, pipeline_mode=pl.Buffered(1))


def _layer_block(shape, layer, col=0):
    index = (layer,) + (0,) * (len(shape) - 1) + (col,)
    return pl.BlockSpec((None,) + tuple(shape), lambda *_: index, pipeline_mode=pl.Buffered(1))


def _vec(layer):
    return _layer_block((1, D_MODEL), layer)


def _rms(x, g):
    y = x * lax.rsqrt(jnp.mean(x * x, axis=-1, keepdims=True) + EPS)
    return y * g


def _dot(a, b):
    return jnp.dot(a, b, preferred_element_type=F32)


def _dot_nt(a, b):
    return lax.dot_general(a, b, (((1,), (1,)), ((), ())), preferred_element_type=F32)


def _dot_tn(a, b):
    return lax.dot_general(a, b, (((0,), (0,)), ((), ())), preferred_element_type=F32)


def _ffn_kernel(x_ref, pre_ref, post_ref, wg_ref, wu_ref, wd_ref, o_ref):
    x = x_ref[...]
    h = _rms(x, pre_ref[...]).astype(BF16)
    g = _dot(h, wg_ref[...])
    u = _dot(h, wu_ref[...])
    a = (g * jax.nn.sigmoid(g) * u).astype(BF16)
    y = _dot(a, wd_ref[...])
    o_ref[...] = x + 0.5 * _rms(y, post_ref[...])


def _ffn(x, layer, pre, post, wg, wu, wd, tm):
    n = x.shape[0]
    row = pl.BlockSpec((tm, D_MODEL), lambda i: (i, 0))
    return pl.pallas_call(
        _ffn_kernel,
        grid=(n // tm,),
        in_specs=[row, _vec(layer), _vec(layer), _layer_block((D_MODEL, D_FF), layer),
                  _layer_block((D_MODEL, D_FF), layer), _layer_block((D_FF, D_MODEL), layer)],
        out_specs=row,
        out_shape=jax.ShapeDtypeStruct((n, D_MODEL), F32),
        compiler_params=_params(("parallel",)),
        name="ffn",
    )(x, pre, post, wg, wu, wd)


def _lower_bound(logits, layer):
    e = jnp.exp(logits - jnp.max(logits, axis=0, keepdims=True))
    sm = e / jnp.sum(e, axis=0, keepdims=True)
    lb = jnp.zeros((1, B_WIDTH), F32)
    for i in range(1, layer + 1):
        lb = lb + sm[i:i + 1, :]
    return lb


def _rope_slab(t, cos, sin_lo, sin_hi):
    return t * cos + pltpu.roll(t, LANES - ROPE_DIM // 2, axis=1) * sin_lo + pltpu.roll(t, ROPE_DIM // 2, axis=1) * sin_hi


def _hgrn_gates(h, wb0_ref, wb1_ref, lb):
    z0 = _dot(h, wb0_ref[...])
    z1 = _dot(h, wb1_ref[...])
    split = W_BLOCK - B_WIDTH
    q = z0[:, :B_WIDTH] * B_SCALE
    f_raw = jnp.concatenate([z0[:, B_WIDTH:], z1[:, :split]], axis=1)
    i_raw = z1[:, split:]
    f = lb + (1.0 - lb) * jax.nn.sigmoid(f_raw)
    return q, 1.0 - f, i_raw * jax.nn.sigmoid(i_raw), jnp.log(f)


def _inproj_prompt_kernel(layer, tm, x_ref, pre_ref, cos_ref, slo_ref, shi_ref, lbl_ref, wa_ref, wb0_ref, wb1_ref,
                          q1_ref, q4_ref, q16_ref, k1_ref, k4_ref, k16_ref, v1_ref, v4_ref, v16_ref,
                          kf_ref, vf_ref, hq_ref, hk_ref, hv_ref, hg_ref, stage_ref):
    h = _rms(x_ref[...], pre_ref[...]).astype(BF16)
    za = _dot(h, wa_ref[...])
    cos, slo, shi = cos_ref[...], slo_ref[...], shi_ref[...]
    outs = ((q1_ref, q4_ref, q16_ref), (k1_ref, k4_ref, k16_ref), (v1_ref, v4_ref, v16_ref))
    for s in range(3 * N_PAIRS):
        kind, hp = divmod(s, N_PAIRS)
        t = za[:, s * LANES:(s + 1) * LANES]
        if kind < 2:
            t = _rope_slab(t, cos, slo, shi)
        if kind == 0:
            t = t * (A_SCALE * LOG2_E)
        if kind == 1:
            kf_ref[:, hp * LANES:(hp + 1) * LANES] = t
        if kind == 2:
            vf_ref[:, hp * LANES:(hp + 1) * LANES] = t
        stage_ref[s] = t
        o1, o4, o16 = outs[kind]
        o1[hp] = t.astype(BF16)
        for r in range(4):
            o4[hp, r] = stage_ref[s, pl.ds(r, tm // 4, stride=4), :].astype(BF16)
        for r in range(16):
            o16[hp, r] = stage_ref[s, pl.ds(r, tm // 16, stride=16), :].astype(BF16)
    q, k, v, g = _hgrn_gates(h, wb0_ref, wb1_ref, _lower_bound(lbl_ref[...], layer))
    for hd in range(B_HEADS):
        sl = slice(hd * B_HEAD_DIM, (hd + 1) * B_HEAD_DIM)
        hq_ref[hd] = q[:, sl]
        hk_ref[hd] = k[:, sl]
        hv_ref[hd] = v[:, sl]
        hg_ref[hd] = g[:, sl]


def _inproj_prompt(x, pre, tabs, lbl, w_in, layer, tm):
    s_len = x.shape[0]
    w_len = min(A_MAX_WINDOW, s_len)
    first = (s_len - w_len) // tm
    row = pl.BlockSpec((tm, D_MODEL), lambda i: (i, 0))
    tab = pl.BlockSpec((tm, LANES), lambda i: (i, 0))
    l1 = pl.BlockSpec((N_PAIRS, tm, LANES), lambda i: (0, i, 0))
    l4 = pl.BlockSpec((N_PAIRS, 4, tm // 4, LANES), lambda i: (0, 0, i, 0))
    l16 = pl.BlockSpec((N_PAIRS, 16, tm // 16, LANES), lambda i: (0, 0, i, 0))
    win = pl.BlockSpec((tm, A_WIDTH), lambda i: (jnp.maximum(i - first, 0), 0))
    hd = pl.BlockSpec((B_HEADS, tm, B_HEAD_DIM), lambda i: (0, i, 0))
    s1 = jax.ShapeDtypeStruct((N_PAIRS, s_len, LANES), BF16)
    s4 = jax.ShapeDtypeStruct((N_PAIRS, 4, s_len // 4, LANES), BF16)
    s16 = jax.ShapeDtypeStruct((N_PAIRS, 16, s_len // 16, LANES), BF16)
    sw = jax.ShapeDtypeStruct((w_len, A_WIDTH), F32)
    sh = jax.ShapeDtypeStruct((B_HEADS, s_len, B_HEAD_DIM), F32)
    return pl.pallas_call(
        functools.partial(_inproj_prompt_kernel, layer, tm),
        grid=(s_len // tm,),
        in_specs=[row, _vec(layer), tab, tab, tab, _resident((DEPTH, B_WIDTH))]
                 + [_layer_block((D_MODEL, W_BLOCK), layer, c) for c in range(3)],
        out_specs=[l1, l4, l16, l1, l4, l16, l1, l4, l16, win, win, hd, hd, hd, hd],
        out_shape=[s1, s4, s16, s1, s4, s16, s1, s4, s16, sw, sw, sh, sh, sh, sh],
        scratch_shapes=[pltpu.VMEM((3 * N_PAIRS, tm, LANES), F32)],
        compiler_params=_params(("arbitrary",)),
        name="inproj_prompt",
    )(x, pre, *tabs, lbl, w_in, w_in, w_in)


def _inproj_sample_kernel(layer, x_ref, pre_ref, cos_ref, slo_ref, shi_ref, lbl_ref, wa_ref, wb0_ref, wb1_ref,
                          qa_ref, ka_ref, va_ref, hq_ref, hk_ref, hv_ref, hg_ref):
    h = _rms(x_ref[...], pre_ref[...]).astype(BF16)
    za = _dot(h, wa_ref[...])
    cos, slo, shi = cos_ref[...], slo_ref[...], shi_ref[...]
    outs = (qa_ref, ka_ref, va_ref)
    for s in range(3 * N_PAIRS):
        kind, hp = divmod(s, N_PAIRS)
        t = za[:, s * LANES:(s + 1) * LANES]
        if kind < 2:
            t = _rope_slab(t, cos, slo, shi)
        if kind == 0:
            t = t * A_SCALE
        outs[kind][:, hp * LANES:(hp + 1) * LANES] = t
    q, k, v, g = _hgrn_gates(h, wb0_ref, wb1_ref, _lower_bound(lbl_ref[...], layer))
    hq_ref[...] = q
    hk_ref[...] = k
    hv_ref[...] = v
    hg_ref[...] = g


def _inproj_sample(x, pre, tabs, lbl, w_in, layer):
    n = x.shape[0]
    full = lambda *shape: pl.BlockSpec(shape, lambda i: (0,) * len(shape))
    sa = jax.ShapeDtypeStruct((n, A_WIDTH), F32)
    sb = jax.ShapeDtypeStruct((n, B_WIDTH), F32)
    return pl.pallas_call(
        functools.partial(_inproj_sample_kernel, layer),
        grid=(1,),
        in_specs=[full(n, D_MODEL), _vec(layer), full(n, LANES), full(n, LANES), full(n, LANES),
                  full(DEPTH, B_WIDTH)] + [_layer_block((D_MODEL, W_BLOCK), layer, c) for c in range(3)],
        out_specs=[full(n, A_WIDTH)] * 3 + [full(n, B_WIDTH)] * 4,
        out_shape=[sa] * 3 + [sb] * 4,
        compiler_params=_params(("arbitrary",)),
        name="inproj_sample",
    )(x, pre, *tabs, lbl, w_in, w_in, w_in)


def _attn_block(q, k, v_ext, bias):
    lane = lax.broadcasted_iota(jnp.int32, (A_SPAN, LANES), 1)
    first = lane < A_HEAD_DIM
    zero = jnp.zeros_like(q)
    q2 = jnp.concatenate([jnp.where(first, q, zero), jnp.where(first, zero, q)], axis=0)
    s = _dot_nt(q2, k) + bias
    m = jnp.max(jnp.maximum(s[:, :A_SPAN], s[:, A_SPAN:]), axis=-1, keepdims=True)
    oe = _dot(jnp.exp2(s - m).astype(BF16), v_ext)
    m = jnp.broadcast_to(m, (2 * A_SPAN, LANES))
    pick = lambda t: jnp.where(first, t[:A_SPAN], t[A_SPAN:])
    return pick(oe[:, :LANES]), pick(m), pick(oe[:, LANES:])


def _attn_prompt_kernel(q1_ref, q4_ref, q16_ref, k1_ref, k4_ref, k16_ref, v1_ref, v4_ref, v16_ref,
                        o_ref, ks1, ks4, ks16, vs1, vs4, vs16, bias_scr, o_scr, m_scr, d_scr):
    j = pl.program_id(1)
    q_refs = (q1_ref, q4_ref, q16_ref)
    k_in = (k1_ref, k4_ref, k16_ref)
    v_in = (v1_ref, v4_ref, v16_ref)
    k_scr = (ks1, ks4, ks16)
    v_scr = (vs1, vs4, vs16)

    @pl.when(j == 0)
    def _():
        for scr in k_scr:
            scr[:, 0:A_SPAN, :] = jnp.zeros((scr.shape[0], A_SPAN, LANES), BF16)
        for scr in v_scr:
            scr[:, 0:A_SPAN, 0:LANES] = jnp.zeros((scr.shape[0], A_SPAN, LANES), BF16)
            scr[:, :, LANES:] = jnp.ones((scr.shape[0], scr.shape[1], LANES), BF16)
        qi = lax.broadcasted_iota(jnp.int32, (2 * A_SPAN, 2 * A_SPAN), 0) % A_SPAN
        ki = lax.broadcasted_iota(jnp.int32, (2 * A_SPAN, 2 * A_SPAN), 1)
        band = jnp.logical_and(ki >= qi, ki <= qi + A_SPAN)
        bias_scr[1] = jnp.where(band, 0.0, -jnp.inf)
        bias_scr[0] = jnp.where(jnp.logical_and(band, ki >= A_SPAN), 0.0, -jnp.inf)

    for p, d in enumerate(A_DILATIONS):
        k_scr[p][:, A_SPAN:, :] = k_in[p][...]
        v_scr[p][:, A_SPAN:, 0:LANES] = v_in[p][...]

    def block(p, b):
        d = A_DILATIONS[p]
        per_class = ATTN_TILE // d // A_SPAN
        r = b // per_class
        n = b % per_class
        lo = pl.multiple_of(n * A_SPAN, A_SPAN)
        has_prev = jnp.logical_or(j > 0, n > 0).astype(jnp.int32)
        acc, m, den = _attn_block(q_refs[p][r, pl.ds(lo, A_SPAN), :], k_scr[p][r, pl.ds(lo, 2 * A_SPAN), :],
                                  v_scr[p][r, pl.ds(lo, 2 * A_SPAN), :], bias_scr[has_prev])
        rows = pl.ds(n * (A_SPAN * d) + r, A_SPAN, stride=d) if d > 1 else pl.ds(lo, A_SPAN)
        return acc, m, den, rows

    n_iter = ATTN_TILE // A_SPAN // A_UNROLL
    for p in range(1, len(A_DILATIONS)):

        def dilated(i, carry, p=p):
            for u in range(A_UNROLL):
                acc, m, den, rows = block(p, i * A_UNROLL + u)
                o_scr[p - 1, rows, :] = acc
                m_scr[p - 1, rows, :] = m
                d_scr[p - 1, rows, :] = den
            return carry

        lax.fori_loop(0, n_iter, dilated, 0)

    def dense(i, carry):
        for u in range(A_UNROLL):
            acc0, m0, den0, rows = block(0, i * A_UNROLL + u)
            m1, m2 = m_scr[0, rows, :], m_scr[1, rows, :]
            top = jnp.maximum(jnp.maximum(m0, m1), m2)
            w0, w1, w2 = jnp.exp2(m0 - top), jnp.exp2(m1 - top), jnp.exp2(m2 - top)
            num = w0 * acc0 + w1 * o_scr[0, rows, :] + w2 * o_scr[1, rows, :]
            den = w0 * den0 + w1 * d_scr[0, rows, :] + w2 * d_scr[1, rows, :]
            o_ref[rows, :] = (num / den).astype(BF16)
        return carry

    lax.fori_loop(0, n_iter, dense, 0)

    for p, d in enumerate(A_DILATIONS):
        tail = ATTN_TILE // d
        k_scr[p][:, 0:A_SPAN, :] = k_scr[p][:, tail:tail + A_SPAN, :]
        v_scr[p][:, 0:A_SPAN, 0:LANES] = v_scr[p][:, tail:tail + A_SPAN, 0:LANES]


def _attn_prompt(q1, q4, q16, k1, k4, k16, v1, v4, v16):
    s_len = q1.shape[1]
    t = ATTN_TILE
    b1 = pl.BlockSpec((None, 1, t, LANES), lambda hp, j: (hp, 0, j, 0))
    b4 = pl.BlockSpec((None, 4, t // 4, LANES), lambda hp, j: (hp, 0, j, 0))
    b16 = pl.BlockSpec((None, 16, t // 16, LANES), lambda hp, j: (hp, 0, j, 0))
    as4 = lambda a: a.reshape(N_PAIRS, 1, s_len, LANES)
    scr = lambda d, width: pltpu.VMEM((d, A_SPAN + t // d, width), BF16)
    n_dil = len(A_DILATIONS) - 1
    return pl.pallas_call(
        _attn_prompt_kernel,
        grid=(N_PAIRS, s_len // t),
        in_specs=[b1, b4, b16] * 3,
        out_specs=pl.BlockSpec((t, LANES), lambda hp, j: (j, hp)),
        out_shape=jax.ShapeDtypeStruct((s_len, A_WIDTH), BF16),
        scratch_shapes=[scr(d, LANES) for d in A_DILATIONS] + [scr(d, 2 * LANES) for d in A_DILATIONS]
                       + [pltpu.VMEM((2, 2 * A_SPAN, 2 * A_SPAN), F32)] + [pltpu.VMEM((n_dil, t, LANES), F32)] * 3,
        compiler_params=_params(("arbitrary", "arbitrary")),
        name="attn_prompt",
    )(as4(q1), q4, q16, as4(k1), k4, k16, as4(v1), v4, v16)


def _decode_kernel(bb, w_c, q_ref, k_ref, v_ref, kt_ref, vt_ref,
                   hq_ref, hk_ref, hv_ref, hg_ref, st_ref, nrm_ref, *rest):
    a_ref, bn_ref, so_ref = rest[-3:]
    n_pat = len(A_DILATIONS)
    dist = w_c - lax.broadcasted_iota(jnp.int32, (A_HEADS, w_c), 1)
    cnt = jnp.zeros((A_HEADS, w_c), F32)
    for d in A_DILATIONS:
        cnt = cnt + jnp.logical_and(dist % d == 0, dist <= A_SPAN * d).astype(F32)
    used = cnt > 0.0
    head_s = lax.broadcasted_iota(jnp.int32, (A_HEADS, w_c), 0)
    head_o = lax.broadcasted_iota(jnp.int32, (A_HEADS, A_HEAD_DIM), 0)
    for b in range(bb):
        q = q_ref[b]
        k_new, v_new = k_ref[b], v_ref[b]
        q16 = q.astype(BF16)
        s_new = jnp.sum(q * k_new, axis=-1, keepdims=True)
        s = jnp.zeros((A_HEADS, w_c), F32)
        for h in range(A_HEADS):
            s = jnp.where(head_s == h, _dot(q16, kt_ref[b, h].astype(BF16)), s)
        m = jnp.maximum(jnp.max(jnp.where(used, s, -jnp.inf), axis=-1, keepdims=True), s_new)
        w = jnp.where(used, jnp.exp(s - m), 0.0) * cnt
        p_new = n_pat * jnp.exp(s_new - m)
        den = jnp.sum(w, axis=-1, keepdims=True) + p_new
        acc = p_new * v_new
        w16 = w.astype(BF16)
        for h in range(A_HEADS):
            acc = acc + jnp.where(head_o == h, _dot_nt(w16, vt_ref[b, h].astype(BF16)), 0.0)
        a_ref[b] = acc / den
        v_rows = hv_ref[b]
        packed = jnp.concatenate([hq_ref[b], hk_ref[b], jnp.exp(hg_ref[b]),
                                  jnp.zeros((LANES - 3 * B_HEADS, B_HEAD_DIM), F32)], axis=0)
        cols = packed.T
        outs = []
        for hd in range(B_HEADS):
            col = lambda i: cols[:, i * B_HEADS + hd:i * B_HEADS + hd + 1]
            st = col(2) * st_ref[b, hd] + col(1) * v_rows[hd:hd + 1, :]
            so_ref[b, hd] = st
            o = jnp.sum(col(0) * st, axis=0, keepdims=True)
            outs.append(_rms(o, nrm_ref[hd:hd + 1, :]))
        bn_ref[b] = jnp.concatenate(outs, axis=0)


def _decode(layer, q, k, v, cache_kt, cache_vt, hq, hk, hv, hg, state, nrm, states_out, bb):
    n = q.shape[0]
    w_c = cache_kt.shape[-1]
    assert w_c >= A_SPAN * max(A_DILATIONS)
    a_heads = lambda a: a.reshape(n, A_HEADS, A_HEAD_DIM)
    heads = lambda a: a.reshape(n, B_HEADS, B_HEAD_DIM)
    a_spec = pl.BlockSpec((bb, A_HEADS, A_HEAD_DIM), lambda i: (i, 0, 0))
    headspec = pl.BlockSpec((bb, B_HEADS, B_HEAD_DIM), lambda i: (i, 0, 0))

    c_spec = pl.BlockSpec((None, bb, A_HEADS, A_HEAD_DIM, w_c), lambda i: (layer, i, 0, 0, 0))
    st_spec = pl.BlockSpec((None, bb, B_HEADS, B_HEAD_DIM, B_HEAD_DIM), lambda i: (layer, i, 0, 0, 0))
    in_specs = [a_spec] * 3 + [c_spec] * 2 + [headspec] * 4 + [st_spec, _layer_block((B_HEADS, B_HEAD_DIM), layer)]
    args = [a_heads(q), a_heads(k), a_heads(v), cache_kt, cache_vt, heads(hq), heads(hk), heads(hv), heads(hg),
            state, nrm.reshape(DEPTH, B_HEADS, B_HEAD_DIM)]
    aliases = {}
    if states_out is not None:
        aliases = {len(args): 2}
        in_specs.append(pl.BlockSpec(memory_space=pl.ANY))
        args.append(states_out)
    a_out, bn, st = pl.pallas_call(
        functools.partial(_decode_kernel, bb, w_c),
        grid=(n // bb,),
        in_specs=in_specs,
        out_specs=[a_spec, headspec, st_spec],
        out_shape=[jax.ShapeDtypeStruct((n, A_HEADS, A_HEAD_DIM), F32),
                   jax.ShapeDtypeStruct((n, B_HEADS, B_HEAD_DIM), F32),
                   jax.ShapeDtypeStruct(state.shape, F32)],
        input_output_aliases=aliases,
        compiler_params=_params(("parallel",)),
        name="decode",
    )(*args)
    return a_out.reshape(n, A_WIDTH).astype(BF16), bn.reshape(n, B_WIDTH), st


def _hgrn_prompt_kernel(th, q_ref, k_ref, v_ref, g_ref, nrm_ref, o_ref, st_ref, state_scr, b_scr, a_scr, part_scr):
    j = pl.program_id(1)

    @pl.when(j == 0)
    def _():
        state_scr[...] = jnp.zeros_like(state_scr)

    rin = lax.broadcasted_iota(jnp.int32, (th, B_HEAD_DIM), 0) & (B_CHUNK - 1)
    b = g_ref[...] * LOG2_E
    shift = 1
    while shift < B_CHUNK:
        b = b + jnp.where(rin >= shift, pltpu.roll(b, shift, axis=0), 0.0)
        shift *= 2
    b_scr[...] = b

    row = lax.broadcasted_iota(jnp.int32, (B_CHUNK, B_HEAD_DIM), 0)
    a_t = lax.broadcasted_iota(jnp.int32, (B_CHUNK, B_CHUNK), 0)
    a_s = lax.broadcasted_iota(jnp.int32, (B_CHUNK, B_CHUNK), 1)
    levels = []
    half = B_CHUNK // 2
    while half >= B_SUB:
        seg = 2 * half
        pair = jnp.logical_and(a_t // seg == a_s // seg,
                               jnp.logical_and(a_t % seg >= half, a_s % seg < half))
        levels.append((half, row % seg >= half, pair))
        half //= 2
    n_grp = B_CHUNK // 8
    sub_row = lax.broadcasted_iota(jnp.int32, (n_grp, 8, B_HEAD_DIM), 1) % B_SUB
    nrm = nrm_ref[...]

    def intra_chunk(q, k, v, bc):
        a = jnp.zeros((B_CHUNK, B_CHUNK), F32)
        for half, upper, pair in levels:
            seg = 2 * half
            b_mid = jnp.concatenate(
                [jnp.broadcast_to(bc[s0 + half - 1:s0 + half, :], (seg, B_HEAD_DIM))
                 for s0 in range(0, B_CHUNK, seg)], axis=0)
            d = bc - b_mid
            x = (jnp.where(upper, q, k) * jnp.exp2(jnp.where(upper, d, -d))).astype(BF16)
            a = jnp.where(pair, _dot_nt(x, x), a)
        q3, k3, v3, b3 = (t.reshape(n_grp, 8, B_HEAD_DIM) for t in (q, k, v, bc))
        o3 = jnp.sum(q3 * k3, axis=-1, keepdims=True) * v3
        for delta in range(1, B_SUB):
            kd, bd, vd = (pltpu.roll(t, delta, axis=1) for t in (k3, b3, v3))
            w = jnp.sum(q3 * kd * jnp.exp2(b3 - bd), axis=-1, keepdims=True)
            o3 = o3 + jnp.where(sub_row >= delta, w, 0.0) * vd
        return a, o3.reshape(B_CHUNK, B_HEAD_DIM)

    def chunk_rows(grp, u):
        return pl.ds(pl.multiple_of((grp * B_UNROLL + u) * B_CHUNK, B_CHUNK), B_CHUNK)

    def front(grp):
        st = state_scr[...]
        for u in range(B_UNROLL):
            rows = chunk_rows(grp, u)
            q, k, v, bc = q_ref[rows, :], k_ref[rows, :], v_ref[rows, :], b_scr[rows, :]
            o = _dot_nt((q * jnp.exp2(bc)).astype(BF16), st.astype(BF16))
            b_last = bc[B_CHUNK - 1:B_CHUNK, :]
            k_dec = (k * jnp.exp2(b_last - bc)).astype(BF16)
            st = st * jnp.exp2(b_last) + _dot_tn(v.astype(BF16), k_dec)
            a, o3 = intra_chunk(q, k, v, bc)
            a_scr[u] = a.astype(BF16)
            part_scr[u] = o + o3
        state_scr[...] = st

    def back(grp):
        for u in range(B_UNROLL):
            rows = chunk_rows(grp, u)
            o = part_scr[u] + _dot(a_scr[u], v_ref[rows, :].astype(BF16))
            o_ref[rows, :] = _rms(o, nrm)

    n_groups = th // (B_CHUNK * B_UNROLL)
    front(0)

    def steady(grp, carry):
        back(grp - 1)
        front(grp)
        return carry

    lax.fori_loop(1, n_groups, steady, 0)
    back(n_groups - 1)

    @pl.when(j == pl.num_programs(1) - 1)
    def _():
        st_ref[...] = state_scr[...].T


def _hgrn_prompt(q, k, v, g, nrm, layer, th):
    s_len = q.shape[1]
    blk = pl.BlockSpec((None, th, B_HEAD_DIM), lambda h, j: (h, j, 0))
    return pl.pallas_call(
        functools.partial(_hgrn_prompt_kernel, th),
        grid=(B_HEADS, s_len // th),
        in_specs=[blk] * 4 + [pl.BlockSpec((None, 1, B_HEAD_DIM), lambda h, j: (layer, 0, h))],
        out_specs=[pl.BlockSpec((th, B_HEAD_DIM), lambda h, j: (j, h)),
                   pl.BlockSpec((None, B_HEAD_DIM, B_HEAD_DIM), lambda h, j: (h, 0, 0))],
        out_shape=[jax.ShapeDtypeStruct((s_len, B_WIDTH), F32),
                   jax.ShapeDtypeStruct((B_HEADS, B_HEAD_DIM, B_HEAD_DIM), F32)],
        scratch_shapes=[pltpu.VMEM((B_HEAD_DIM, B_HEAD_DIM), F32), pltpu.VMEM((th, B_HEAD_DIM), F32),
                        pltpu.VMEM((B_UNROLL, B_CHUNK, B_CHUNK), BF16),
                        pltpu.VMEM((B_UNROLL, B_CHUNK, B_HEAD_DIM), F32)],
        compiler_params=_params(("arbitrary", "arbitrary")),
        name="hgrn_prompt",
    )(q, k, v, g, nrm)


def _mixout_kernel(x_ref, a_ref, bn_ref, pre_ref, post_ref, wg0_ref, wg1_ref, wa_ref, wb_ref, wo_ref, o_ref):
    x = x_ref[...]
    h = _rms(x, pre_ref[...]).astype(BF16)
    z0 = _dot(h, wg0_ref[...])
    z1 = _dot(h, wg1_ref[...])
    split = W_BLOCK - B_WIDTH
    g_b = z0[:, :B_WIDTH]
    gate_a = jnp.concatenate([z0[:, B_WIDTH:], z1[:, :split]], axis=1)
    gate_b = z1[:, split:]
    b_out = (bn_ref[...] * (g_b * jax.nn.sigmoid(g_b))).astype(BF16)
    a_proj = _dot(a_ref[...], wa_ref[...])
    b_proj = _dot(b_out, wb_ref[...])
    mix = (jax.nn.sigmoid(gate_a) * a_proj + jax.nn.sigmoid(gate_b) * b_proj).astype(BF16)
    o_ref[...] = x + _rms(_dot(mix, wo_ref[...]), post_ref[...])


def _mixout(x, a_out, bn, layer, pre, post, w_in, wa, wb, wo, tm):
    n = x.shape[0]
    row = lambda width: pl.BlockSpec((tm, width), lambda i: (i, 0))
    return pl.pallas_call(
        _mixout_kernel,
        grid=(n // tm,),
        in_specs=[row(D_MODEL), row(A_WIDTH), row(B_WIDTH), _vec(layer), _vec(layer),
                  _layer_block((D_MODEL, W_BLOCK), layer, 3), _layer_block((D_MODEL, W_BLOCK), layer, 4),
                  _layer_block((A_WIDTH, D_MODEL), layer), _layer_block((B_WIDTH, D_MODEL), layer),
                  _layer_block((D_MODEL, D_MODEL), layer)],
        out_specs=row(D_MODEL),
        out_shape=jax.ShapeDtypeStruct((n, D_MODEL), F32),
        compiler_params=_params(("parallel",)),
        name="mixout",
    )(x, a_out, bn, pre, post, w_in, w_in, wa, wb, wo)


def _rope_tables(pos):
    half = ROPE_DIM // 2
    inv = ROPE_THETA ** (-jnp.arange(half, dtype=F32) / half)
    ang = pos.astype(F32)[:, None] * inv[None, :]
    cos, sin = jnp.cos(ang), jnp.sin(ang)
    t = pos.shape[0]
    one = jnp.ones((t, A_HEAD_DIM - ROPE_DIM), F32)
    zero = jnp.zeros_like(one)
    z8 = jnp.zeros((t, half), F32)
    per_head = lambda parts: jnp.tile(jnp.concatenate(parts, axis=-1), (1, LANES // A_HEAD_DIM))
    return per_head([cos, cos, one]), per_head([-sin, z8, zero]), per_head([z8, sin, zero])


def kernel(x_prompt, x_sample, cache_k, cache_v, state_hgrn, ffn1_norm_pre, ffn1_norm_post, ffn1_w_gate, ffn1_w_up, ffn1_w_down, mix_norm_pre, mix_norm_post, w_in, hgrn_lb_logits, hgrn_out_norm, w_a_out, w_b_out, w_mix_out, ffn2_norm_pre, ffn2_norm_post, ffn2_w_gate, ffn2_w_up, ffn2_w_down):
    batch, s_len, _ = x_prompt.shape
    n_dec, t_dec, _ = x_sample.shape
    assert batch == 1 and t_dec == 1 and s_len % ATTN_TILE == 0
    tm = 512
    tm_in = 256
    th = min(4096, s_len)
    bb = 2
    cache_kt = jnp.transpose(cache_k, (0, 1, 3, 4, 2))
    cache_vt = jnp.transpose(cache_v, (0, 1, 3, 4, 2))
    yp = x_prompt.reshape(s_len, D_MODEL)
    ys = x_sample.reshape(n_dec, D_MODEL)
    tabs_p = _rope_tables(jnp.arange(s_len))
    tabs_s = _rope_tables(jnp.full((n_dec,), PAST_LEN))
    lbl = hgrn_lb_logits.astype(F32)
    vecs = lambda a: a.reshape(DEPTH, 1, -1).astype(F32)
    w16 = lambda w: w.astype(BF16)
    w_len = min(A_MAX_WINDOW, s_len)
    f1 = (vecs(ffn1_norm_pre), vecs(ffn1_norm_post), w16(ffn1_w_gate), w16(ffn1_w_up), w16(ffn1_w_down))
    f2 = (vecs(ffn2_norm_pre), vecs(ffn2_norm_post), w16(ffn2_w_gate), w16(ffn2_w_up), w16(ffn2_w_down))
    win = w16(w_in)
    m_pre, m_post, nrm = vecs(mix_norm_pre), vecs(mix_norm_post), vecs(hgrn_out_norm)
    mo = (m_pre, m_post, win, w16(w_a_out), w16(w_b_out), w16(w_mix_out))
    kp, vp, sp, ksn, vsn = [], [], [], [], []
    states = None
    for l in range(DEPTH):
        yp = _ffn(yp, l, *f1, tm)
        (q1, q4, q16, k1, k4, k16, v1, v4, v16, kf, vf, hq, hk, hv, hg) = _inproj_prompt(
            yp, m_pre, tabs_p, lbl, win, l, tm_in)
        a_out = _attn_prompt(q1, q4, q16, k1, k4, k16, v1, v4, v16)
        bn, st_p = _hgrn_prompt(hq, hk, hv, hg, nrm, l, th)
        yp = _mixout(yp, a_out, bn, l, *mo, tm)
        yp = _ffn(yp, l, *f2, tm)
        kp.append(kf.reshape(batch, w_len, A_HEADS, A_HEAD_DIM))
        vp.append(vf.reshape(batch, w_len, A_HEADS, A_HEAD_DIM))
        sp.append(st_p.reshape(batch, B_HEADS, B_HEAD_DIM, B_HEAD_DIM))

        ys = _ffn(ys, l, *f1, n_dec)
        qa, ka, va, sq, sk, sv, sg = _inproj_sample(ys, m_pre, tabs_s, lbl, win, l)
        a_s, bn_s, states = _decode(l, qa, ka, va, cache_kt, cache_vt, sq, sk, sv, sg, state_hgrn, nrm, states, bb)
        ys = _mixout(ys, a_s, bn_s, l, *mo, n_dec)
        ys = _ffn(ys, l, *f2, n_dec)
        ksn.append(ka.reshape(n_dec, t_dec, A_HEADS, A_HEAD_DIM))
        vsn.append(va.reshape(n_dec, t_dec, A_HEADS, A_HEAD_DIM))
    return (yp.reshape(batch, s_len, D_MODEL), ys.reshape(n_dec, t_dec, D_MODEL),
            jnp.stack(kp), jnp.stack(vp), jnp.stack(sp), jnp.stack(ksn), jnp.stack(vsn), states)
```

```python
import functools

import jax
import jax.numpy as jnp
import numpy as np
from jax import lax
from jax.experimental import pallas as pl
from jax.experimental.pallas import tpu as pltpu

F32 = jnp.float32
BF16 = jnp.bfloat16

D_MODEL = 1024
DEPTH = 2
PAST_LEN = 16384
A_HEADS = 8
A_HEAD_DIM = 64
A_WIDTH = A_HEADS * A_HEAD_DIM
A_DILATIONS = (1, 4, 16)
A_SPAN = 128
A_MAX_WINDOW = 2048
A_SCALE = A_HEAD_DIM ** -0.5
ROPE_THETA = 500000.0
ROPE_DIM = A_HEAD_DIM // 4
B_HEADS = 8
B_HEAD_DIM = 128
B_WIDTH = B_HEADS * B_HEAD_DIM
B_CHUNK = 64
B_SUB = 4
B_UNROLL = 4
B_SCALE = B_HEAD_DIM ** -0.5
D_FF = 2816
EPS = 1e-6
LOG2_E = 1.4426950408889634

W_BLOCK = 3 * A_WIDTH
assert 3 * B_WIDTH == 2 * W_BLOCK and B_WIDTH + 2 * D_MODEL == 2 * W_BLOCK
LANES = 128
N_PAIRS = A_WIDTH // LANES
ATTN_TILE = A_SPAN * max(A_DILATIONS)
A_UNROLL = 8
VMEM_LIMIT = 56 * 1024 * 1024


def _params(sem, vmem=VMEM_LIMIT):
    return pltpu.CompilerParams(dimension_semantics=sem, vmem_limit_bytes=vmem)


def _resident(shape):
    nd = len(shape)
    return pl.BlockSpec(shape, lambda *_: (0,) * nd, pipeline_mode=pl.Buffered(1))


def _layer_block(shape, layer, col=0):
    index = (layer,) + (0,) * (len(shape) - 1) + (col,)
    return pl.BlockSpec((None,) + tuple(shape), lambda *_: index, pipeline_mode=pl.Buffered(1))


def _vec(layer):
    return _layer_block((1, D_MODEL), layer)


def _rms(x, g):
    y = x * lax.rsqrt(jnp.mean(x * x, axis=-1, keepdims=True) + EPS)
    return y * g


def _dot(a, b):
    return jnp.dot(a, b, preferred_element_type=F32)


def _dot_nt(a, b):
    return lax.dot_general(a, b, (((1,), (1,)), ((), ())), preferred_element_type=F32)


def _dot_tn(a, b):
    return lax.dot_general(a, b, (((0,), (0,)), ((), ())), preferred_element_type=F32)


def _ffn_kernel(x_ref, pre_ref, post_ref, wg_ref, wu_ref, wd_ref, o_ref):
    x = x_ref[...]
    h = _rms(x, pre_ref[...]).astype(BF16)
    g = _dot(h, wg_ref[...])
    u = _dot(h, wu_ref[...])
    a = (g * jax.nn.sigmoid(g) * u).astype(BF16)
    y = _dot(a, wd_ref[...])
    o_ref[...] = x + 0.5 * _rms(y, post_ref[...])


def _ffn(x, layer, pre, post, wg, wu, wd, tm):
    n = x.shape[0]
    row = pl.BlockSpec((tm, D_MODEL), lambda i: (i, 0))
    return pl.pallas_call(
        _ffn_kernel,
        grid=(n // tm,),
        in_specs=[row, _vec(layer), _vec(layer), _layer_block((D_MODEL, D_FF), layer),
                  _layer_block((D_MODEL, D_FF), layer), _layer_block((D_FF, D_MODEL), layer)],
        out_specs=row,
        out_shape=jax.ShapeDtypeStruct((n, D_MODEL), F32),
        compiler_params=_params(("parallel",)),
        name="ffn",
    )(x, pre, post, wg, wu, wd)


def _lower_bound(logits, layer):
    e = jnp.exp(logits - jnp.max(logits, axis=0, keepdims=True))
    sm = e / jnp.sum(e, axis=0, keepdims=True)
    lb = jnp.zeros((1, B_WIDTH), F32)
    for i in range(1, layer + 1):
        lb = lb + sm[i:i + 1, :]
    return lb


def _rope_slab(t, cos, sin_lo, sin_hi):
    return t * cos + pltpu.roll(t, LANES - ROPE_DIM // 2, axis=1) * sin_lo + pltpu.roll(t, ROPE_DIM // 2, axis=1) * sin_hi


def _hgrn_gates(h, wb0_ref, wb1_ref, lb):
    z0 = _dot(h, wb0_ref[...])
    z1 = _dot(h, wb1_ref[...])
    split = W_BLOCK - B_WIDTH
    q = z0[:, :B_WIDTH] * B_SCALE
    f_raw = jnp.concatenate([z0[:, B_WIDTH:], z1[:, :split]], axis=1)
    i_raw = z1[:, split:]
    f = lb + (1.0 - lb) * jax.nn.sigmoid(f_raw)
    return q, 1.0 - f, i_raw * jax.nn.sigmoid(i_raw), jnp.log(f)


def _inproj_attn_kernel(tm, x_ref, pre_ref, cos_ref, slo_ref, shi_ref, wa_ref,
                        q1_ref, q4_ref, q16_ref, k1_ref, k4_ref, k16_ref, v1_ref, v4_ref, v16_ref,
                        kf_ref, vf_ref, stage_ref, stage4_ref):
    h = _rms(x_ref[...], pre_ref[...]).astype(BF16)
    za = _dot(h, wa_ref[...])
    cos, slo, shi = cos_ref[...], slo_ref[...], shi_ref[...]
    outs = ((q1_ref, q4_ref, q16_ref), (k1_ref, k4_ref, k16_ref), (v1_ref, v4_ref, v16_ref))
    for s in range(3 * N_PAIRS):
        kind, hp = divmod(s, N_PAIRS)
        t = za[:, s * LANES:(s + 1) * LANES]
        if kind < 2:
            t = _rope_slab(t, cos, slo, shi)
        if kind == 0:
            t = t * (A_SCALE * LOG2_E)
        if kind == 1:
            kf_ref[:, hp * LANES:(hp + 1) * LANES] = t
        if kind == 2:
            vf_ref[:, hp * LANES:(hp + 1) * LANES] = t
        stage_ref[s] = t
        o1, o4, o16 = outs[kind]
        o1[hp] = t.astype(BF16)
        for r in range(4):
            c4 = stage_ref[s, pl.ds(r, tm // 4, stride=4), :]
            o4[hp, r] = c4.astype(BF16)
            stage4_ref[s, r] = c4
        for r in range(4):
            for a in range(4):
                o16[hp, r + 4 * a] = stage4_ref[s, r, pl.ds(a, tm // 16, stride=4), :].astype(BF16)


def _inproj_attn(x, pre, tabs, w_in, layer, tm):
    s_len = x.shape[0]
    w_len = min(A_MAX_WINDOW, s_len)
    first = (s_len - w_len) // tm
    row = pl.BlockSpec((tm, D_MODEL), lambda i: (i, 0))
    tab = pl.BlockSpec((tm, LANES), lambda i: (i, 0))
    l1 = pl.BlockSpec((N_PAIRS, tm, LANES), lambda i: (0, i, 0))
    l4 = pl.BlockSpec((N_PAIRS, 4, tm // 4, LANES), lambda i: (0, 0, i, 0))
    l16 = pl.BlockSpec((N_PAIRS, 16, tm // 16, LANES), lambda i: (0, 0, i, 0))
    win = pl.BlockSpec((tm, A_WIDTH), lambda i: (jnp.maximum(i - first, 0), 0))
    s1 = jax.ShapeDtypeStruct((N_PAIRS, s_len, LANES), BF16)
    s4 = jax.ShapeDtypeStruct((N_PAIRS, 4, s_len // 4, LANES), BF16)
    s16 = jax.ShapeDtypeStruct((N_PAIRS, 16, s_len // 16, LANES), BF16)
    sw = jax.ShapeDtypeStruct((w_len, A_WIDTH), F32)
    return pl.pallas_call(
        functools.partial(_inproj_attn_kernel, tm),
        grid=(s_len // tm,),
        in_specs=[row, _vec(layer), tab, tab, tab, _layer_block((D_MODEL, W_BLOCK), layer, 0)],
        out_specs=[l1, l4, l16, l1, l4, l16, l1, l4, l16, win, win],
        out_shape=[s1, s4, s16, s1, s4, s16, s1, s4, s16, sw, sw],
        scratch_shapes=[pltpu.VMEM((3 * N_PAIRS, tm, LANES), F32), pltpu.VMEM((3 * N_PAIRS, 4, tm // 4, LANES), F32)],
        compiler_params=_params(("arbitrary",)),
        name="inproj_attn",
    )(x, pre, *tabs, w_in)


def _inproj_rec_kernel(layer, x_ref, pre_ref, lbl_ref, wb0_ref, wb1_ref, hq_ref, hk_ref, hv_ref, hg_ref):
    h = _rms(x_ref[...], pre_ref[...]).astype(BF16)
    q, k, v, g = _hgrn_gates(h, wb0_ref, wb1_ref, _lower_bound(lbl_ref[...], layer))
    for hd in range(B_HEADS):
        sl = slice(hd * B_HEAD_DIM, (hd + 1) * B_HEAD_DIM)
        hq_ref[hd] = q[:, sl]
        hk_ref[hd] = k[:, sl]
        hv_ref[hd] = v[:, sl]
        hg_ref[hd] = g[:, sl]


def _inproj_rec(x, pre, lbl, w_in, layer, tm):
    s_len = x.shape[0]
    row = pl.BlockSpec((tm, D_MODEL), lambda i: (i, 0))
    hd = pl.BlockSpec((B_HEADS, tm, B_HEAD_DIM), lambda i: (0, i, 0))
    sh = jax.ShapeDtypeStruct((B_HEADS, s_len, B_HEAD_DIM), F32)
    return pl.pallas_call(
        functools.partial(_inproj_rec_kernel, layer),
        grid=(s_len // tm,),
        in_specs=[row, _vec(layer), _resident((DEPTH, B_WIDTH)),
                  _layer_block((D_MODEL, W_BLOCK), layer, 1), _layer_block((D_MODEL, W_BLOCK), layer, 2)],
        out_specs=[hd] * 4,
        out_shape=[sh] * 4,
        compiler_params=_params(("parallel",)),
        name="inproj_rec",
    )(x, pre, lbl, w_in, w_in)


def _inproj_sample_kernel(layer, x_ref, pre_ref, cos_ref, slo_ref, shi_ref, lbl_ref, wa_ref, wb0_ref, wb1_ref,
                          qa_ref, ka_ref, va_ref, hq_ref, hk_ref, hv_ref, hg_ref):
    h = _rms(x_ref[...], pre_ref[...]).astype(BF16)
    za = _dot(h, wa_ref[...])
    cos, slo, shi = cos_ref[...], slo_ref[...], shi_ref[...]
    outs = (qa_ref, ka_ref, va_ref)
    for s in range(3 * N_PAIRS):
        kind, hp = divmod(s, N_PAIRS)
        t = za[:, s * LANES:(s + 1) * LANES]
        if kind < 2:
            t = _rope_slab(t, cos, slo, shi)
        if kind == 0:
            t = t * A_SCALE
        outs[kind][:, hp * LANES:(hp + 1) * LANES] = t
    q, k, v, g = _hgrn_gates(h, wb0_ref, wb1_ref, _lower_bound(lbl_ref[...], layer))
    hq_ref[...] = q
    hk_ref[...] = k
    hv_ref[...] = v
    hg_ref[...] = g


def _inproj_sample(x, pre, tabs, lbl, w_in, layer):
    n = x.shape[0]
    full = lambda *shape: pl.BlockSpec(shape, lambda i: (0,) * len(shape))
    sa = jax.ShapeDtypeStruct((n, A_WIDTH), F32)
    sb = jax.ShapeDtypeStruct((n, B_WIDTH), F32)
    return pl.pallas_call(
        functools.partial(_inproj_sample_kernel, layer),
        grid=(1,),
        in_specs=[full(n, D_MODEL), _vec(layer), full(n, LANES), full(n, LANES), full(n, LANES),
                  full(DEPTH, B_WIDTH)] + [_layer_block((D_MODEL, W_BLOCK), layer, c) for c in range(3)],
        out_specs=[full(n, A_WIDTH)] * 3 + [full(n, B_WIDTH)] * 4,
        out_shape=[sa] * 3 + [sb] * 4,
        compiler_params=_params(("arbitrary",)),
        name="inproj_sample",
    )(x, pre, *tabs, lbl, w_in, w_in, w_in)


def _attn_scores(q, k, bias):
    first = lax.broadcasted_iota(jnp.int32, (A_SPAN, LANES), 1) < A_HEAD_DIM
    zero = jnp.zeros_like(q)
    q2 = jnp.concatenate([jnp.where(first, q, zero), jnp.where(first, zero, q)], axis=0)
    s = _dot_nt(q2, k) + bias
    m = jnp.max(jnp.maximum(s[:, :A_SPAN], s[:, A_SPAN:]), axis=-1, keepdims=True)
    return jnp.exp2(s - m).astype(BF16), jnp.broadcast_to(m, (2 * A_SPAN, LANES))


def _attn_values(p, m, v_ext):
    first = lax.broadcasted_iota(jnp.int32, (A_SPAN, LANES), 1) < A_HEAD_DIM
    oe = _dot(p, v_ext)
    pick = lambda t: jnp.where(first, t[:A_SPAN], t[A_SPAN:])
    return pick(oe[:, :LANES]), pick(m), pick(oe[:, LANES:])


def _attn_prompt_kernel(q1_ref, q4_ref, q16_ref, k1_ref, k4_ref, k16_ref, v1_ref, v4_ref, v16_ref,
                        o_ref, ks1, ks4, ks16, vs1, vs4, vs16, bias_scr, o_scr, m_scr, d_scr, p_scr, mx_scr):
    j = pl.program_id(1)
    q_refs = (q1_ref, q4_ref, q16_ref)
    k_in = (k1_ref, k4_ref, k16_ref)
    v_in = (v1_ref, v4_ref, v16_ref)
    k_scr = (ks1, ks4, ks16)
    v_scr = (vs1, vs4, vs16)

    @pl.when(j == 0)
    def _():
        for scr in k_scr:
            scr[:, 0:A_SPAN, :] = jnp.zeros((scr.shape[0], A_SPAN, LANES), BF16)
        for scr in v_scr:
            scr[:, 0:A_SPAN, 0:LANES] = jnp.zeros((scr.shape[0], A_SPAN, LANES), BF16)
            scr[:, :, LANES:] = jnp.ones((scr.shape[0], scr.shape[1], LANES), BF16)
        qi = lax.broadcasted_iota(jnp.int32, (2 * A_SPAN, 2 * A_SPAN), 0) % A_SPAN
        ki = lax.broadcasted_iota(jnp.int32, (2 * A_SPAN, 2 * A_SPAN), 1)
        band = jnp.logical_and(ki >= qi, ki <= qi + A_SPAN)
        bias_scr[1] = jnp.where(band, 0.0, -jnp.inf)
        bias_scr[0] = jnp.where(jnp.logical_and(band, ki >= A_SPAN), 0.0, -jnp.inf)

    for p, d in enumerate(A_DILATIONS):
        k_scr[p][:, A_SPAN:, :] = k_in[p][...]
        v_scr[p][:, A_SPAN:, 0:LANES] = v_in[p][...]

    def locate(p, grp, u):
        per_class = ATTN_TILE // A_DILATIONS[p] // A_SPAN
        b = grp * A_UNROLL + u
        n = b % per_class
        return b // per_class, pl.multiple_of(n * A_SPAN, A_SPAN), n

    def front(p, grp):
        for u in range(A_UNROLL):
            r, lo, n = locate(p, grp, u)
            has_prev = jnp.logical_or(j > 0, n > 0).astype(jnp.int32)
            p_scr[u], mx_scr[u] = _attn_scores(q_refs[p][r, pl.ds(lo, A_SPAN), :],
                                               k_scr[p][r, pl.ds(lo, 2 * A_SPAN), :], bias_scr[has_prev])

    def back(p, grp):
        d = A_DILATIONS[p]
        for u in range(A_UNROLL):
            r, lo, n = locate(p, grp, u)
            acc, m, den = _attn_values(p_scr[u], mx_scr[u], v_scr[p][r, pl.ds(lo, 2 * A_SPAN), :])
            if d > 1:
                rows = pl.ds(n * (A_SPAN * d) + r, A_SPAN, stride=d)
                o_scr[p - 1, rows, :] = acc
                m_scr[p - 1, rows, :] = m
                d_scr[p - 1, rows, :] = den
            else:
                rows = pl.ds(lo, A_SPAN)
                m1, m2 = m_scr[0, rows, :], m_scr[1, rows, :]
                top = jnp.maximum(jnp.maximum(m, m1), m2)
                w0, w1, w2 = jnp.exp2(m - top), jnp.exp2(m1 - top), jnp.exp2(m2 - top)
                num = w0 * acc + w1 * o_scr[0, rows, :] + w2 * o_scr[1, rows, :]
                tot = w0 * den + w1 * d_scr[0, rows, :] + w2 * d_scr[1, rows, :]
                o_ref[rows, :] = (num / tot).astype(BF16)

    n_grp = ATTN_TILE // A_SPAN // A_UNROLL
    order = tuple(range(1, len(A_DILATIONS))) + (0,)
    n_dyn = n_grp + jnp.minimum(j, 0)
    front(order[0], 0)
    for idx, p in enumerate(order):

        def steady(grp, carry, p=p):
            back(p, grp - 1)
            front(p, grp)
            return carry

        lax.fori_loop(1, n_dyn, steady, 0)
        back(p, n_grp - 1)
        if idx + 1 < len(order):
            front(order[idx + 1], 0)

    for p, d in enumerate(A_DILATIONS):
        tail = ATTN_TILE // d
        k_scr[p][:, 0:A_SPAN, :] = k_scr[p][:, tail:tail + A_SPAN, :]
        v_scr[p][:, 0:A_SPAN, 0:LANES] = v_scr[p][:, tail:tail + A_SPAN, 0:LANES]


def _attn_prompt(q1, q4, q16, k1, k4, k16, v1, v4, v16):
    s_len = q1.shape[1]
    t = ATTN_TILE
    b1 = pl.BlockSpec((None, 1, t, LANES), lambda hp, j: (hp, 0, j, 0))
    b4 = pl.BlockSpec((None, 4, t // 4, LANES), lambda hp, j: (hp, 0, j, 0))
    b16 = pl.BlockSpec((None, 16, t // 16, LANES), lambda hp, j: (hp, 0, j, 0))
    as4 = lambda a: a.reshape(N_PAIRS, 1, s_len, LANES)
    scr = lambda d, width: pltpu.VMEM((d, A_SPAN + t // d, width), BF16)
    n_dil = len(A_DILATIONS) - 1
    return pl.pallas_call(
        _attn_prompt_kernel,
        grid=(N_PAIRS, s_len // t),
        in_specs=[b1, b4, b16] * 3,
        out_specs=pl.BlockSpec((t, LANES), lambda hp, j: (j, hp)),
        out_shape=jax.ShapeDtypeStruct((s_len, A_WIDTH), BF16),
        scratch_shapes=[scr(d, LANES) for d in A_DILATIONS] + [scr(d, 2 * LANES) for d in A_DILATIONS]
                       + [pltpu.VMEM((2, 2 * A_SPAN, 2 * A_SPAN), F32)] + [pltpu.VMEM((n_dil, t, LANES), F32)] * 3
                       + [pltpu.VMEM((A_UNROLL, 2 * A_SPAN, 2 * A_SPAN), BF16),
                          pltpu.VMEM((A_UNROLL, 2 * A_SPAN, LANES), F32)],
        compiler_params=_params(("arbitrary", "arbitrary")),
        name="attn_prompt",
    )(as4(q1), q4, q16, as4(k1), k4, k16, as4(v1), v4, v16)


def _decode_kernel(bb, w_c, q_ref, k_ref, v_ref, kt_ref, vt_ref,
                   hq_ref, hk_ref, hv_ref, hg_ref, st_ref, nrm_ref, *rest):
    a_ref, bn_ref, so_ref = rest[-3:]
    n_pat = len(A_DILATIONS)
    dist = w_c - lax.broadcasted_iota(jnp.int32, (A_HEADS, w_c), 1)
    cnt = jnp.zeros((A_HEADS, w_c), F32)
    for d in A_DILATIONS:
        cnt = cnt + jnp.logical_and(dist % d == 0, dist <= A_SPAN * d).astype(F32)
    used = cnt > 0.0
    head_s = lax.broadcasted_iota(jnp.int32, (A_HEADS, w_c), 0)
    head_o = lax.broadcasted_iota(jnp.int32, (A_HEADS, A_HEAD_DIM), 0)
    for b in range(bb):
        q = q_ref[b]
        k_new, v_new = k_ref[b], v_ref[b]
        q16 = q.astype(BF16)
        s_new = jnp.sum(q * k_new, axis=-1, keepdims=True)
        s = jnp.zeros((A_HEADS, w_c), F32)
        for h in range(A_HEADS):
            s = jnp.where(head_s == h, _dot(q16, kt_ref[b, h].astype(BF16)), s)
        m = jnp.maximum(jnp.max(jnp.where(used, s, -jnp.inf), axis=-1, keepdims=True), s_new)
        w = jnp.where(used, jnp.exp(s - m), 0.0) * cnt
        p_new = n_pat * jnp.exp(s_new - m)
        den = jnp.sum(w, axis=-1, keepdims=True) + p_new
        acc = p_new * v_new
        w16 = w.astype(BF16)
        for h in range(A_HEADS):
            acc = acc + jnp.where(head_o == h, _dot_nt(w16, vt_ref[b, h].astype(BF16)), 0.0)
        a_ref[b] = acc / den
        v_rows = hv_ref[b]
        packed = jnp.concatenate([hq_ref[b], hk_ref[b], jnp.exp(hg_ref[b]),
                                  jnp.zeros((LANES - 3 * B_HEADS, B_HEAD_DIM), F32)], axis=0)
        cols = packed.T
        outs = []
        for hd in range(B_HEADS):
            col = lambda i: cols[:, i * B_HEADS + hd:i * B_HEADS + hd + 1]
            st = col(2) * st_ref[b, hd] + col(1) * v_rows[hd:hd + 1, :]
            so_ref[b, hd] = st
            o = jnp.sum(col(0) * st, axis=0, keepdims=True)
            outs.append(_rms(o, nrm_ref[hd:hd + 1, :]))
        bn_ref[b] = jnp.concatenate(outs, axis=0)


def _decode(layer, q, k, v, cache_kt, cache_vt, hq, hk, hv, hg, state, nrm, states_out, bb):
    n = q.shape[0]
    w_c = cache_kt.shape[-1]
    assert w_c >= A_SPAN * max(A_DILATIONS)
    a_heads = lambda a: a.reshape(n, A_HEADS, A_HEAD_DIM)
    heads = lambda a: a.reshape(n, B_HEADS, B_HEAD_DIM)
    a_spec = pl.BlockSpec((bb, A_HEADS, A_HEAD_DIM), lambda i: (i, 0, 0))
    headspec = pl.BlockSpec((bb, B_HEADS, B_HEAD_DIM), lambda i: (i, 0, 0))

    c_spec = pl.BlockSpec((None, bb, A_HEADS, A_HEAD_DIM, w_c), lambda i: (layer, i, 0, 0, 0))
    st_spec = pl.BlockSpec((None, bb, B_HEADS, B_HEAD_DIM, B_HEAD_DIM), lambda i: (layer, i, 0, 0, 0))
    in_specs = [a_spec] * 3 + [c_spec] * 2 + [headspec] * 4 + [st_spec, _layer_block((B_HEADS, B_HEAD_DIM), layer)]
    args = [a_heads(q), a_heads(k), a_heads(v), cache_kt, cache_vt, heads(hq), heads(hk), heads(hv), heads(hg),
            state, nrm.reshape(DEPTH, B_HEADS, B_HEAD_DIM)]
    aliases = {}
    if states_out is not None:
        aliases = {len(args): 2}
        in_specs.append(pl.BlockSpec(memory_space=pl.ANY))
        args.append(states_out)
    a_out, bn, st = pl.pallas_call(
        functools.partial(_decode_kernel, bb, w_c),
        grid=(n // bb,),
        in_specs=in_specs,
        out_specs=[a_spec, headspec, st_spec],
        out_shape=[jax.ShapeDtypeStruct((n, A_HEADS, A_HEAD_DIM), F32),
                   jax.ShapeDtypeStruct((n, B_HEADS, B_HEAD_DIM), F32),
                   jax.ShapeDtypeStruct(state.shape, F32)],
        input_output_aliases=aliases,
        compiler_params=_params(("parallel",)),
        name="decode",
    )(*args)
    return a_out.reshape(n, A_WIDTH).astype(BF16), bn.reshape(n, B_WIDTH), st


def _hgrn_prompt_kernel(th, q_ref, k_ref, v_ref, g_ref, nrm_ref, o_ref, st_ref, state_scr, b_scr, a_scr, part_scr):
    j = pl.program_id(1)

    @pl.when(j == 0)
    def _():
        state_scr[...] = jnp.zeros_like(state_scr)

    piece = B_CHUNK * B_UNROLL
    rin = lax.broadcasted_iota(jnp.int32, (piece, B_HEAD_DIM), 0) & (B_CHUNK - 1)

    def scan_piece(i, carry):
        rows = pl.ds(pl.multiple_of(i * piece, piece), piece)
        b = g_ref[rows, :] * LOG2_E
        shift = 1
        while shift < B_CHUNK:
            b = b + jnp.where(rin >= shift, pltpu.roll(b, shift, axis=0), 0.0)
            shift *= 2
        b_scr[rows, :] = b
        return carry

    lax.fori_loop(0, th // piece, scan_piece, 0)

    row = lax.broadcasted_iota(jnp.int32, (B_CHUNK, B_HEAD_DIM), 0)
    a_t = lax.broadcasted_iota(jnp.int32, (B_CHUNK, B_CHUNK), 0)
    a_s = lax.broadcasted_iota(jnp.int32, (B_CHUNK, B_CHUNK), 1)
    levels = []
    half = B_CHUNK // 2
    while half >= B_SUB:
        seg = 2 * half
        pair = jnp.logical_and(a_t // seg == a_s // seg,
                               jnp.logical_and(a_t % seg >= half, a_s % seg < half))
        levels.append((half, row % seg >= half, pair))
        half //= 2
    n_grp = B_CHUNK // 8
    sub_row = lax.broadcasted_iota(jnp.int32, (n_grp, 8, B_HEAD_DIM), 1) % B_SUB
    nrm = nrm_ref[...]

    def intra_chunk(q, k, v, bc):
        a = jnp.zeros((B_CHUNK, B_CHUNK), F32)
        for half, upper, pair in levels:
            seg = 2 * half
            b_mid = jnp.concatenate(
                [jnp.broadcast_to(bc[s0 + half - 1:s0 + half, :], (seg, B_HEAD_DIM))
                 for s0 in range(0, B_CHUNK, seg)], axis=0)
            d = bc - b_mid
            x = (jnp.where(upper, q, k) * jnp.exp2(jnp.where(upper, d, -d))).astype(BF16)
            a = jnp.where(pair, _dot_nt(x, x), a)
        q3, k3, v3, b3 = (t.reshape(n_grp, 8, B_HEAD_DIM) for t in (q, k, v, bc))
        o3 = jnp.sum(q3 * k3, axis=-1, keepdims=True) * v3
        for delta in range(1, B_SUB):
            kd, bd, vd = (pltpu.roll(t, delta, axis=1) for t in (k3, b3, v3))
            w = jnp.sum(q3 * kd * jnp.exp2(b3 - bd), axis=-1, keepdims=True)
            o3 = o3 + jnp.where(sub_row >= delta, w, 0.0) * vd
        return a, o3.reshape(B_CHUNK, B_HEAD_DIM)

    def chunk_rows(grp, u):
        return pl.ds(pl.multiple_of((grp * B_UNROLL + u) * B_CHUNK, B_CHUNK), B_CHUNK)

    def front(grp):
        st = state_scr[...]
        for u in range(B_UNROLL):
            rows = chunk_rows(grp, u)
            q, k, v, bc = q_ref[rows, :], k_ref[rows, :], v_ref[rows, :], b_scr[rows, :]
            o = _dot_nt((q * jnp.exp2(bc)).astype(BF16), st.astype(BF16))
            b_last = bc[B_CHUNK - 1:B_CHUNK, :]
            k_dec = (k * jnp.exp2(b_last - bc)).astype(BF16)
            st = st * jnp.exp2(b_last) + _dot_tn(v.astype(BF16), k_dec)
            a, o3 = intra_chunk(q, k, v, bc)
            a_scr[u] = a.astype(BF16)
            part_scr[u] = o + o3
        state_scr[...] = st

    def back(grp):
        for u in range(B_UNROLL):
            rows = chunk_rows(grp, u)
            o = part_scr[u] + _dot(a_scr[u], v_ref[rows, :].astype(BF16))
            o_ref[rows, :] = _rms(o, nrm)

    n_groups = th // (B_CHUNK * B_UNROLL)
    front(0)

    def steady(grp, carry):
        back(grp - 1)
        front(grp)
        return carry

    lax.fori_loop(1, n_groups, steady, 0)
    back(n_groups - 1)

    @pl.when(j == pl.num_programs(1) - 1)
    def _():
        st_ref[...] = state_scr[...].T


def _hgrn_prompt(q, k, v, g, nrm, layer, th):
    s_len = q.shape[1]
    blk = pl.BlockSpec((None, th, B_HEAD_DIM), lambda h, j: (h, j, 0))
    return pl.pallas_call(
        functools.partial(_hgrn_prompt_kernel, th),
        grid=(B_HEADS, s_len // th),
        in_specs=[blk] * 4 + [pl.BlockSpec((None, 1, B_HEAD_DIM), lambda h, j: (layer, 0, h))],
        out_specs=[pl.BlockSpec((th, B_HEAD_DIM), lambda h, j: (j, h)),
                   pl.BlockSpec((None, B_HEAD_DIM, B_HEAD_DIM), lambda h, j: (h, 0, 0))],
        out_shape=[jax.ShapeDtypeStruct((s_len, B_WIDTH), F32),
                   jax.ShapeDtypeStruct((B_HEADS, B_HEAD_DIM, B_HEAD_DIM), F32)],
        scratch_shapes=[pltpu.VMEM((B_HEAD_DIM, B_HEAD_DIM), F32), pltpu.VMEM((th, B_HEAD_DIM), F32),
                        pltpu.VMEM((B_UNROLL, B_CHUNK, B_CHUNK), BF16),
                        pltpu.VMEM((B_UNROLL, B_CHUNK, B_HEAD_DIM), F32)],
        compiler_params=_params(("arbitrary", "arbitrary")),
        name="hgrn_prompt",
    )(q, k, v, g, nrm)


def _mixout_kernel(x_ref, a_ref, bn_ref, pre_ref, post_ref, wg0_ref, wg1_ref, wa_ref, wb_ref, wo_ref, o_ref):
    x = x_ref[...]
    h = _rms(x, pre_ref[...]).astype(BF16)
    z0 = _dot(h, wg0_ref[...])
    z1 = _dot(h, wg1_ref[...])
    split = W_BLOCK - B_WIDTH
    g_b = z0[:, :B_WIDTH]
    gate_a = jnp.concatenate([z0[:, B_WIDTH:], z1[:, :split]], axis=1)
    gate_b = z1[:, split:]
    b_out = (bn_ref[...] * (g_b * jax.nn.sigmoid(g_b))).astype(BF16)
    a_proj = _dot(a_ref[...], wa_ref[...])
    b_proj = _dot(b_out, wb_ref[...])
    mix = (jax.nn.sigmoid(gate_a) * a_proj + jax.nn.sigmoid(gate_b) * b_proj).astype(BF16)
    o_ref[...] = x + _rms(_dot(mix, wo_ref[...]), post_ref[...])


def _mixout(x, a_out, bn, layer, pre, post, w_in, wa, wb, wo, tm):
    n = x.shape[0]
    row = lambda width: pl.BlockSpec((tm, width), lambda i: (i, 0))
    return pl.pallas_call(
        _mixout_kernel,
        grid=(n // tm,),
        in_specs=[row(D_MODEL), row(A_WIDTH), row(B_WIDTH), _vec(layer), _vec(layer),
                  _layer_block((D_MODEL, W_BLOCK), layer, 3), _layer_block((D_MODEL, W_BLOCK), layer, 4),
                  _layer_block((A_WIDTH, D_MODEL), layer), _layer_block((B_WIDTH, D_MODEL), layer),
                  _layer_block((D_MODEL, D_MODEL), layer)],
        out_specs=row(D_MODEL),
        out_shape=jax.ShapeDtypeStruct((n, D_MODEL), F32),
        compiler_params=_params(("parallel",)),
        name="mixout",
    )(x, a_out, bn, pre, post, w_in, w_in, wa, wb, wo)


def _rope_tables(pos):
    half = ROPE_DIM // 2
    inv = ROPE_THETA ** (-jnp.arange(half, dtype=F32) / half)
    ang = pos.astype(F32)[:, None] * inv[None, :]
    cos, sin = jnp.cos(ang), jnp.sin(ang)
    t = pos.shape[0]
    one = jnp.ones((t, A_HEAD_DIM - ROPE_DIM), F32)
    zero = jnp.zeros_like(one)
    z8 = jnp.zeros((t, half), F32)
    per_head = lambda parts: jnp.tile(jnp.concatenate(parts, axis=-1), (1, LANES // A_HEAD_DIM))
    return per_head([cos, cos, one]), per_head([-sin, z8, zero]), per_head([z8, sin, zero])


def kernel(x_prompt, x_sample, cache_k, cache_v, state_hgrn, ffn1_norm_pre, ffn1_norm_post, ffn1_w_gate, ffn1_w_up, ffn1_w_down, mix_norm_pre, mix_norm_post, w_in, hgrn_lb_logits, hgrn_out_norm, w_a_out, w_b_out, w_mix_out, ffn2_norm_pre, ffn2_norm_post, ffn2_w_gate, ffn2_w_up, ffn2_w_down):
    batch, s_len, _ = x_prompt.shape
    n_dec, t_dec, _ = x_sample.shape
    assert batch == 1 and t_dec == 1 and s_len % ATTN_TILE == 0
    tm = 512
    th = min(4096, s_len)
    bb = 2
    cache_kt = jnp.transpose(cache_k, (0, 1, 3, 4, 2))
    cache_vt = jnp.transpose(cache_v, (0, 1, 3, 4, 2))
    yp = x_prompt.reshape(s_len, D_MODEL)
    ys = x_sample.reshape(n_dec, D_MODEL)
    tabs_p = _rope_tables(jnp.arange(s_len))
    tabs_s = _rope_tables(jnp.full((n_dec,), PAST_LEN))
    lbl = hgrn_lb_logits.astype(F32)
    vecs = lambda a: a.reshape(DEPTH, 1, -1).astype(F32)
    w16 = lambda w: w.astype(BF16)
    w_len = min(A_MAX_WINDOW, s_len)
    f1 = (vecs(ffn1_norm_pre), vecs(ffn1_norm_post), w16(ffn1_w_gate), w16(ffn1_w_up), w16(ffn1_w_down))
    f2 = (vecs(ffn2_norm_pre), vecs(ffn2_norm_post), w16(ffn2_w_gate), w16(ffn2_w_up), w16(ffn2_w_down))
    win = w16(w_in)
    m_pre, m_post, nrm = vecs(mix_norm_pre), vecs(mix_norm_post), vecs(hgrn_out_norm)
    mo = (m_pre, m_post, win, w16(w_a_out), w16(w_b_out), w16(w_mix_out))
    kp, vp, sp, ksn, vsn = [], [], [], [], []
    states = None
    for l in range(DEPTH):
        yp = _ffn(yp, l, *f1, tm)
        q1, q4, q16, k1, k4, k16, v1, v4, v16, kf, vf = _inproj_attn(yp, m_pre, tabs_p, win, l, tm)
        hq, hk, hv, hg = _inproj_rec(yp, m_pre, lbl, win, l, tm)
        a_out = _attn_prompt(q1, q4, q16, k1, k4, k16, v1, v4, v16)
        bn, st_p = _hgrn_prompt(hq, hk, hv, hg, nrm, l, th)
        yp = _mixout(yp, a_out, bn, l, *mo, tm)
        yp = _ffn(yp, l, *f2, tm)
        kp.append(kf.reshape(batch, w_len, A_HEADS, A_HEAD_DIM))
        vp.append(vf.reshape(batch, w_len, A_HEADS, A_HEAD_DIM))
        sp.append(st_p.reshape(batch, B_HEADS, B_HEAD_DIM, B_HEAD_DIM))

        ys = _ffn(ys, l, *f1, n_dec)
        qa, ka, va, sq, sk, sv, sg = _inproj_sample(ys, m_pre, tabs_s, lbl, win, l)
        a_s, bn_s, states = _decode(l, qa, ka, va, cache_kt, cache_vt, sq, sk, sv, sg, state_hgrn, nrm, states, bb)
        ys = _mixout(ys, a_s, bn_s, l, *mo, n_dec)
        ys = _ffn(ys, l, *f2, n_dec)
        ksn.append(ka.reshape(n_dec, t_dec, A_HEADS, A_HEAD_DIM))
        vsn.append(va.reshape(n_dec, t_dec, A_HEADS, A_HEAD_DIM))
    return (yp.reshape(batch, s_len, D_MODEL), ys.reshape(n_dec, t_dec, D_MODEL),
            jnp.stack(kp), jnp.stack(vp), jnp.stack(sp), jnp.stack(ksn), jnp.stack(vsn), states)
```

```python
import functools

import jax
import jax.numpy as jnp
import numpy as np
from jax import lax
from jax.experimental import pallas as pl
from jax.experimental.pallas import tpu as pltpu

F32 = jnp.float32
BF16 = jnp.bfloat16

D_MODEL = 1024
DEPTH = 2
PAST_LEN = 16384
A_HEADS = 8
A_HEAD_DIM = 64
A_WIDTH = A_HEADS * A_HEAD_DIM
A_DILATIONS = (1, 4, 16)
A_SPAN = 128
A_MAX_WINDOW = 2048
A_SCALE = A_HEAD_DIM ** -0.5
ROPE_THETA = 500000.0
ROPE_DIM = A_HEAD_DIM // 4
B_HEADS = 8
B_HEAD_DIM = 128
B_WIDTH = B_HEADS * B_HEAD_DIM
B_CHUNK = 64
B_SUB = 4
B_UNROLL = 4
B_SCALE = B_HEAD_DIM ** -0.5
D_FF = 2816
EPS = 1e-6
LOG2_E = 1.4426950408889634

W_BLOCK = 3 * A_WIDTH
assert 3 * B_WIDTH == 2 * W_BLOCK and B_WIDTH + 2 * D_MODEL == 2 * W_BLOCK
LANES = 128
N_PAIRS = A_WIDTH // LANES
ATTN_TILE = A_SPAN * max(A_DILATIONS)
A_UNROLL = 8
VMEM_LIMIT = 56 * 1024 * 1024


def _params(sem, vmem=VMEM_LIMIT):
    return pltpu.CompilerParams(dimension_semantics=sem, vmem_limit_bytes=vmem)


def _resident(shape):
    nd = len(shape)
    return pl.BlockSpec(shape, lambda *_: (0,) * nd, pipeline_mode=pl.Buffered(1))


def _layer_block(shape, layer, col=0):
    index = (layer,) + (0,) * (len(shape) - 1) + (col,)
    return pl.BlockSpec((None,) + tuple(shape), lambda *_: index, pipeline_mode=pl.Buffered(1))


def _vec(layer):
    return _layer_block((1, D_MODEL), layer)


def _rms(x, g):
    y = x * lax.rsqrt(jnp.mean(x * x, axis=-1, keepdims=True) + EPS)
    return y * g


def _dot(a, b):
    return jnp.dot(a, b, preferred_element_type=F32)


def _dot_nt(a, b):
    return lax.dot_general(a, b, (((1,), (1,)), ((), ())), preferred_element_type=F32)


def _dot_tn(a, b):
    return lax.dot_general(a, b, (((0,), (0,)), ((), ())), preferred_element_type=F32)


def _ffn_kernel(x_ref, pre_ref, post_ref, wg_ref, wu_ref, wd_ref, o_ref):
    x = x_ref[...]
    h = _rms(x, pre_ref[...]).astype(BF16)
    g = _dot(h, wg_ref[...])
    u = _dot(h, wu_ref[...])
    a = (g * jax.nn.sigmoid(g) * u).astype(BF16)
    y = _dot(a, wd_ref[...])
    o_ref[...] = x + 0.5 * _rms(y, post_ref[...])


def _ffn(x, layer, pre, post, wg, wu, wd, tm):
    n = x.shape[0]
    row = pl.BlockSpec((tm, D_MODEL), lambda i: (i, 0))
    return pl.pallas_call(
        _ffn_kernel,
        grid=(n // tm,),
        in_specs=[row, _vec(layer), _vec(layer), _layer_block((D_MODEL, D_FF), layer),
                  _layer_block((D_MODEL, D_FF), layer), _layer_block((D_FF, D_MODEL), layer)],
        out_specs=row,
        out_shape=jax.ShapeDtypeStruct((n, D_MODEL), F32),
        compiler_params=_params(("parallel",)),
        name="ffn",
    )(x, pre, post, wg, wu, wd)


def _lower_bound(logits, layer):
    e = jnp.exp(logits - jnp.max(logits, axis=0, keepdims=True))
    sm = e / jnp.sum(e, axis=0, keepdims=True)
    lb = jnp.zeros((1, B_WIDTH), F32)
    for i in range(1, layer + 1):
        lb = lb + sm[i:i + 1, :]
    return lb


def _rope_slab(t, cos, sin_lo, sin_hi):
    return t * cos + pltpu.roll(t, LANES - ROPE_DIM // 2, axis=1) * sin_lo + pltpu.roll(t, ROPE_DIM // 2, axis=1) * sin_hi


def _hgrn_gates(h, wb0_ref, wb1_ref, lb):
    z0 = _dot(h, wb0_ref[...])
    z1 = _dot(h, wb1_ref[...])
    split = W_BLOCK - B_WIDTH
    q = z0[:, :B_WIDTH] * B_SCALE
    f_raw = jnp.concatenate([z0[:, B_WIDTH:], z1[:, :split]], axis=1)
    i_raw = z1[:, split:]
    f = lb + (1.0 - lb) * jax.nn.sigmoid(f_raw)
    return q, 1.0 - f, i_raw * jax.nn.sigmoid(i_raw), jnp.log(f)


def _inproj_attn_kernel(tm, x_ref, pre_ref, cos_ref, slo_ref, shi_ref, wa_ref,
                        q1_ref, q4_ref, q16_ref, k1_ref, k4_ref, k16_ref, v1_ref, v4_ref, v16_ref,
                        kf_ref, vf_ref, stage_ref, stage4_ref):
    h = _rms(x_ref[...], pre_ref[...]).astype(BF16)
    za = _dot(h, wa_ref[...])
    cos, slo, shi = cos_ref[...], slo_ref[...], shi_ref[...]
    outs = ((q1_ref, q4_ref, q16_ref), (k1_ref, k4_ref, k16_ref), (v1_ref, v4_ref, v16_ref))
    for s in range(3 * N_PAIRS):
        kind, hp = divmod(s, N_PAIRS)
        t = za[:, s * LANES:(s + 1) * LANES]
        if kind < 2:
            t = _rope_slab(t, cos, slo, shi)
        if kind == 0:
            t = t * (A_SCALE * LOG2_E)
        if kind == 1:
            kf_ref[:, hp * LANES:(hp + 1) * LANES] = t
        if kind == 2:
            vf_ref[:, hp * LANES:(hp + 1) * LANES] = t
        stage_ref[s] = t
        o1, o4, o16 = outs[kind]
        o1[hp] = t.astype(BF16)
        for r in range(4):
            c4 = stage_ref[s, pl.ds(r, tm // 4, stride=4), :]
            o4[hp, r] = c4.astype(BF16)
            stage4_ref[s, r] = c4
        for r in range(4):
            for a in range(4):
                o16[hp, r + 4 * a] = stage4_ref[s, r, pl.ds(a, tm // 16, stride=4), :].astype(BF16)


def _inproj_attn(x, pre, tabs, w_in, layer, tm):
    s_len = x.shape[0]
    w_len = min(A_MAX_WINDOW, s_len)
    first = (s_len - w_len) // tm
    row = pl.BlockSpec((tm, D_MODEL), lambda i: (i, 0))
    tab = pl.BlockSpec((tm, LANES), lambda i: (i, 0))
    l1 = pl.BlockSpec((N_PAIRS, tm, LANES), lambda i: (0, i, 0))
    l4 = pl.BlockSpec((N_PAIRS, 4, tm // 4, LANES), lambda i: (0, 0, i, 0))
    l16 = pl.BlockSpec((N_PAIRS, 16, tm // 16, LANES), lambda i: (0, 0, i, 0))
    win = pl.BlockSpec((tm, A_WIDTH), lambda i: (jnp.maximum(i - first, 0), 0))
    s1 = jax.ShapeDtypeStruct((N_PAIRS, s_len, LANES), BF16)
    s4 = jax.ShapeDtypeStruct((N_PAIRS, 4, s_len // 4, LANES), BF16)
    s16 = jax.ShapeDtypeStruct((N_PAIRS, 16, s_len // 16, LANES), BF16)
    sw = jax.ShapeDtypeStruct((w_len, A_WIDTH), F32)
    return pl.pallas_call(
        functools.partial(_inproj_attn_kernel, tm),
        grid=(s_len // tm,),
        in_specs=[row, _vec(layer), tab, tab, tab, _layer_block((D_MODEL, W_BLOCK), layer, 0)],
        out_specs=[l1, l4, l16, l1, l4, l16, l1, l4, l16, win, win],
        out_shape=[s1, s4, s16, s1, s4, s16, s1, s4, s16, sw, sw],
        scratch_shapes=[pltpu.VMEM((3 * N_PAIRS, tm, LANES), F32), pltpu.VMEM((3 * N_PAIRS, 4, tm // 4, LANES), F32)],
        compiler_params=_params(("arbitrary",)),
        name="inproj_attn",
    )(x, pre, *tabs, w_in)


def _inproj_rec_kernel(layer, x_ref, pre_ref, lbl_ref, wb0_ref, wb1_ref, hq_ref, hk_ref, hv_ref, hg_ref):
    h = _rms(x_ref[...], pre_ref[...]).astype(BF16)
    q, k, v, g = _hgrn_gates(h, wb0_ref, wb1_ref, _lower_bound(lbl_ref[...], layer))
    for hd in range(B_HEADS):
        sl = slice(hd * B_HEAD_DIM, (hd + 1) * B_HEAD_DIM)
        hq_ref[hd] = q[:, sl]
        hk_ref[hd] = k[:, sl]
        hv_ref[hd] = v[:, sl]
        hg_ref[hd] = g[:, sl]


def _inproj_rec(x, pre, lbl, w_in, layer, tm):
    s_len = x.shape[0]
    row = pl.BlockSpec((tm, D_MODEL), lambda i: (i, 0))
    hd = pl.BlockSpec((B_HEADS, tm, B_HEAD_DIM), lambda i: (0, i, 0))
    sh = jax.ShapeDtypeStruct((B_HEADS, s_len, B_HEAD_DIM), F32)
    return pl.pallas_call(
        functools.partial(_inproj_rec_kernel, layer),
        grid=(s_len // tm,),
        in_specs=[row, _vec(layer), _resident((DEPTH, B_WIDTH)),
                  _layer_block((D_MODEL, W_BLOCK), layer, 1), _layer_block((D_MODEL, W_BLOCK), layer, 2)],
        out_specs=[hd] * 4,
        out_shape=[sh] * 4,
        compiler_params=_params(("parallel",)),
        name="inproj_rec",
    )(x, pre, lbl, w_in, w_in)


def _inproj_sample_kernel(layer, x_ref, pre_ref, cos_ref, slo_ref, shi_ref, lbl_ref, wa_ref, wb0_ref, wb1_ref,
                          qa_ref, ka_ref, va_ref, hq_ref, hk_ref, hv_ref, hg_ref):
    h = _rms(x_ref[...], pre_ref[...]).astype(BF16)
    za = _dot(h, wa_ref[...])
    cos, slo, shi = cos_ref[...], slo_ref[...], shi_ref[...]
    outs = (qa_ref, ka_ref, va_ref)
    for s in range(3 * N_PAIRS):
        kind, hp = divmod(s, N_PAIRS)
        t = za[:, s * LANES:(s + 1) * LANES]
        if kind < 2:
            t = _rope_slab(t, cos, slo, shi)
        if kind == 0:
            t = t * A_SCALE
        outs[kind][:, hp * LANES:(hp + 1) * LANES] = t
    q, k, v, g = _hgrn_gates(h, wb0_ref, wb1_ref, _lower_bound(lbl_ref[...], layer))
    hq_ref[...] = q
    hk_ref[...] = k
    hv_ref[...] = v
    hg_ref[...] = g


def _inproj_sample(x, pre, tabs, lbl, w_in, layer):
    n = x.shape[0]
    full = lambda *shape: pl.BlockSpec(shape, lambda i: (0,) * len(shape))
    sa = jax.ShapeDtypeStruct((n, A_WIDTH), F32)
    sb = jax.ShapeDtypeStruct((n, B_WIDTH), F32)
    return pl.pallas_call(
        functools.partial(_inproj_sample_kernel, layer),
        grid=(1,),
        in_specs=[full(n, D_MODEL), _vec(layer), full(n, LANES), full(n, LANES), full(n, LANES),
                  full(DEPTH, B_WIDTH)] + [_layer_block((D_MODEL, W_BLOCK), layer, c) for c in range(3)],
        out_specs=[full(n, A_WIDTH)] * 3 + [full(n, B_WIDTH)] * 4,
        out_shape=[sa] * 3 + [sb] * 4,
        compiler_params=_params(("arbitrary",)),
        name="inproj_sample",
    )(x, pre, *tabs, lbl, w_in, w_in, w_in)


def _attn_scores(q, k, bias):
    first = lax.broadcasted_iota(jnp.int32, (A_SPAN, LANES), 1) < A_HEAD_DIM
    zero = jnp.zeros_like(q)
    q2 = jnp.concatenate([jnp.where(first, q, zero), jnp.where(first, zero, q)], axis=0)
    s = _dot_nt(q2, k) + bias
    m = jnp.max(jnp.maximum(s[:, :A_SPAN], s[:, A_SPAN:]), axis=-1, keepdims=True)
    return jnp.exp2(s - m).astype(BF16), jnp.broadcast_to(m, (2 * A_SPAN, LANES))


def _attn_values(p, m, v_ext):
    first = lax.broadcasted_iota(jnp.int32, (A_SPAN, LANES), 1) < A_HEAD_DIM
    oe = _dot(p, v_ext)
    pick = lambda t: jnp.where(first, t[:A_SPAN], t[A_SPAN:])
    return pick(oe[:, :LANES]), pick(m), pick(oe[:, LANES:])


def _attn_prompt_kernel(q1_ref, q4_ref, q16_ref, k1_ref, k4_ref, k16_ref, v1_ref, v4_ref, v16_ref,
                        o_ref, ks1, ks4, ks16, vs1, vs4, vs16, bias_scr, o_scr, m_scr, d_scr, p_scr, mx_scr):
    j = pl.program_id(1)
    q_refs = (q1_ref, q4_ref, q16_ref)
    k_in = (k1_ref, k4_ref, k16_ref)
    v_in = (v1_ref, v4_ref, v16_ref)
    k_scr = (ks1, ks4, ks16)
    v_scr = (vs1, vs4, vs16)

    @pl.when(j == 0)
    def _():
        for scr in k_scr:
            scr[:, 0:A_SPAN, :] = jnp.zeros((scr.shape[0], A_SPAN, LANES), BF16)
        for scr in v_scr:
            scr[:, 0:A_SPAN, 0:LANES] = jnp.zeros((scr.shape[0], A_SPAN, LANES), BF16)
            scr[:, :, LANES:] = jnp.ones((scr.shape[0], scr.shape[1], LANES), BF16)
        qi = lax.broadcasted_iota(jnp.int32, (2 * A_SPAN, 2 * A_SPAN), 0) % A_SPAN
        ki = lax.broadcasted_iota(jnp.int32, (2 * A_SPAN, 2 * A_SPAN), 1)
        band = jnp.logical_and(ki >= qi, ki <= qi + A_SPAN)
        bias_scr[1] = jnp.where(band, 0.0, -jnp.inf)
        bias_scr[0] = jnp.where(jnp.logical_and(band, ki >= A_SPAN), 0.0, -jnp.inf)

    for p, d in enumerate(A_DILATIONS):
        k_scr[p][:, A_SPAN:, :] = k_in[p][...]
        v_scr[p][:, A_SPAN:, 0:LANES] = v_in[p][...]

    def locate(p, grp, u):
        per_class = ATTN_TILE // A_DILATIONS[p] // A_SPAN
        b = grp * A_UNROLL + u
        n = b % per_class
        return b // per_class, pl.multiple_of(n * A_SPAN, A_SPAN), n

    def front(p, grp):
        for u in range(A_UNROLL):
            r, lo, n = locate(p, grp, u)
            has_prev = jnp.logical_or(j > 0, n > 0).astype(jnp.int32)
            p_scr[u], mx_scr[u] = _attn_scores(q_refs[p][r, pl.ds(lo, A_SPAN), :],
                                               k_scr[p][r, pl.ds(lo, 2 * A_SPAN), :], bias_scr[has_prev])

    def back(p, grp):
        d = A_DILATIONS[p]
        for u in range(A_UNROLL):
            r, lo, n = locate(p, grp, u)
            acc, m, den = _attn_values(p_scr[u], mx_scr[u], v_scr[p][r, pl.ds(lo, 2 * A_SPAN), :])
            if d > 1:
                rows = pl.ds(n * (A_SPAN * d) + r, A_SPAN, stride=d)
                o_scr[p - 1, rows, :] = acc
                m_scr[p - 1, rows, :] = m
                d_scr[p - 1, rows, :] = den
            else:
                rows = pl.ds(lo, A_SPAN)
                m1, m2 = m_scr[0, rows, :], m_scr[1, rows, :]
                top = jnp.maximum(jnp.maximum(m, m1), m2)
                w0, w1, w2 = jnp.exp2(m - top), jnp.exp2(m1 - top), jnp.exp2(m2 - top)
                num = w0 * acc + w1 * o_scr[0, rows, :] + w2 * o_scr[1, rows, :]
                tot = w0 * den + w1 * d_scr[0, rows, :] + w2 * d_scr[1, rows, :]
                o_ref[rows, :] = (num / tot).astype(BF16)

    n_grp = ATTN_TILE // A_SPAN // A_UNROLL
    order = tuple(range(1, len(A_DILATIONS))) + (0,)
    n_dyn = n_grp + jnp.minimum(j, 0)
    front(order[0], 0)
    for idx, p in enumerate(order):

        def steady(grp, carry, p=p):
            back(p, grp - 1)
            front(p, grp)
            return carry

        lax.fori_loop(1, n_dyn, steady, 0)
        back(p, n_grp - 1)
        if idx + 1 < len(order):
            front(order[idx + 1], 0)

    for p, d in enumerate(A_DILATIONS):
        tail = ATTN_TILE // d
        k_scr[p][:, 0:A_SPAN, :] = k_scr[p][:, tail:tail + A_SPAN, :]
        v_scr[p][:, 0:A_SPAN, 0:LANES] = v_scr[p][:, tail:tail + A_SPAN, 0:LANES]


def _attn_prompt(q1, q4, q16, k1, k4, k16, v1, v4, v16):
    s_len = q1.shape[1]
    t = ATTN_TILE
    b1 = pl.BlockSpec((None, 1, t, LANES), lambda hp, j: (hp, 0, j, 0))
    b4 = pl.BlockSpec((None, 4, t // 4, LANES), lambda hp, j: (hp, 0, j, 0))
    b16 = pl.BlockSpec((None, 16, t // 16, LANES), lambda hp, j: (hp, 0, j, 0))
    as4 = lambda a: a.reshape(N_PAIRS, 1, s_len, LANES)
    scr = lambda d, width: pltpu.VMEM((d, A_SPAN + t // d, width), BF16)
    n_dil = len(A_DILATIONS) - 1
    return pl.pallas_call(
        _attn_prompt_kernel,
        grid=(N_PAIRS, s_len // t),
        in_specs=[b1, b4, b16] * 3,
        out_specs=pl.BlockSpec((t, LANES), lambda hp, j: (j, hp)),
        out_shape=jax.ShapeDtypeStruct((s_len, A_WIDTH), BF16),
        scratch_shapes=[scr(d, LANES) for d in A_DILATIONS] + [scr(d, 2 * LANES) for d in A_DILATIONS]
                       + [pltpu.VMEM((2, 2 * A_SPAN, 2 * A_SPAN), F32)] + [pltpu.VMEM((n_dil, t, LANES), F32)] * 3
                       + [pltpu.VMEM((A_UNROLL, 2 * A_SPAN, 2 * A_SPAN), BF16),
                          pltpu.VMEM((A_UNROLL, 2 * A_SPAN, LANES), F32)],
        compiler_params=_params(("arbitrary", "arbitrary")),
        name="attn_prompt",
    )(as4(q1), q4, q16, as4(k1), k4, k16, as4(v1), v4, v16)


def _decode_rows(bb, w_c, q_ref, k_ref, v_ref, kt_ref, vt_ref,
                 hq_ref, hk_ref, hv_ref, hg_ref, st_ref, nrm_ref, a_ref, bn_ref, so_ref):
    n_pat = len(A_DILATIONS)
    dist = w_c - lax.broadcasted_iota(jnp.int32, (A_HEADS, w_c), 1)
    cnt = jnp.zeros((A_HEADS, w_c), F32)
    for d in A_DILATIONS:
        cnt = cnt + jnp.logical_and(dist % d == 0, dist <= A_SPAN * d).astype(F32)
    used = cnt > 0.0
    head_s = lax.broadcasted_iota(jnp.int32, (A_HEADS, w_c), 0)
    head_o = lax.broadcasted_iota(jnp.int32, (A_HEADS, A_HEAD_DIM), 0)
    for b in range(bb):
        q = q_ref[b]
        k_new, v_new = k_ref[b], v_ref[b]
        q16 = q.astype(BF16)
        s_new = jnp.sum(q * k_new, axis=-1, keepdims=True)
        s = jnp.zeros((A_HEADS, w_c), F32)
        for h in range(A_HEADS):
            s = jnp.where(head_s == h, _dot(q16, kt_ref[b, h].astype(BF16)), s)
        m = jnp.maximum(jnp.max(jnp.where(used, s, -jnp.inf), axis=-1, keepdims=True), s_new)
        w = jnp.where(used, jnp.exp(s - m), 0.0) * cnt
        p_new = n_pat * jnp.exp(s_new - m)
        den = jnp.sum(w, axis=-1, keepdims=True) + p_new
        acc = p_new * v_new
        w16 = w.astype(BF16)
        for h in range(A_HEADS):
            acc = acc + jnp.where(head_o == h, _dot_nt(w16, vt_ref[b, h].astype(BF16)), 0.0)
        a_ref[b] = acc / den
        v_rows = hv_ref[b]
        packed = jnp.concatenate([hq_ref[b], hk_ref[b], jnp.exp(hg_ref[b]),
                                  jnp.zeros((LANES - 3 * B_HEADS, B_HEAD_DIM), F32)], axis=0)
        cols = packed.T
        outs = []
        for hd in range(B_HEADS):
            col = lambda i: cols[:, i * B_HEADS + hd:i * B_HEADS + hd + 1]
            st = col(2) * st_ref[b, hd] + col(1) * v_rows[hd:hd + 1, :]
            so_ref[b, hd] = st
            o = jnp.sum(col(0) * st, axis=0, keepdims=True)
            outs.append(_rms(o, nrm_ref[hd:hd + 1, :]))
        bn_ref[b] = jnp.concatenate(outs, axis=0)


N_DECODE_IN = 11


def _recurrent_kernel(th, bb, w_c, q_ref, k_ref, v_ref, g_ref, nrm_ref, *rest):
    dec_in = rest[:N_DECODE_IN]
    o_ref, st_ref, da_ref, dbn_ref, dso_ref, state_scr, b_scr, a_scr, part_scr = rest[-9:]
    j = pl.program_id(1)

    @pl.when(j == 0)
    def _():
        state_scr[...] = jnp.zeros_like(state_scr)

    piece = B_CHUNK * B_UNROLL
    rin = lax.broadcasted_iota(jnp.int32, (piece, B_HEAD_DIM), 0) & (B_CHUNK - 1)

    def scan_piece(i, carry):
        rows = pl.ds(pl.multiple_of(i * piece, piece), piece)
        b = g_ref[rows, :] * LOG2_E
        shift = 1
        while shift < B_CHUNK:
            b = b + jnp.where(rin >= shift, pltpu.roll(b, shift, axis=0), 0.0)
            shift *= 2
        b_scr[rows, :] = b
        return carry

    lax.fori_loop(0, th // piece, scan_piece, 0)

    row = lax.broadcasted_iota(jnp.int32, (B_CHUNK, B_HEAD_DIM), 0)
    a_t = lax.broadcasted_iota(jnp.int32, (B_CHUNK, B_CHUNK), 0)
    a_s = lax.broadcasted_iota(jnp.int32, (B_CHUNK, B_CHUNK), 1)
    levels = []
    half = B_CHUNK // 2
    while half >= B_SUB:
        seg = 2 * half
        pair = jnp.logical_and(a_t // seg == a_s // seg,
                               jnp.logical_and(a_t % seg >= half, a_s % seg < half))
        levels.append((half, row % seg >= half, pair))
        half //= 2
    n_grp = B_CHUNK // 8
    sub_row = lax.broadcasted_iota(jnp.int32, (n_grp, 8, B_HEAD_DIM), 1) % B_SUB
    nrm = nrm_ref[...]

    def intra_chunk(q, k, v, bc):
        a = jnp.zeros((B_CHUNK, B_CHUNK), F32)
        for half, upper, pair in levels:
            seg = 2 * half
            b_mid = jnp.concatenate(
                [jnp.broadcast_to(bc[s0 + half - 1:s0 + half, :], (seg, B_HEAD_DIM))
                 for s0 in range(0, B_CHUNK, seg)], axis=0)
            d = bc - b_mid
            x = (jnp.where(upper, q, k) * jnp.exp2(jnp.where(upper, d, -d))).astype(BF16)
            a = jnp.where(pair, _dot_nt(x, x), a)
        q3, k3, v3, b3 = (t.reshape(n_grp, 8, B_HEAD_DIM) for t in (q, k, v, bc))
        o3 = jnp.sum(q3 * k3, axis=-1, keepdims=True) * v3
        for delta in range(1, B_SUB):
            kd, bd, vd = (pltpu.roll(t, delta, axis=1) for t in (k3, b3, v3))
            w = jnp.sum(q3 * kd * jnp.exp2(b3 - bd), axis=-1, keepdims=True)
            o3 = o3 + jnp.where(sub_row >= delta, w, 0.0) * vd
        return a, o3.reshape(B_CHUNK, B_HEAD_DIM)

    def chunk_rows(grp, u):
        return pl.ds(pl.multiple_of((grp * B_UNROLL + u) * B_CHUNK, B_CHUNK), B_CHUNK)

    def front(grp):
        st = state_scr[...]
        for u in range(B_UNROLL):
            rows = chunk_rows(grp, u)
            q, k, v, bc = q_ref[rows, :], k_ref[rows, :], v_ref[rows, :], b_scr[rows, :]
            o = _dot_nt((q * jnp.exp2(bc)).astype(BF16), st.astype(BF16))
            b_last = bc[B_CHUNK - 1:B_CHUNK, :]
            k_dec = (k * jnp.exp2(b_last - bc)).astype(BF16)
            st = st * jnp.exp2(b_last) + _dot_tn(v.astype(BF16), k_dec)
            a, o3 = intra_chunk(q, k, v, bc)
            a_scr[u] = a.astype(BF16)
            part_scr[u] = o + o3
        state_scr[...] = st

    def back(grp):
        for u in range(B_UNROLL):
            rows = chunk_rows(grp, u)
            o = part_scr[u] + _dot(a_scr[u], v_ref[rows, :].astype(BF16))
            o_ref[rows, :] = _rms(o, nrm)

    n_groups = th // (B_CHUNK * B_UNROLL)
    front(0)

    def steady(grp, carry):
        back(grp - 1)
        front(grp)
        return carry

    lax.fori_loop(1, n_groups, steady, 0)
    back(n_groups - 1)
    _decode_rows(bb, w_c, *dec_in, da_ref, dbn_ref, dso_ref)

    @pl.when(j == pl.num_programs(1) - 1)
    def _():
        st_ref[...] = state_scr[...].T


def _recurrent(layer, q, k, v, g, nrm, dq, dk, dv, cache_kt, cache_vt, dhq, dhk, dhv, dhg, state, states_out, th):
    s_len = q.shape[1]
    n = dq.shape[0]
    w_c = cache_kt.shape[-1]
    n_tiles = s_len // th
    steps = B_HEADS * n_tiles
    assert w_c >= A_SPAN * max(A_DILATIONS) and n % steps == 0
    bb = n // steps
    step = lambda h, j: h * n_tiles + j
    blk = pl.BlockSpec((None, th, B_HEAD_DIM), lambda h, j: (h, j, 0))
    a_spec = pl.BlockSpec((bb, A_HEADS, A_HEAD_DIM), lambda h, j: (step(h, j), 0, 0))
    headspec = pl.BlockSpec((bb, B_HEADS, B_HEAD_DIM), lambda h, j: (step(h, j), 0, 0))
    c_spec = pl.BlockSpec((None, bb, A_HEADS, A_HEAD_DIM, w_c), lambda h, j: (layer, step(h, j), 0, 0, 0))
    st_spec = pl.BlockSpec((None, bb, B_HEADS, B_HEAD_DIM, B_HEAD_DIM), lambda h, j: (layer, step(h, j), 0, 0, 0))
    a_heads = lambda a: a.reshape(n, A_HEADS, A_HEAD_DIM)
    heads = lambda a: a.reshape(n, B_HEADS, B_HEAD_DIM)
    in_specs = [blk] * 4 + [pl.BlockSpec((None, 1, B_HEAD_DIM), lambda h, j: (layer, 0, h))]
    in_specs += [a_spec] * 3 + [c_spec] * 2 + [headspec] * 4 + [st_spec, _layer_block((B_HEADS, B_HEAD_DIM), layer)]
    args = [q, k, v, g, nrm, a_heads(dq), a_heads(dk), a_heads(dv), cache_kt, cache_vt,
            heads(dhq), heads(dhk), heads(dhv), heads(dhg), state, nrm.reshape(DEPTH, B_HEADS, B_HEAD_DIM)]
    assert len(args) == 5 + N_DECODE_IN
    aliases = {}
    if states_out is not None:
        aliases = {len(args): 4}
        in_specs.append(pl.BlockSpec(memory_space=pl.ANY))
        args.append(states_out)
    bn, st_p, a_out, dbn, states = pl.pallas_call(
        functools.partial(_recurrent_kernel, th, bb, w_c),
        grid=(B_HEADS, n_tiles),
        in_specs=in_specs,
        out_specs=[pl.BlockSpec((th, B_HEAD_DIM), lambda h, j: (j, h)),
                   pl.BlockSpec((None, B_HEAD_DIM, B_HEAD_DIM), lambda h, j: (h, 0, 0)),
                   a_spec, headspec, st_spec],
        out_shape=[jax.ShapeDtypeStruct((s_len, B_WIDTH), F32),
                   jax.ShapeDtypeStruct((B_HEADS, B_HEAD_DIM, B_HEAD_DIM), F32),
                   jax.ShapeDtypeStruct((n, A_HEADS, A_HEAD_DIM), F32),
                   jax.ShapeDtypeStruct((n, B_HEADS, B_HEAD_DIM), F32),
                   jax.ShapeDtypeStruct(state.shape, F32)],
        scratch_shapes=[pltpu.VMEM((B_HEAD_DIM, B_HEAD_DIM), F32), pltpu.VMEM((th, B_HEAD_DIM), F32),
                        pltpu.VMEM((B_UNROLL, B_CHUNK, B_CHUNK), BF16),
                        pltpu.VMEM((B_UNROLL, B_CHUNK, B_HEAD_DIM), F32)],
        input_output_aliases=aliases,
        compiler_params=_params(("arbitrary", "arbitrary")),
        name="recurrent",
    )(*args)
    return bn, st_p, a_out.reshape(n, A_WIDTH).astype(BF16), dbn.reshape(n, B_WIDTH), states


def _mixout_kernel(x_ref, a_ref, bn_ref, pre_ref, post_ref, wg0_ref, wg1_ref, wa_ref, wb_ref, wo_ref, o_ref):
    x = x_ref[...]
    h = _rms(x, pre_ref[...]).astype(BF16)
    z0 = _dot(h, wg0_ref[...])
    z1 = _dot(h, wg1_ref[...])
    split = W_BLOCK - B_WIDTH
    g_b = z0[:, :B_WIDTH]
    gate_a = jnp.concatenate([z0[:, B_WIDTH:], z1[:, :split]], axis=1)
    gate_b = z1[:, split:]
    b_out = (bn_ref[...] * (g_b * jax.nn.sigmoid(g_b))).astype(BF16)
    a_proj = _dot(a_ref[...], wa_ref[...])
    b_proj = _dot(b_out, wb_ref[...])
    mix = (jax.nn.sigmoid(gate_a) * a_proj + jax.nn.sigmoid(gate_b) * b_proj).astype(BF16)
    o_ref[...] = x + _rms(_dot(mix, wo_ref[...]), post_ref[...])


def _mixout(x, a_out, bn, layer, pre, post, w_in, wa, wb, wo, tm):
    n = x.shape[0]
    row = lambda width: pl.BlockSpec((tm, width), lambda i: (i, 0))
    return pl.pallas_call(
        _mixout_kernel,
        grid=(n // tm,),
        in_specs=[row(D_MODEL), row(A_WIDTH), row(B_WIDTH), _vec(layer), _vec(layer),
                  _layer_block((D_MODEL, W_BLOCK), layer, 3), _layer_block((D_MODEL, W_BLOCK), layer, 4),
                  _layer_block((A_WIDTH, D_MODEL), layer), _layer_block((B_WIDTH, D_MODEL), layer),
                  _layer_block((D_MODEL, D_MODEL), layer)],
        out_specs=row(D_MODEL),
        out_shape=jax.ShapeDtypeStruct((n, D_MODEL), F32),
        compiler_params=_params(("parallel",)),
        name="mixout",
    )(x, a_out, bn, pre, post, w_in, w_in, wa, wb, wo)


def _rope_tables(pos):
    half = ROPE_DIM // 2
    inv = ROPE_THETA ** (-jnp.arange(half, dtype=F32) / half)
    dim = jnp.arange(LANES) % A_HEAD_DIM
    ang = pos.astype(F32)[:, None] * inv[dim % half][None, :]
    cos, sin = jnp.cos(ang), jnp.sin(ang)
    lo, hi = (dim < half)[None, :], jnp.logical_and(dim >= half, dim < ROPE_DIM)[None, :]
    return (jnp.where(jnp.logical_or(lo, hi), cos, 1.0), jnp.where(lo, -sin, 0.0), jnp.where(hi, sin, 0.0))


def kernel(x_prompt, x_sample, cache_k, cache_v, state_hgrn, ffn1_norm_pre, ffn1_norm_post, ffn1_w_gate, ffn1_w_up, ffn1_w_down, mix_norm_pre, mix_norm_post, w_in, hgrn_lb_logits, hgrn_out_norm, w_a_out, w_b_out, w_mix_out, ffn2_norm_pre, ffn2_norm_post, ffn2_w_gate, ffn2_w_up, ffn2_w_down):
    batch, s_len, _ = x_prompt.shape
    n_dec, t_dec, _ = x_sample.shape
    assert batch == 1 and t_dec == 1 and s_len % ATTN_TILE == 0
    tm = 512
    th = min(4096, s_len)
    cache_kt = jnp.transpose(cache_k, (0, 1, 3, 4, 2))
    cache_vt = jnp.transpose(cache_v, (0, 1, 3, 4, 2))
    yp = x_prompt.reshape(s_len, D_MODEL)
    ys = x_sample.reshape(n_dec, D_MODEL)
    tabs_p = _rope_tables(jnp.arange(s_len))
    tabs_s = _rope_tables(jnp.full((n_dec,), PAST_LEN))
    lbl = hgrn_lb_logits.astype(F32)
    vecs = lambda a: a.reshape(DEPTH, 1, -1).astype(F32)
    w16 = lambda w: w.astype(BF16)
    w_len = min(A_MAX_WINDOW, s_len)
    f1 = (vecs(ffn1_norm_pre), vecs(ffn1_norm_post), w16(ffn1_w_gate), w16(ffn1_w_up), w16(ffn1_w_down))
    f2 = (vecs(ffn2_norm_pre), vecs(ffn2_norm_post), w16(ffn2_w_gate), w16(ffn2_w_up), w16(ffn2_w_down))
    win = w16(w_in)
    m_pre, m_post, nrm = vecs(mix_norm_pre), vecs(mix_norm_post), vecs(hgrn_out_norm)
    mo = (m_pre, m_post, win, w16(w_a_out), w16(w_b_out), w16(w_mix_out))
    kp, vp, sp, ksn, vsn = [], [], [], [], []
    states = None
    for l in range(DEPTH):
        yp = _ffn(yp, l, *f1, tm)
        q1, q4, q16, k1, k4, k16, v1, v4, v16, kf, vf = _inproj_attn(yp, m_pre, tabs_p, win, l, tm)
        hq, hk, hv, hg = _inproj_rec(yp, m_pre, lbl, win, l, tm)
        a_out = _attn_prompt(q1, q4, q16, k1, k4, k16, v1, v4, v16)
        ys = _ffn(ys, l, *f1, n_dec)
        qa, ka, va, sq, sk, sv, sg = _inproj_sample(ys, m_pre, tabs_s, lbl, win, l)
        bn, st_p, a_s, bn_s, states = _recurrent(l, hq, hk, hv, hg, nrm, qa, ka, va, cache_kt, cache_vt,
                                                 sq, sk, sv, sg, state_hgrn, states, th)
        yp = _mixout(yp, a_out, bn, l, *mo, tm)
        yp = _ffn(yp, l, *f2, tm)
        kp.append(kf.reshape(batch, w_len, A_HEADS, A_HEAD_DIM))
        vp.append(vf.reshape(batch, w_len, A_HEADS, A_HEAD_DIM))
        sp.append(st_p.reshape(batch, B_HEADS, B_HEAD_DIM, B_HEAD_DIM))
        ys = _mixout(ys, a_s, bn_s, l, *mo, n_dec)
        ys = _ffn(ys, l, *f2, n_dec)
        ksn.append(ka.reshape(n_dec, t_dec, A_HEADS, A_HEAD_DIM))
        vsn.append(va.reshape(n_dec, t_dec, A_HEADS, A_HEAD_DIM))
    return (yp.reshape(batch, s_len, D_MODEL), ys.reshape(n_dec, t_dec, D_MODEL),
            jnp.stack(kp), jnp.stack(vp), jnp.stack(sp), jnp.stack(ksn), jnp.stack(vsn), states)
```

```python
import functools

import jax
import jax.numpy as jnp
import numpy as np
from jax import lax
from jax.experimental import pallas as pl
from jax.experimental.pallas import tpu as pltpu

F32 = jnp.float32
BF16 = jnp.bfloat16

D_MODEL = 1024
DEPTH = 2
PAST_LEN = 16384
A_HEADS = 8
A_HEAD_DIM = 64
A_WIDTH = A_HEADS * A_HEAD_DIM
A_DILATIONS = (1, 4, 16)
A_SPAN = 128
A_MAX_WINDOW = 2048
A_SCALE = A_HEAD_DIM ** -0.5
ROPE_THETA = 500000.0
ROPE_DIM = A_HEAD_DIM // 4
B_HEADS = 8
B_HEAD_DIM = 128
B_WIDTH = B_HEADS * B_HEAD_DIM
B_CHUNK = 64
B_UNROLL = 8
B_SCALE = B_HEAD_DIM ** -0.5
D_FF = 2816
FF_CHUNK = 256
EPS = 1e-6
LOG2_E = 1.4426950408889634

W_BLOCK = 3 * A_WIDTH
assert 3 * B_WIDTH == 2 * W_BLOCK and B_WIDTH + 2 * D_MODEL == 2 * W_BLOCK
LANES = 128
N_PAIRS = A_WIDTH // LANES
ATTN_TILE = A_SPAN * max(A_DILATIONS)
A_UNROLL = 8
VMEM_LIMIT = 56 * 1024 * 1024


def _params(sem, vmem=VMEM_LIMIT):
    return pltpu.CompilerParams(dimension_semantics=sem, vmem_limit_bytes=vmem)


def _resident(shape):
    nd = len(shape)
    return pl.BlockSpec(shape, lambda *_: (0,) * nd, pipeline_mode=pl.Buffered(1))


def _layer_block(shape, layer, col=0):
    index = (layer,) + (0,) * (len(shape) - 1) + (col,)
    return pl.BlockSpec((None,) + tuple(shape), lambda *_: index, pipeline_mode=pl.Buffered(1))


def _vec(layer):
    return _layer_block((1, D_MODEL), layer)


def _rms(x, g):
    y = x * lax.rsqrt(jnp.mean(x * x, axis=-1, keepdims=True) + EPS)
    return y * g


def _dot(a, b):
    return jnp.dot(a, b, preferred_element_type=F32)


def _dot_nt(a, b):
    return lax.dot_general(a, b, (((1,), (1,)), ((), ())), preferred_element_type=F32)


def _dot_tn(a, b):
    return lax.dot_general(a, b, (((0,), (0,)), ((), ())), preferred_element_type=F32)


def _ffn_kernel(x_ref, pre_ref, post_ref, wg_ref, wu_ref, wd_ref, o_ref):
    x = x_ref[...]
    h = _rms(x, pre_ref[...]).astype(BF16)
    g = _dot(h, wg_ref[...])
    u = _dot(h, wu_ref[...])
    a = (g * jax.nn.sigmoid(g) * u).astype(BF16)
    y = _dot(a, wd_ref[...])
    o_ref[...] = x + 0.5 * _rms(y, post_ref[...])


def _ffn(x, layer, pre, post, wg, wu, wd, tm):
    n = x.shape[0]
    row = pl.BlockSpec((tm, D_MODEL), lambda i: (i, 0))
    return pl.pallas_call(
        _ffn_kernel,
        grid=(n // tm,),
        in_specs=[row, _vec(layer), _vec(layer), _resident((D_MODEL, D_FF)),
                  _resident((D_MODEL, D_FF)), _resident((D_FF, D_MODEL))],
        out_specs=row,
        out_shape=jax.ShapeDtypeStruct((n, D_MODEL), F32),
        compiler_params=_params(("parallel",)),
        name="ffn",
    )(x, pre, post, wg, wu, wd)


def _ffn_sample_kernel(x_ref, pre_ref, post_ref, wg_ref, wu_ref, wd_ref,
                       o_ref, wg16_ref, wu16_ref, wd16_ref, h_scr, acc_scr):
    c = pl.program_id(0)

    @pl.when(c == 0)
    def _():
        h_scr[...] = _rms(x_ref[...], pre_ref[...]).astype(BF16)
        acc_scr[...] = jnp.zeros_like(acc_scr)

    wg, wu, wd = wg_ref[...].astype(BF16), wu_ref[...].astype(BF16), wd_ref[...].astype(BF16)
    wg16_ref[...] = wg
    wu16_ref[...] = wu
    wd16_ref[...] = wd
    h = h_scr[...]
    g = _dot(h, wg)
    u = _dot(h, wu)
    acc_scr[...] += _dot((g * jax.nn.sigmoid(g) * u).astype(BF16), wd)

    @pl.when(c == pl.num_programs(0) - 1)
    def _():
        o_ref[...] = x_ref[...] + 0.5 * _rms(acc_scr[...], post_ref[...])


def _ffn_sample(x, layer, pre, post, wg, wu, wd):
    n = x.shape[0]
    full = pl.BlockSpec((n, D_MODEL), lambda c: (0, 0))
    col = lambda: pl.BlockSpec((None, D_MODEL, FF_CHUNK), lambda c: (layer, 0, c))
    col16 = lambda: pl.BlockSpec((D_MODEL, FF_CHUNK), lambda c: (0, c))
    return pl.pallas_call(
        _ffn_sample_kernel,
        grid=(D_FF // FF_CHUNK,),
        in_specs=[full, _vec(layer), _vec(layer), col(), col(),
                  pl.BlockSpec((None, FF_CHUNK, D_MODEL), lambda c: (layer, c, 0))],
        out_specs=[full, col16(), col16(), pl.BlockSpec((FF_CHUNK, D_MODEL), lambda c: (c, 0))],
        out_shape=[jax.ShapeDtypeStruct((n, D_MODEL), F32), jax.ShapeDtypeStruct((D_MODEL, D_FF), BF16),
                   jax.ShapeDtypeStruct((D_MODEL, D_FF), BF16), jax.ShapeDtypeStruct((D_FF, D_MODEL), BF16)],
        scratch_shapes=[pltpu.VMEM((n, D_MODEL), BF16), pltpu.VMEM((n, D_MODEL), F32)],
        compiler_params=_params(("arbitrary",)),
        name="ffn_sample",
    )(x, pre, post, wg, wu, wd)


def _lower_bound(logits, layer):
    e = jnp.exp(logits - jnp.max(logits, axis=0, keepdims=True))
    sm = e / jnp.sum(e, axis=0, keepdims=True)
    lb = jnp.zeros((1, B_WIDTH), F32)
    for i in range(1, layer + 1):
        lb = lb + sm[i:i + 1, :]
    return lb


def _rope_slab(t, cos, sin_lo, sin_hi):
    return t * cos + pltpu.roll(t, LANES - ROPE_DIM // 2, axis=1) * sin_lo + pltpu.roll(t, ROPE_DIM // 2, axis=1) * sin_hi


def _col_block(col):
    return pl.BlockSpec((D_MODEL, W_BLOCK), lambda *_: (0, col), pipeline_mode=pl.Buffered(1))


def _hgrn_gates(z0, z1, lb):
    split = W_BLOCK - B_WIDTH
    q = z0[:, :B_WIDTH] * B_SCALE
    f_raw = jnp.concatenate([z0[:, B_WIDTH:], z1[:, :split]], axis=1)
    i_raw = z1[:, split:]
    f = lb + (1.0 - lb) * jax.nn.sigmoid(f_raw)
    return q, 1.0 - f, i_raw * jax.nn.sigmoid(i_raw), jnp.log(f)


def _inproj_attn_kernel(tm, x_ref, pre_ref, cos_ref, slo_ref, shi_ref, wa_ref,
                        q1_ref, q4_ref, q16_ref, k1_ref, k4_ref, k16_ref, v1_ref, v4_ref, v16_ref,
                        kf_ref, vf_ref, stage_ref, stage4_ref):
    h = _rms(x_ref[...], pre_ref[...]).astype(BF16)
    za = _dot(h, wa_ref[...])
    cos, slo, shi = cos_ref[...], slo_ref[...], shi_ref[...]
    outs = ((q1_ref, q4_ref, q16_ref), (k1_ref, k4_ref, k16_ref), (v1_ref, v4_ref, v16_ref))
    for s in range(3 * N_PAIRS):
        kind, hp = divmod(s, N_PAIRS)
        t = za[:, s * LANES:(s + 1) * LANES]
        if kind < 2:
            t = _rope_slab(t, cos, slo, shi)
        if kind == 0:
            t = t * (A_SCALE * LOG2_E)
        if kind == 1:
            kf_ref[:, hp * LANES:(hp + 1) * LANES] = t
        if kind == 2:
            vf_ref[:, hp * LANES:(hp + 1) * LANES] = t
        stage_ref[s] = t
        o1, o4, o16 = outs[kind]
        o1[hp] = t.astype(BF16)
        for r in range(4):
            c4 = stage_ref[s, pl.ds(r, tm // 4, stride=4), :]
            o4[hp, r] = c4.astype(BF16)
            stage4_ref[s, r] = c4
        for r in range(4):
            for a in range(4):
                o16[hp, r + 4 * a] = stage4_ref[s, r, pl.ds(a, tm // 16, stride=4), :].astype(BF16)


def _inproj_attn(x, pre, tabs, w_in, layer, tm):
    s_len = x.shape[0]
    w_len = min(A_MAX_WINDOW, s_len)
    first = (s_len - w_len) // tm
    row = pl.BlockSpec((tm, D_MODEL), lambda i: (i, 0))
    tab = pl.BlockSpec((tm, LANES), lambda i: (i, 0))
    l1 = pl.BlockSpec((N_PAIRS, tm, LANES), lambda i: (0, i, 0))
    l4 = pl.BlockSpec((N_PAIRS, 4, tm // 4, LANES), lambda i: (0, 0, i, 0))
    l16 = pl.BlockSpec((N_PAIRS, 16, tm // 16, LANES), lambda i: (0, 0, i, 0))
    win = pl.BlockSpec((tm, A_WIDTH), lambda i: (jnp.maximum(i - first, 0), 0))
    s1 = jax.ShapeDtypeStruct((N_PAIRS, s_len, LANES), BF16)
    s4 = jax.ShapeDtypeStruct((N_PAIRS, 4, s_len // 4, LANES), BF16)
    s16 = jax.ShapeDtypeStruct((N_PAIRS, 16, s_len // 16, LANES), BF16)
    sw = jax.ShapeDtypeStruct((w_len, A_WIDTH), F32)
    return pl.pallas_call(
        functools.partial(_inproj_attn_kernel, tm),
        grid=(s_len // tm,),
        in_specs=[row, _vec(layer), tab, tab, tab, _col_block(0)],
        out_specs=[l1, l4, l16, l1, l4, l16, l1, l4, l16, win, win],
        out_shape=[s1, s4, s16, s1, s4, s16, s1, s4, s16, sw, sw],
        scratch_shapes=[pltpu.VMEM((3 * N_PAIRS, tm, LANES), F32), pltpu.VMEM((3 * N_PAIRS, 4, tm // 4, LANES), F32)],
        compiler_params=_params(("arbitrary",)),
        name="inproj_attn",
    )(x, pre, *tabs, w_in)


def _inproj_rec_kernel(layer, x_ref, pre_ref, lbl_ref, wb0_ref, wb1_ref, hq_ref, hk_ref, hv_ref, hg_ref):
    h = _rms(x_ref[...], pre_ref[...]).astype(BF16)
    q, k, v, g = _hgrn_gates(_dot(h, wb0_ref[...]), _dot(h, wb1_ref[...]), _lower_bound(lbl_ref[...], layer))
    for hd in range(B_HEADS):
        sl = slice(hd * B_HEAD_DIM, (hd + 1) * B_HEAD_DIM)
        hq_ref[hd] = q[:, sl]
        hk_ref[hd] = k[:, sl]
        hv_ref[hd] = v[:, sl]
        hg_ref[hd] = g[:, sl]


def _inproj_rec(x, pre, lbl, w_in, layer, tm):
    s_len = x.shape[0]
    row = pl.BlockSpec((tm, D_MODEL), lambda i: (i, 0))
    hd = pl.BlockSpec((B_HEADS, tm, B_HEAD_DIM), lambda i: (0, i, 0))
    sh = jax.ShapeDtypeStruct((B_HEADS, s_len, B_HEAD_DIM), F32)
    return pl.pallas_call(
        functools.partial(_inproj_rec_kernel, layer),
        grid=(s_len // tm,),
        in_specs=[row, _vec(layer), _resident((DEPTH, B_WIDTH)), _col_block(1), _col_block(2)],
        out_specs=[hd] * 4,
        out_shape=[sh] * 4,
        compiler_params=_params(("parallel",)),
        name="inproj_rec",
    )(x, pre, lbl, w_in, w_in)


def _inproj_sample_kernel(layer, x_ref, pre_ref, cos_ref, slo_ref, shi_ref, lbl_ref, w_ref,
                          qa_ref, ka_ref, va_ref, hq_ref, hk_ref, hv_ref, hg_ref, w16_ref, h_scr, z_scr):
    c = pl.program_id(0)

    @pl.when(c == 0)
    def _():
        h_scr[...] = _rms(x_ref[...], pre_ref[...]).astype(BF16)

    w = w_ref[...].astype(BF16)
    w16_ref[...] = w

    @pl.when(c == 0)
    def _():
        za = _dot(h_scr[...], w)
        cos, slo, shi = cos_ref[...], slo_ref[...], shi_ref[...]
        outs = (qa_ref, ka_ref, va_ref)
        for s in range(3 * N_PAIRS):
            kind, hp = divmod(s, N_PAIRS)
            t = za[:, s * LANES:(s + 1) * LANES]
            if kind < 2:
                t = _rope_slab(t, cos, slo, shi)
            if kind == 0:
                t = t * A_SCALE
            outs[kind][:, hp * LANES:(hp + 1) * LANES] = t

    @pl.when(c == 1)
    def _():
        z_scr[...] = _dot(h_scr[...], w)

    @pl.when(c == 2)
    def _():
        q, k, v, g = _hgrn_gates(z_scr[...], _dot(h_scr[...], w), _lower_bound(lbl_ref[...], layer))
        hq_ref[...] = q
        hk_ref[...] = k
        hv_ref[...] = v
        hg_ref[...] = g


def _inproj_sample(x, pre, tabs, lbl, w_in, layer):
    n = x.shape[0]
    n_col = w_in.shape[-1] // W_BLOCK
    full = lambda *shape: pl.BlockSpec(shape, lambda c: (0,) * len(shape))
    sa = jax.ShapeDtypeStruct((n, A_WIDTH), F32)
    sb = jax.ShapeDtypeStruct((n, B_WIDTH), F32)
    return pl.pallas_call(
        functools.partial(_inproj_sample_kernel, layer),
        grid=(n_col,),
        in_specs=[full(n, D_MODEL), _vec(layer), full(n, LANES), full(n, LANES), full(n, LANES),
                  full(DEPTH, B_WIDTH), pl.BlockSpec((None, D_MODEL, W_BLOCK), lambda c: (layer, 0, c))],
        out_specs=[full(n, A_WIDTH)] * 3 + [full(n, B_WIDTH)] * 4
                  + [pl.BlockSpec((D_MODEL, W_BLOCK), lambda c: (0, c))],
        out_shape=[sa] * 3 + [sb] * 4 + [jax.ShapeDtypeStruct((D_MODEL, n_col * W_BLOCK), BF16)],
        scratch_shapes=[pltpu.VMEM((n, D_MODEL), BF16), pltpu.VMEM((n, W_BLOCK), F32)],
        compiler_params=_params(("arbitrary",)),
        name="inproj_sample",
    )(x, pre, *tabs, lbl, w_in)


def _attn_scores(q, k, bias):
    first = lax.broadcasted_iota(jnp.int32, (A_SPAN, LANES), 1) < A_HEAD_DIM
    zero = jnp.zeros_like(q)
    q2 = jnp.concatenate([jnp.where(first, q, zero), jnp.where(first, zero, q)], axis=0)
    s = _dot_nt(q2, k) + bias
    m = jnp.max(jnp.maximum(s[:, :A_SPAN], s[:, A_SPAN:]), axis=-1, keepdims=True)
    return jnp.exp2(s - m).astype(BF16), jnp.broadcast_to(m, (2 * A_SPAN, LANES))


def _attn_values(p, m, v_ext):
    first = lax.broadcasted_iota(jnp.int32, (A_SPAN, LANES), 1) < A_HEAD_DIM
    oe = _dot(p, v_ext)
    pick = lambda t: jnp.where(first, t[:A_SPAN], t[A_SPAN:])
    return pick(oe[:, :LANES]), pick(m), pick(oe[:, LANES:])


def _attn_prompt_kernel(q1_ref, q4_ref, q16_ref, k1_ref, k4_ref, k16_ref, v1_ref, v4_ref, v16_ref,
                        o_ref, ks1, ks4, ks16, vs1, vs4, vs16, bias_scr, o_scr, m_scr, d_scr, p_scr, mx_scr):
    j = pl.program_id(1)
    q_refs = (q1_ref, q4_ref, q16_ref)
    k_in = (k1_ref, k4_ref, k16_ref)
    v_in = (v1_ref, v4_ref, v16_ref)
    k_scr = (ks1, ks4, ks16)
    v_scr = (vs1, vs4, vs16)

    @pl.when(j == 0)
    def _():
        for scr in k_scr:
            scr[:, 0:A_SPAN, :] = jnp.zeros((scr.shape[0], A_SPAN, LANES), BF16)
        for scr in v_scr:
            scr[:, 0:A_SPAN, 0:LANES] = jnp.zeros((scr.shape[0], A_SPAN, LANES), BF16)
            scr[:, :, LANES:] = jnp.ones((scr.shape[0], scr.shape[1], LANES), BF16)
        qi = lax.broadcasted_iota(jnp.int32, (2 * A_SPAN, 2 * A_SPAN), 0) % A_SPAN
        ki = lax.broadcasted_iota(jnp.int32, (2 * A_SPAN, 2 * A_SPAN), 1)
        band = jnp.logical_and(ki >= qi, ki <= qi + A_SPAN)
        bias_scr[1] = jnp.where(band, 0.0, -jnp.inf)
        bias_scr[0] = jnp.where(jnp.logical_and(band, ki >= A_SPAN), 0.0, -jnp.inf)

    for p, d in enumerate(A_DILATIONS):
        k_scr[p][:, A_SPAN:, :] = k_in[p][...]
        v_scr[p][:, A_SPAN:, 0:LANES] = v_in[p][...]

    def locate(p, grp, u):
        per_class = ATTN_TILE // A_DILATIONS[p] // A_SPAN
        b = grp * A_UNROLL + u
        n = b % per_class
        return b // per_class, pl.multiple_of(n * A_SPAN, A_SPAN), n

    def front(p, grp):
        for u in range(A_UNROLL):
            r, lo, n = locate(p, grp, u)
            has_prev = jnp.logical_or(j > 0, n > 0).astype(jnp.int32)
            p_scr[u], mx_scr[u] = _attn_scores(q_refs[p][r, pl.ds(lo, A_SPAN), :],
                                               k_scr[p][r, pl.ds(lo, 2 * A_SPAN), :], bias_scr[has_prev])

    def back(p, grp):
        d = A_DILATIONS[p]
        for u in range(A_UNROLL):
            r, lo, n = locate(p, grp, u)
            acc, m, den = _attn_values(p_scr[u], mx_scr[u], v_scr[p][r, pl.ds(lo, 2 * A_SPAN), :])
            if d > 1:
                rows = pl.ds(n * (A_SPAN * d) + r, A_SPAN, stride=d)
                o_scr[p - 1, rows, :] = acc
                m_scr[p - 1, rows, :] = m
                d_scr[p - 1, rows, :] = den
            else:
                rows = pl.ds(lo, A_SPAN)
                m1, m2 = m_scr[0, rows, :], m_scr[1, rows, :]
                top = jnp.maximum(jnp.maximum(m, m1), m2)
                w0, w1, w2 = jnp.exp2(m - top), jnp.exp2(m1 - top), jnp.exp2(m2 - top)
                num = w0 * acc + w1 * o_scr[0, rows, :] + w2 * o_scr[1, rows, :]
                tot = w0 * den + w1 * d_scr[0, rows, :] + w2 * d_scr[1, rows, :]
                o_ref[rows, :] = (num / tot).astype(BF16)

    n_grp = ATTN_TILE // A_SPAN // A_UNROLL
    order = tuple(range(1, len(A_DILATIONS))) + (0,)
    n_dyn = n_grp + jnp.minimum(j, 0)
    front(order[0], 0)
    for idx, p in enumerate(order):

        def steady(grp, carry, p=p):
            back(p, grp - 1)
            front(p, grp)
            return carry

        lax.fori_loop(1, n_dyn, steady, 0)
        back(p, n_grp - 1)
        if idx + 1 < len(order):
            front(order[idx + 1], 0)

    for p, d in enumerate(A_DILATIONS):
        tail = ATTN_TILE // d
        k_scr[p][:, 0:A_SPAN, :] = k_scr[p][:, tail:tail + A_SPAN, :]
        v_scr[p][:, 0:A_SPAN, 0:LANES] = v_scr[p][:, tail:tail + A_SPAN, 0:LANES]


def _attn_prompt(q1, q4, q16, k1, k4, k16, v1, v4, v16):
    s_len = q1.shape[1]
    t = ATTN_TILE
    b1 = pl.BlockSpec((None, 1, t, LANES), lambda hp, j: (hp, 0, j, 0))
    b4 = pl.BlockSpec((None, 4, t // 4, LANES), lambda hp, j: (hp, 0, j, 0))
    b16 = pl.BlockSpec((None, 16, t // 16, LANES), lambda hp, j: (hp, 0, j, 0))
    as4 = lambda a: a.reshape(N_PAIRS, 1, s_len, LANES)
    scr = lambda d, width: pltpu.VMEM((d, A_SPAN + t // d, width), BF16)
    n_dil = len(A_DILATIONS) - 1
    return pl.pallas_call(
        _attn_prompt_kernel,
        grid=(N_PAIRS, s_len // t),
        in_specs=[b1, b4, b16] * 3,
        out_specs=pl.BlockSpec((t, LANES), lambda hp, j: (j, hp)),
        out_shape=jax.ShapeDtypeStruct((s_len, A_WIDTH), BF16),
        scratch_shapes=[scr(d, LANES) for d in A_DILATIONS] + [scr(d, 2 * LANES) for d in A_DILATIONS]
                       + [pltpu.VMEM((2, 2 * A_SPAN, 2 * A_SPAN), F32)] + [pltpu.VMEM((n_dil, t, LANES), F32)] * 3
                       + [pltpu.VMEM((A_UNROLL, 2 * A_SPAN, 2 * A_SPAN), BF16),
                          pltpu.VMEM((A_UNROLL, 2 * A_SPAN, LANES), F32)],
        compiler_params=_params(("arbitrary", "arbitrary")),
        name="attn_prompt",
    )(as4(q1), q4, q16, as4(k1), k4, k16, as4(v1), v4, v16)


def _decode_rows(bb, w_c, q_ref, k_ref, v_ref, kt_ref, vt_ref,
                 hq_ref, hk_ref, hv_ref, hg_ref, st_ref, nrm_ref, a_ref, bn_ref, so_ref):
    n_pat = len(A_DILATIONS)
    dist = w_c - lax.broadcasted_iota(jnp.int32, (A_HEADS, w_c), 1)
    cnt = jnp.zeros((A_HEADS, w_c), F32)
    for d in A_DILATIONS:
        cnt = cnt + jnp.logical_and(dist % d == 0, dist <= A_SPAN * d).astype(F32)
    used = cnt > 0.0
    head_s = lax.broadcasted_iota(jnp.int32, (A_HEADS, w_c), 0)
    head_o = lax.broadcasted_iota(jnp.int32, (A_HEADS, A_HEAD_DIM), 0)
    for b in range(bb):
        q = q_ref[b]
        k_new, v_new = k_ref[b], v_ref[b]
        q16 = q.astype(BF16)
        s_new = jnp.sum(q * k_new, axis=-1, keepdims=True)
        s = jnp.zeros((A_HEADS, w_c), F32)
        for h in range(A_HEADS):
            s = jnp.where(head_s == h, _dot(q16, kt_ref[b, h].astype(BF16)), s)
        m = jnp.maximum(jnp.max(jnp.where(used, s, -jnp.inf), axis=-1, keepdims=True), s_new)
        w = jnp.where(used, jnp.exp(s - m), 0.0) * cnt
        p_new = n_pat * jnp.exp(s_new - m)
        den = jnp.sum(w, axis=-1, keepdims=True) + p_new
        acc = p_new * v_new
        w16 = w.astype(BF16)
        for h in range(A_HEADS):
            acc = acc + jnp.where(head_o == h, _dot_nt(w16, vt_ref[b, h].astype(BF16)), 0.0)
        a_ref[b] = acc / den
        v_rows = hv_ref[b]
        packed = jnp.concatenate([hq_ref[b], hk_ref[b], jnp.exp(hg_ref[b]),
                                  jnp.zeros((LANES - 3 * B_HEADS, B_HEAD_DIM), F32)], axis=0)
        cols = packed.T
        outs = []
        for hd in range(B_HEADS):
            col = lambda i: cols[:, i * B_HEADS + hd:i * B_HEADS + hd + 1]
            st = col(2) * st_ref[b, hd] + col(1) * v_rows[hd:hd + 1, :]
            so_ref[b, hd] = st
            o = jnp.sum(col(0) * st, axis=0, keepdims=True)
            outs.append(_rms(o, nrm_ref[hd:hd + 1, :]))
        bn_ref[b] = jnp.concatenate(outs, axis=0)


N_DECODE_IN = 11


def _recurrent_kernel(th, bb, w_c, q_ref, k_ref, v_ref, g_ref, nrm_ref, *rest):
    dec_in = rest[:N_DECODE_IN]
    o_ref, st_ref, da_ref, dbn_ref, dso_ref, state_scr, b_scr, a_scr, part_scr = rest[-9:]
    j = pl.program_id(1)

    @pl.when(j == 0)
    def _():
        state_scr[...] = jnp.zeros_like(state_scr)

    piece = B_CHUNK * B_UNROLL
    rin = lax.broadcasted_iota(jnp.int32, (piece, B_HEAD_DIM), 0) & (B_CHUNK - 1)

    def scan_piece(i, carry):
        rows = pl.ds(pl.multiple_of(i * piece, piece), piece)
        b = g_ref[rows, :] * LOG2_E
        shift = 1
        while shift < B_CHUNK:
            b = b + jnp.where(rin >= shift, pltpu.roll(b, shift, axis=0), 0.0)
            shift *= 2
        b_scr[rows, :] = b
        return carry

    lax.fori_loop(0, th // piece, scan_piece, 0)

    row = lax.broadcasted_iota(jnp.int32, (B_CHUNK, B_HEAD_DIM), 0)
    a_t = lax.broadcasted_iota(jnp.int32, (B_CHUNK, B_CHUNK), 0)
    a_s = lax.broadcasted_iota(jnp.int32, (B_CHUNK, B_CHUNK), 1)
    levels = []
    half = B_CHUNK // 2
    while half >= 1:
        seg = 2 * half
        pair = jnp.logical_and(a_t // seg == a_s // seg,
                               jnp.logical_and(a_t % seg >= half, a_s % seg < half))
        upper = row % seg >= half
        levels.append((half, upper, jnp.where(upper, 1.0, -1.0), pair))
        half //= 2
    n_grp = B_CHUNK // 8
    sub8 = lax.broadcasted_iota(jnp.int32, (n_grp, 8, B_HEAD_DIM), 1)
    nrm = nrm_ref[...]

    def centre_value(bc, half):
        seg = 2 * half
        if seg >= 8:
            return jnp.concatenate(
                [jnp.broadcast_to(bc[s0 + half - 1:s0 + half, :], (seg, B_HEAD_DIM))
                 for s0 in range(0, B_CHUNK, seg)], axis=0)
        b3 = bc.reshape(n_grp, 8, B_HEAD_DIM)
        if half == 1:
            out = jnp.where(sub8 % 2 == 0, b3, pltpu.roll(b3, 1, axis=1))
        else:
            out = jnp.broadcast_to(b3[:, half - 1:half, :], b3.shape)
            for s0 in range(seg, 8, seg):
                out = jnp.where(sub8 >= s0, jnp.broadcast_to(b3[:, s0 + half - 1:s0 + half, :], b3.shape), out)
        return out.reshape(B_CHUNK, B_HEAD_DIM)

    def intra_chunk(q, k, v, bc):
        a = jnp.zeros((B_CHUNK, B_CHUNK), F32)
        for half, upper, sign, pair in levels:
            x = (jnp.where(upper, q, k) * jnp.exp2((bc - centre_value(bc, half)) * sign)).astype(BF16)
            a = jnp.where(pair, _dot_nt(x, x), a)
        return a, jnp.sum(q * k, axis=-1, keepdims=True) * v

    def chunk_rows(grp, u):
        return pl.ds(pl.multiple_of((grp * B_UNROLL + u) * B_CHUNK, B_CHUNK), B_CHUNK)

    def front(grp):
        st = state_scr[...]
        for u in range(B_UNROLL):
            rows = chunk_rows(grp, u)
            q, k, v, bc = q_ref[rows, :], k_ref[rows, :], v_ref[rows, :], b_scr[rows, :]
            o = _dot_nt((q * jnp.exp2(bc)).astype(BF16), st.astype(BF16))
            b_last = bc[B_CHUNK - 1:B_CHUNK, :]
            k_dec = (k * jnp.exp2(b_last - bc)).astype(BF16)
            st = st * jnp.exp2(b_last) + _dot_tn(v.astype(BF16), k_dec)
            a, o3 = intra_chunk(q, k, v, bc)
            a_scr[u] = a.astype(BF16)
            part_scr[u] = o + o3
        state_scr[...] = st

    def back(grp):
        for u in range(B_UNROLL):
            rows = chunk_rows(grp, u)
            o = part_scr[u] + _dot(a_scr[u], v_ref[rows, :].astype(BF16))
            o_ref[rows, :] = _rms(o, nrm)

    n_groups = th // (B_CHUNK * B_UNROLL)
    front(0)

    def steady(grp, carry):
        back(grp - 1)
        front(grp)
        return carry

    lax.fori_loop(1, n_groups, steady, 0)
    back(n_groups - 1)
    _decode_rows(bb, w_c, *dec_in, da_ref, dbn_ref, dso_ref)

    @pl.when(j == pl.num_programs(1) - 1)
    def _():
        st_ref[...] = state_scr[...].T


def _recurrent(layer, q, k, v, g, nrm, dq, dk, dv, cache_kt, cache_vt, dhq, dhk, dhv, dhg, state, states_out, th):
    s_len = q.shape[1]
    n = dq.shape[0]
    w_c = cache_kt.shape[-1]
    n_tiles = s_len // th
    steps = B_HEADS * n_tiles
    assert w_c >= A_SPAN * max(A_DILATIONS) and n % steps == 0
    bb = n // steps
    step = lambda h, j: h * n_tiles + j
    blk = pl.BlockSpec((None, th, B_HEAD_DIM), lambda h, j: (h, j, 0))
    a_spec = pl.BlockSpec((bb, A_HEADS, A_HEAD_DIM), lambda h, j: (step(h, j), 0, 0))
    headspec = pl.BlockSpec((bb, B_HEADS, B_HEAD_DIM), lambda h, j: (step(h, j), 0, 0))
    c_spec = pl.BlockSpec((None, bb, A_HEADS, A_HEAD_DIM, w_c), lambda h, j: (layer, step(h, j), 0, 0, 0))
    st_spec = pl.BlockSpec((None, bb, B_HEADS, B_HEAD_DIM, B_HEAD_DIM), lambda h, j: (layer, step(h, j), 0, 0, 0))
    a_heads = lambda a: a.reshape(n, A_HEADS, A_HEAD_DIM)
    heads = lambda a: a.reshape(n, B_HEADS, B_HEAD_DIM)
    in_specs = [blk] * 4 + [pl.BlockSpec((None, 1, B_HEAD_DIM), lambda h, j: (layer, 0, h))]
    in_specs += [a_spec] * 3 + [c_spec] * 2 + [headspec] * 4 + [st_spec, _layer_block((B_HEADS, B_HEAD_DIM), layer)]
    args = [q, k, v, g, nrm, a_heads(dq), a_heads(dk), a_heads(dv), cache_kt, cache_vt,
            heads(dhq), heads(dhk), heads(dhv), heads(dhg), state, nrm.reshape(DEPTH, B_HEADS, B_HEAD_DIM)]
    assert len(args) == 5 + N_DECODE_IN
    aliases = {}
    if states_out is not None:
        aliases = {len(args): 4}
        in_specs.append(pl.BlockSpec(memory_space=pl.ANY))
        args.append(states_out)
    bn, st_p, a_out, dbn, states = pl.pallas_call(
        functools.partial(_recurrent_kernel, th, bb, w_c),
        grid=(B_HEADS, n_tiles),
        in_specs=in_specs,
        out_specs=[pl.BlockSpec((th, B_HEAD_DIM), lambda h, j: (j, h)),
                   pl.BlockSpec((None, B_HEAD_DIM, B_HEAD_DIM), lambda h, j: (h, 0, 0)),
                   a_spec, headspec, st_spec],
        out_shape=[jax.ShapeDtypeStruct((s_len, B_WIDTH), F32),
                   jax.ShapeDtypeStruct((B_HEADS, B_HEAD_DIM, B_HEAD_DIM), F32),
                   jax.ShapeDtypeStruct((n, A_HEADS, A_HEAD_DIM), F32),
                   jax.ShapeDtypeStruct((n, B_HEADS, B_HEAD_DIM), F32),
                   jax.ShapeDtypeStruct(state.shape, F32)],
        scratch_shapes=[pltpu.VMEM((B_HEAD_DIM, B_HEAD_DIM), F32), pltpu.VMEM((th, B_HEAD_DIM), F32),
                        pltpu.VMEM((B_UNROLL, B_CHUNK, B_CHUNK), BF16),
                        pltpu.VMEM((B_UNROLL, B_CHUNK, B_HEAD_DIM), F32)],
        input_output_aliases=aliases,
        compiler_params=_params(("arbitrary", "arbitrary")),
        name="recurrent",
    )(*args)
    return bn, st_p, a_out.reshape(n, A_WIDTH).astype(BF16), dbn.reshape(n, B_WIDTH), states


def _mix(x, a16, bn, pre, post, wg0, wg1, wa, wb, wo):
    h = _rms(x, pre).astype(BF16)
    z0 = _dot(h, wg0)
    z1 = _dot(h, wg1)
    split = W_BLOCK - B_WIDTH
    g_b = z0[:, :B_WIDTH]
    gate_a = jnp.concatenate([z0[:, B_WIDTH:], z1[:, :split]], axis=1)
    gate_b = z1[:, split:]
    b_out = (bn * (g_b * jax.nn.sigmoid(g_b))).astype(BF16)
    mix = (jax.nn.sigmoid(gate_a) * _dot(a16, wa) + jax.nn.sigmoid(gate_b) * _dot(b_out, wb)).astype(BF16)
    return x + _rms(_dot(mix, wo), post)


def _mixout_kernel(x_ref, a_ref, bn_ref, pre_ref, post_ref, wg0_ref, wg1_ref, wa_ref, wb_ref, wo_ref, o_ref):
    o_ref[...] = _mix(x_ref[...], a_ref[...], bn_ref[...], pre_ref[...], post_ref[...],
                      wg0_ref[...], wg1_ref[...], wa_ref[...], wb_ref[...], wo_ref[...])


def _mixout(x, a_out, bn, layer, pre, post, w_in, wa, wb, wo, tm):
    n = x.shape[0]
    row = lambda width: pl.BlockSpec((tm, width), lambda i: (i, 0))
    return pl.pallas_call(
        _mixout_kernel,
        grid=(n // tm,),
        in_specs=[row(D_MODEL), row(A_WIDTH), row(B_WIDTH), _vec(layer), _vec(layer), _col_block(3), _col_block(4),
                  _resident((A_WIDTH, D_MODEL)), _resident((B_WIDTH, D_MODEL)), _resident((D_MODEL, D_MODEL))],
        out_specs=row(D_MODEL),
        out_shape=jax.ShapeDtypeStruct((n, D_MODEL), F32),
        compiler_params=_params(("parallel",)),
        name="mixout",
    )(x, a_out, bn, pre, post, w_in, w_in, wa, wb, wo)


def _mixout_sample_kernel(x_ref, a_ref, bn_ref, pre_ref, post_ref, wg0_ref, wg1_ref, wa_ref, wb_ref, wo_ref,
                          o_ref, wa16_ref, wb16_ref, wo16_ref):
    wa, wb, wo = wa_ref[...].astype(BF16), wb_ref[...].astype(BF16), wo_ref[...].astype(BF16)
    wa16_ref[...] = wa
    wb16_ref[...] = wb
    wo16_ref[...] = wo
    o_ref[...] = _mix(x_ref[...], a_ref[...], bn_ref[...], pre_ref[...], post_ref[...],
                      wg0_ref[...], wg1_ref[...], wa, wb, wo)


def _mixout_sample(x, a_out, bn, layer, pre, post, w_in16, wa, wb, wo):
    n = x.shape[0]
    full = lambda *shape: pl.BlockSpec(shape, lambda i: (0,) * len(shape))
    shapes = ((A_WIDTH, D_MODEL), (B_WIDTH, D_MODEL), (D_MODEL, D_MODEL))
    return pl.pallas_call(
        _mixout_sample_kernel,
        grid=(1,),
        in_specs=[full(n, D_MODEL), full(n, A_WIDTH), full(n, B_WIDTH), _vec(layer), _vec(layer),
                  _col_block(3), _col_block(4)] + [_layer_block(s, layer) for s in shapes],
        out_specs=[full(n, D_MODEL)] + [full(*s) for s in shapes],
        out_shape=[jax.ShapeDtypeStruct((n, D_MODEL), F32)] + [jax.ShapeDtypeStruct(s, BF16) for s in shapes],
        compiler_params=_params(("arbitrary",)),
        name="mixout_sample",
    )(x, a_out, bn, pre, post, w_in16, w_in16, wa, wb, wo)


def _rope_tables(pos):
    half = ROPE_DIM // 2
    inv = ROPE_THETA ** (-jnp.arange(half, dtype=F32) / half)
    dim = jnp.arange(LANES) % A_HEAD_DIM
    ang = pos.astype(F32)[:, None] * inv[dim % half][None, :]
    cos, sin = jnp.cos(ang), jnp.sin(ang)
    lo, hi = (dim < half)[None, :], jnp.logical_and(dim >= half, dim < ROPE_DIM)[None, :]
    return (jnp.where(jnp.logical_or(lo, hi), cos, 1.0), jnp.where(lo, -sin, 0.0), jnp.where(hi, sin, 0.0))


def kernel(x_prompt, x_sample, cache_k, cache_v, state_hgrn, ffn1_norm_pre, ffn1_norm_post, ffn1_w_gate, ffn1_w_up, ffn1_w_down, mix_norm_pre, mix_norm_post, w_in, hgrn_lb_logits, hgrn_out_norm, w_a_out, w_b_out, w_mix_out, ffn2_norm_pre, ffn2_norm_post, ffn2_w_gate, ffn2_w_up, ffn2_w_down):
    batch, s_len, _ = x_prompt.shape
    n_dec, t_dec, _ = x_sample.shape
    assert batch == 1 and t_dec == 1 and s_len % ATTN_TILE == 0
    tm = 512
    th = min(4096, s_len)
    cache_kt = jnp.transpose(cache_k, (0, 1, 3, 4, 2))
    cache_vt = jnp.transpose(cache_v, (0, 1, 3, 4, 2))
    yp = x_prompt.reshape(s_len, D_MODEL)
    ys = x_sample.reshape(n_dec, D_MODEL)
    tabs_p = _rope_tables(jnp.arange(s_len))
    tabs_s = _rope_tables(jnp.full((n_dec,), PAST_LEN))
    lbl = hgrn_lb_logits.astype(F32)
    vecs = lambda a: a.reshape(DEPTH, 1, -1).astype(F32)
    w_len = min(A_MAX_WINDOW, s_len)
    f1_pre, f1_post, f2_pre, f2_post = (vecs(a) for a in (ffn1_norm_pre, ffn1_norm_post, ffn2_norm_pre, ffn2_norm_post))
    m_pre, m_post, nrm = vecs(mix_norm_pre), vecs(mix_norm_post), vecs(hgrn_out_norm)
    kp, vp, sp, ksn, vsn = [], [], [], [], []
    states = None
    for l in range(DEPTH):
        ys, *f1 = _ffn_sample(ys, l, f1_pre, f1_post, ffn1_w_gate, ffn1_w_up, ffn1_w_down)
        qa, ka, va, sq, sk, sv, sg, win = _inproj_sample(ys, m_pre, tabs_s, lbl, w_in, l)
        yp = _ffn(yp, l, f1_pre, f1_post, *f1, tm)
        q1, q4, q16, k1, k4, k16, v1, v4, v16, kf, vf = _inproj_attn(yp, m_pre, tabs_p, win, l, tm)
        hq, hk, hv, hg = _inproj_rec(yp, m_pre, lbl, win, l, tm)
        a_out = _attn_prompt(q1, q4, q16, k1, k4, k16, v1, v4, v16)
        bn, st_p, a_s, bn_s, states = _recurrent(l, hq, hk, hv, hg, nrm, qa, ka, va, cache_kt, cache_vt,
                                                 sq, sk, sv, sg, state_hgrn, states, th)
        ys, *mo = _mixout_sample(ys, a_s, bn_s, l, m_pre, m_post, win, w_a_out, w_b_out, w_mix_out)
        ys, *f2 = _ffn_sample(ys, l, f2_pre, f2_post, ffn2_w_gate, ffn2_w_up, ffn2_w_down)
        yp = _mixout(yp, a_out, bn, l, m_pre, m_post, win, *mo, tm)
        yp = _ffn(yp, l, f2_pre, f2_post, *f2, tm)
        kp.append(kf.reshape(batch, w_len, A_HEADS, A_HEAD_DIM))
        vp.append(vf.reshape(batch, w_len, A_HEADS, A_HEAD_DIM))
        sp.append(st_p.reshape(batch, B_HEADS, B_HEAD_DIM, B_HEAD_DIM))
        ksn.append(ka.reshape(n_dec, t_dec, A_HEADS, A_HEAD_DIM))
        vsn.append(va.reshape(n_dec, t_dec, A_HEADS, A_HEAD_DIM))
    return (yp.reshape(batch, s_len, D_MODEL), ys.reshape(n_dec, t_dec, D_MODEL),
            jnp.stack(kp), jnp.stack(vp), jnp.stack(sp), jnp.stack(ksn), jnp.stack(vsn), states)
```

```python
import functools

import jax
import jax.numpy as jnp
from jax import lax
from jax.experimental import pallas as pl
from jax.experimental.pallas import tpu as pltpu

F32 = jnp.float32
BF16 = jnp.bfloat16

D_MODEL = 1024
DEPTH = 2
PAST_LEN = 16384
A_HEADS = 8
A_HEAD_DIM = 64
A_WIDTH = A_HEADS * A_HEAD_DIM
A_DILATIONS = (1, 4, 16)
A_SPAN = 128
A_MAX_WINDOW = 2048
A_SCALE = A_HEAD_DIM ** -0.5
ROPE_THETA = 500000.0
ROPE_DIM = A_HEAD_DIM // 4
B_HEADS = 8
B_HEAD_DIM = 128
B_WIDTH = B_HEADS * B_HEAD_DIM
B_CHUNK = 64
B_UNROLL = 8
B_SCALE = B_HEAD_DIM ** -0.5
D_FF = 2816
FF_CHUNK = 256
EPS = 1e-6
LOG2_E = 1.4426950408889634

W_BLOCK = 3 * A_WIDTH
assert 3 * B_WIDTH == 2 * W_BLOCK and B_WIDTH + 2 * D_MODEL == 2 * W_BLOCK
LANES = 128
N_PAIRS = A_WIDTH // LANES
ATTN_TILE = A_SPAN * max(A_DILATIONS)
A_UNROLL = 8
VMEM_LIMIT = 56 * 1024 * 1024


def _params(sem, vmem=VMEM_LIMIT):
    return pltpu.CompilerParams(dimension_semantics=sem, vmem_limit_bytes=vmem)


def _resident(shape):
    nd = len(shape)
    return pl.BlockSpec(shape, lambda *_: (0,) * nd, pipeline_mode=pl.Buffered(1))


def _layer_block(shape, layer, col=0):
    index = (layer,) + (0,) * (len(shape) - 1) + (col,)
    return pl.BlockSpec((None,) + tuple(shape), lambda *_: index, pipeline_mode=pl.Buffered(1))


def _vec(layer):
    return _layer_block((1, D_MODEL), layer)


def _rms(x, g):
    y = x * lax.rsqrt(jnp.mean(x * x, axis=-1, keepdims=True) + EPS)
    return y * g


def _dot(a, b):
    return jnp.dot(a, b, preferred_element_type=F32)


def _dot_nt(a, b):
    return lax.dot_general(a, b, (((1,), (1,)), ((), ())), preferred_element_type=F32)


def _dot_tn(a, b):
    return lax.dot_general(a, b, (((0,), (0,)), ((), ())), preferred_element_type=F32)


def _ffn_kernel(x_ref, pre_ref, post_ref, wg_ref, wu_ref, wd_ref, o_ref):
    x = x_ref[...]
    h = _rms(x, pre_ref[...]).astype(BF16)
    g = _dot(h, wg_ref[...])
    u = _dot(h, wu_ref[...])
    a = (g * jax.nn.sigmoid(g) * u).astype(BF16)
    y = _dot(a, wd_ref[...])
    o_ref[...] = x + 0.5 * _rms(y, post_ref[...])


def _ffn(x, layer, pre, post, wg, wu, wd, tm):
    n = x.shape[0]
    row = pl.BlockSpec((tm, D_MODEL), lambda i: (i, 0))
    return pl.pallas_call(
        _ffn_kernel,
        grid=(n // tm,),
        in_specs=[row, _vec(layer), _vec(layer), _resident((D_MODEL, D_FF)),
                  _resident((D_MODEL, D_FF)), _resident((D_FF, D_MODEL))],
        out_specs=row,
        out_shape=jax.ShapeDtypeStruct((n, D_MODEL), F32),
        compiler_params=_params(("parallel",)),
        name="ffn",
    )(x, pre, post, wg, wu, wd)


def _ffn_sample_kernel(x_ref, pre_ref, post_ref, wg_ref, wu_ref, wd_ref,
                       o_ref, wg16_ref, wu16_ref, wd16_ref, h_scr, acc_scr):
    c = pl.program_id(0)

    @pl.when(c == 0)
    def _():
        h_scr[...] = _rms(x_ref[...], pre_ref[...]).astype(BF16)
        acc_scr[...] = jnp.zeros_like(acc_scr)

    wg, wu, wd = wg_ref[...].astype(BF16), wu_ref[...].astype(BF16), wd_ref[...].astype(BF16)
    wg16_ref[...] = wg
    wu16_ref[...] = wu
    wd16_ref[...] = wd
    h = h_scr[...]
    g = _dot(h, wg)
    u = _dot(h, wu)
    acc_scr[...] += _dot((g * jax.nn.sigmoid(g) * u).astype(BF16), wd)

    @pl.when(c == pl.num_programs(0) - 1)
    def _():
        o_ref[...] = x_ref[...] + 0.5 * _rms(acc_scr[...], post_ref[...])


def _ffn_sample(x, layer, pre, post, wg, wu, wd):
    n = x.shape[0]
    full = pl.BlockSpec((n, D_MODEL), lambda c: (0, 0))
    col = lambda: pl.BlockSpec((None, D_MODEL, FF_CHUNK), lambda c: (layer, 0, c))
    col16 = lambda: pl.BlockSpec((D_MODEL, FF_CHUNK), lambda c: (0, c))
    return pl.pallas_call(
        _ffn_sample_kernel,
        grid=(D_FF // FF_CHUNK,),
        in_specs=[full, _vec(layer), _vec(layer), col(), col(),
                  pl.BlockSpec((None, FF_CHUNK, D_MODEL), lambda c: (layer, c, 0))],
        out_specs=[full, col16(), col16(), pl.BlockSpec((FF_CHUNK, D_MODEL), lambda c: (c, 0))],
        out_shape=[jax.ShapeDtypeStruct((n, D_MODEL), F32), jax.ShapeDtypeStruct((D_MODEL, D_FF), BF16),
                   jax.ShapeDtypeStruct((D_MODEL, D_FF), BF16), jax.ShapeDtypeStruct((D_FF, D_MODEL), BF16)],
        scratch_shapes=[pltpu.VMEM((n, D_MODEL), BF16), pltpu.VMEM((n, D_MODEL), F32)],
        compiler_params=_params(("arbitrary",)),
        name="ffn_sample",
    )(x, pre, post, wg, wu, wd)


def _lower_bound(logits, layer):
    e = jnp.exp(logits - jnp.max(logits, axis=0, keepdims=True))
    sm = e / jnp.sum(e, axis=0, keepdims=True)
    lb = jnp.zeros((1, B_WIDTH), F32)
    for i in range(1, layer + 1):
        lb = lb + sm[i:i + 1, :]
    return lb


def _rope_slab(t, cos, sin_lo, sin_hi):
    return t * cos + pltpu.roll(t, LANES - ROPE_DIM // 2, axis=1) * sin_lo + pltpu.roll(t, ROPE_DIM // 2, axis=1) * sin_hi


def _col_block(col):
    return pl.BlockSpec((D_MODEL, W_BLOCK), lambda *_: (0, col), pipeline_mode=pl.Buffered(1))


def _hgrn_gates(z0, z1, lb):
    split = W_BLOCK - B_WIDTH
    q = z0[:, :B_WIDTH] * B_SCALE
    f_raw = jnp.concatenate([z0[:, B_WIDTH:], z1[:, :split]], axis=1)
    i_raw = z1[:, split:]
    f = lb + (1.0 - lb) * jax.nn.sigmoid(f_raw)
    return q, 1.0 - f, i_raw * jax.nn.sigmoid(i_raw), jnp.log(f)


def _inproj_attn_kernel(tm, x_ref, pre_ref, cos_ref, slo_ref, shi_ref, wa_ref,
                        q1_ref, q4_ref, q16_ref, k1_ref, k4_ref, k16_ref, v1_ref, v4_ref, v16_ref,
                        kf_ref, vf_ref, stage_ref, stage4_ref):
    h = _rms(x_ref[...], pre_ref[...]).astype(BF16)
    za = _dot(h, wa_ref[...])
    cos, slo, shi = cos_ref[...], slo_ref[...], shi_ref[...]
    outs = ((q1_ref, q4_ref, q16_ref), (k1_ref, k4_ref, k16_ref), (v1_ref, v4_ref, v16_ref))
    for s in range(3 * N_PAIRS):
        kind, hp = divmod(s, N_PAIRS)
        t = za[:, s * LANES:(s + 1) * LANES]
        if kind < 2:
            t = _rope_slab(t, cos, slo, shi)
        if kind == 0:
            t = t * (A_SCALE * LOG2_E)
        if kind == 1:
            kf_ref[:, hp * LANES:(hp + 1) * LANES] = t
        if kind == 2:
            vf_ref[:, hp * LANES:(hp + 1) * LANES] = t
        stage_ref[s] = t
        o1, o4, o16 = outs[kind]
        o1[hp] = t.astype(BF16)
        for r in range(4):
            c4 = stage_ref[s, pl.ds(r, tm // 4, stride=4), :]
            o4[hp, r] = c4.astype(BF16)
            stage4_ref[s, r] = c4
        for r in range(4):
            for a in range(4):
                o16[hp, r + 4 * a] = stage4_ref[s, r, pl.ds(a, tm // 16, stride=4), :].astype(BF16)


def _inproj_attn(x, pre, tabs, w_in, layer, tm):
    s_len = x.shape[0]
    w_len = min(A_MAX_WINDOW, s_len)
    first = (s_len - w_len) // tm
    row = pl.BlockSpec((tm, D_MODEL), lambda i: (i, 0))
    tab = pl.BlockSpec((tm, LANES), lambda i: (i, 0))
    l1 = pl.BlockSpec((N_PAIRS, tm, LANES), lambda i: (0, i, 0))
    l4 = pl.BlockSpec((N_PAIRS, 4, tm // 4, LANES), lambda i: (0, 0, i, 0))
    l16 = pl.BlockSpec((N_PAIRS, 16, tm // 16, LANES), lambda i: (0, 0, i, 0))
    win = pl.BlockSpec((tm, A_WIDTH), lambda i: (jnp.maximum(i - first, 0), 0))
    s1 = jax.ShapeDtypeStruct((N_PAIRS, s_len, LANES), BF16)
    s4 = jax.ShapeDtypeStruct((N_PAIRS, 4, s_len // 4, LANES), BF16)
    s16 = jax.ShapeDtypeStruct((N_PAIRS, 16, s_len // 16, LANES), BF16)
    sw = jax.ShapeDtypeStruct((w_len, A_WIDTH), F32)
    return pl.pallas_call(
        functools.partial(_inproj_attn_kernel, tm),
        grid=(s_len // tm,),
        in_specs=[row, _vec(layer), tab, tab, tab, _col_block(0)],
        out_specs=[l1, l4, l16, l1, l4, l16, l1, l4, l16, win, win],
        out_shape=[s1, s4, s16, s1, s4, s16, s1, s4, s16, sw, sw],
        scratch_shapes=[pltpu.VMEM((3 * N_PAIRS, tm, LANES), F32), pltpu.VMEM((3 * N_PAIRS, 4, tm // 4, LANES), F32)],
        compiler_params=_params(("arbitrary",)),
        name="inproj_attn",
    )(x, pre, *tabs, w_in)


def _inproj_rec_kernel(layer, x_ref, pre_ref, lbl_ref, wb0_ref, wb1_ref, hq_ref, hk_ref, hv_ref, hg_ref):
    h = _rms(x_ref[...], pre_ref[...]).astype(BF16)
    q, k, v, g = _hgrn_gates(_dot(h, wb0_ref[...]), _dot(h, wb1_ref[...]), _lower_bound(lbl_ref[...], layer))
    for hd in range(B_HEADS):
        sl = slice(hd * B_HEAD_DIM, (hd + 1) * B_HEAD_DIM)
        hq_ref[hd] = q[:, sl]
        hk_ref[hd] = k[:, sl]
        hv_ref[hd] = v[:, sl]
        hg_ref[hd] = g[:, sl]


def _inproj_rec(x, pre, lbl, w_in, layer, tm):
    s_len = x.shape[0]
    row = pl.BlockSpec((tm, D_MODEL), lambda i: (i, 0))
    hd = pl.BlockSpec((B_HEADS, tm, B_HEAD_DIM), lambda i: (0, i, 0))
    sh = jax.ShapeDtypeStruct((B_HEADS, s_len, B_HEAD_DIM), F32)
    return pl.pallas_call(
        functools.partial(_inproj_rec_kernel, layer),
        grid=(s_len // tm,),
        in_specs=[row, _vec(layer), _resident((DEPTH, B_WIDTH)), _col_block(1), _col_block(2)],
        out_specs=[hd] * 4,
        out_shape=[sh] * 4,
        compiler_params=_params(("parallel",)),
        name="inproj_rec",
    )(x, pre, lbl, w_in, w_in)


def _inproj_sample_kernel(layer, x_ref, pre_ref, cos_ref, slo_ref, shi_ref, lbl_ref, w_ref,
                          qa_ref, ka_ref, va_ref, hq_ref, hk_ref, hv_ref, hg_ref, w16_ref, h_scr, z_scr):
    c = pl.program_id(0)

    @pl.when(c == 0)
    def _():
        h_scr[...] = _rms(x_ref[...], pre_ref[...]).astype(BF16)

    w = w_ref[...].astype(BF16)
    w16_ref[...] = w

    @pl.when(c == 0)
    def _():
        za = _dot(h_scr[...], w)
        cos, slo, shi = cos_ref[...], slo_ref[...], shi_ref[...]
        outs = (qa_ref, ka_ref, va_ref)
        for s in range(3 * N_PAIRS):
            kind, hp = divmod(s, N_PAIRS)
            t = za[:, s * LANES:(s + 1) * LANES]
            if kind < 2:
                t = _rope_slab(t, cos, slo, shi)
            if kind == 0:
                t = t * A_SCALE
            outs[kind][:, hp * LANES:(hp + 1) * LANES] = t

    @pl.when(c == 1)
    def _():
        z_scr[...] = _dot(h_scr[...], w)

    @pl.when(c == 2)
    def _():
        q, k, v, g = _hgrn_gates(z_scr[...], _dot(h_scr[...], w), _lower_bound(lbl_ref[...], layer))
        hq_ref[...] = q
        hk_ref[...] = k
        hv_ref[...] = v
        hg_ref[...] = g


def _inproj_sample(x, pre, tabs, lbl, w_in, layer):
    n = x.shape[0]
    n_col = w_in.shape[-1] // W_BLOCK
    full = lambda *shape: pl.BlockSpec(shape, lambda c: (0,) * len(shape))
    sa = jax.ShapeDtypeStruct((n, A_WIDTH), F32)
    sb = jax.ShapeDtypeStruct((n, B_WIDTH), F32)
    return pl.pallas_call(
        functools.partial(_inproj_sample_kernel, layer),
        grid=(n_col,),
        in_specs=[full(n, D_MODEL), _vec(layer), full(n, LANES), full(n, LANES), full(n, LANES),
                  full(DEPTH, B_WIDTH), pl.BlockSpec((None, D_MODEL, W_BLOCK), lambda c: (layer, 0, c))],
        out_specs=[full(n, A_WIDTH)] * 3 + [full(n, B_WIDTH)] * 4
                  + [pl.BlockSpec((D_MODEL, W_BLOCK), lambda c: (0, c))],
        out_shape=[sa] * 3 + [sb] * 4 + [jax.ShapeDtypeStruct((D_MODEL, n_col * W_BLOCK), BF16)],
        scratch_shapes=[pltpu.VMEM((n, D_MODEL), BF16), pltpu.VMEM((n, W_BLOCK), F32)],
        compiler_params=_params(("arbitrary",)),
        name="inproj_sample",
    )(x, pre, *tabs, lbl, w_in)


def _attn_scores(q, k, bias):
    first = lax.broadcasted_iota(jnp.int32, (A_SPAN, LANES), 1) < A_HEAD_DIM
    zero = jnp.zeros_like(q)
    q2 = jnp.concatenate([jnp.where(first, q, zero), jnp.where(first, zero, q)], axis=0)
    s = _dot_nt(q2, k) + bias
    m = jnp.max(jnp.maximum(s[:, :A_SPAN], s[:, A_SPAN:]), axis=-1, keepdims=True)
    return jnp.exp2(s - m).astype(BF16), jnp.broadcast_to(m, (2 * A_SPAN, LANES))


def _attn_values(p, m, v_ext):
    first = lax.broadcasted_iota(jnp.int32, (A_SPAN, LANES), 1) < A_HEAD_DIM
    oe = _dot(p, v_ext)
    pick = lambda t: jnp.where(first, t[:A_SPAN], t[A_SPAN:])
    return pick(oe[:, :LANES]), pick(m), pick(oe[:, LANES:])


def _attn_prompt_kernel(q1_ref, q4_ref, q16_ref, k1_ref, k4_ref, k16_ref, v1_ref, v4_ref, v16_ref,
                        o_ref, ks1, ks4, ks16, vs1, vs4, vs16, bias_scr, o_scr, m_scr, d_scr, p_scr, mx_scr):
    j = pl.program_id(1)
    q_refs = (q1_ref, q4_ref, q16_ref)
    k_in = (k1_ref, k4_ref, k16_ref)
    v_in = (v1_ref, v4_ref, v16_ref)
    k_scr = (ks1, ks4, ks16)
    v_scr = (vs1, vs4, vs16)

    @pl.when(j == 0)
    def _():
        for scr in k_scr:
            scr[:, 0:A_SPAN, :] = jnp.zeros((scr.shape[0], A_SPAN, LANES), BF16)
        for scr in v_scr:
            scr[:, 0:A_SPAN, 0:LANES] = jnp.zeros((scr.shape[0], A_SPAN, LANES), BF16)
            scr[:, :, LANES:] = jnp.ones((scr.shape[0], scr.shape[1], LANES), BF16)
        qi = lax.broadcasted_iota(jnp.int32, (2 * A_SPAN, 2 * A_SPAN), 0) % A_SPAN
        ki = lax.broadcasted_iota(jnp.int32, (2 * A_SPAN, 2 * A_SPAN), 1)
        band = jnp.logical_and(ki >= qi, ki <= qi + A_SPAN)
        bias_scr[1] = jnp.where(band, 0.0, -jnp.inf)
        bias_scr[0] = jnp.where(jnp.logical_and(band, ki >= A_SPAN), 0.0, -jnp.inf)

    for p, d in enumerate(A_DILATIONS):
        k_scr[p][:, A_SPAN:, :] = k_in[p][...]
        v_scr[p][:, A_SPAN:, 0:LANES] = v_in[p][...]

    def locate(p, grp, u):
        per_class = ATTN_TILE // A_DILATIONS[p] // A_SPAN
        b = grp * A_UNROLL + u
        n = b % per_class
        return b // per_class, pl.multiple_of(n * A_SPAN, A_SPAN), n

    def front(p, grp):
        for u in range(A_UNROLL):
            r, lo, n = locate(p, grp, u)
            has_prev = jnp.logical_or(j > 0, n > 0).astype(jnp.int32)
            p_scr[u], mx_scr[u] = _attn_scores(q_refs[p][r, pl.ds(lo, A_SPAN), :],
                                               k_scr[p][r, pl.ds(lo, 2 * A_SPAN), :], bias_scr[has_prev])

    def back(p, grp):
        d = A_DILATIONS[p]
        for u in range(A_UNROLL):
            r, lo, n = locate(p, grp, u)
            acc, m, den = _attn_values(p_scr[u], mx_scr[u], v_scr[p][r, pl.ds(lo, 2 * A_SPAN), :])
            if d > 1:
                rows = pl.ds(n * (A_SPAN * d) + r, A_SPAN, stride=d)
                o_scr[p - 1, rows, :] = acc
                m_scr[p - 1, rows, :] = m
                d_scr[p - 1, rows, :] = den
            else:
                rows = pl.ds(lo, A_SPAN)
                m1, m2 = m_scr[0, rows, :], m_scr[1, rows, :]
                top = jnp.maximum(jnp.maximum(m, m1), m2)
                w0, w1, w2 = jnp.exp2(m - top), jnp.exp2(m1 - top), jnp.exp2(m2 - top)
                num = w0 * acc + w1 * o_scr[0, rows, :] + w2 * o_scr[1, rows, :]
                tot = w0 * den + w1 * d_scr[0, rows, :] + w2 * d_scr[1, rows, :]
                o_ref[rows, :] = (num / tot).astype(BF16)

    n_grp = ATTN_TILE // A_SPAN // A_UNROLL
    order = tuple(range(1, len(A_DILATIONS))) + (0,)
    n_dyn = n_grp + jnp.minimum(j, 0)
    front(order[0], 0)
    for idx, p in enumerate(order):

        def steady(grp, carry, p=p):
            back(p, grp - 1)
            front(p, grp)
            return carry

        lax.fori_loop(1, n_dyn, steady, 0)
        back(p, n_grp - 1)
        if idx + 1 < len(order):
            front(order[idx + 1], 0)

    for p, d in enumerate(A_DILATIONS):
        tail = ATTN_TILE // d
        k_scr[p][:, 0:A_SPAN, :] = k_scr[p][:, tail:tail + A_SPAN, :]
        v_scr[p][:, 0:A_SPAN, 0:LANES] = v_scr[p][:, tail:tail + A_SPAN, 0:LANES]


def _attn_prompt(q1, q4, q16, k1, k4, k16, v1, v4, v16):
    s_len = q1.shape[1]
    t = ATTN_TILE
    b1 = pl.BlockSpec((None, 1, t, LANES), lambda hp, j: (hp, 0, j, 0))
    b4 = pl.BlockSpec((None, 4, t // 4, LANES), lambda hp, j: (hp, 0, j, 0))
    b16 = pl.BlockSpec((None, 16, t // 16, LANES), lambda hp, j: (hp, 0, j, 0))
    as4 = lambda a: a.reshape(N_PAIRS, 1, s_len, LANES)
    scr = lambda d, width: pltpu.VMEM((d, A_SPAN + t // d, width), BF16)
    n_dil = len(A_DILATIONS) - 1
    return pl.pallas_call(
        _attn_prompt_kernel,
        grid=(N_PAIRS, s_len // t),
        in_specs=[b1, b4, b16] * 3,
        out_specs=pl.BlockSpec((t, LANES), lambda hp, j: (j, hp)),
        out_shape=jax.ShapeDtypeStruct((s_len, A_WIDTH), BF16),
        scratch_shapes=[scr(d, LANES) for d in A_DILATIONS] + [scr(d, 2 * LANES) for d in A_DILATIONS]
                       + [pltpu.VMEM((2, 2 * A_SPAN, 2 * A_SPAN), F32)] + [pltpu.VMEM((n_dil, t, LANES), F32)] * 3
                       + [pltpu.VMEM((A_UNROLL, 2 * A_SPAN, 2 * A_SPAN), BF16),
                          pltpu.VMEM((A_UNROLL, 2 * A_SPAN, LANES), F32)],
        compiler_params=_params(("arbitrary", "arbitrary")),
        name="attn_prompt",
    )(as4(q1), q4, q16, as4(k1), k4, k16, as4(v1), v4, v16)


def _decode_rows(bb, w_c, q_ref, k_ref, v_ref, kt_ref, vt_ref,
                 hq_ref, hk_ref, hv_ref, hg_ref, st_ref, nrm_ref, a_ref, bn_ref, so_ref):
    n_pat = len(A_DILATIONS)
    dist = w_c - lax.broadcasted_iota(jnp.int32, (A_HEADS, w_c), 1)
    cnt = jnp.zeros((A_HEADS, w_c), F32)
    for d in A_DILATIONS:
        cnt = cnt + jnp.logical_and(dist % d == 0, dist <= A_SPAN * d).astype(F32)
    used = cnt > 0.0
    head_s = lax.broadcasted_iota(jnp.int32, (A_HEADS, w_c), 0)
    head_o = lax.broadcasted_iota(jnp.int32, (A_HEADS, A_HEAD_DIM), 0)
    for b in range(bb):
        q = q_ref[b]
        k_new, v_new = k_ref[b], v_ref[b]
        q16 = q.astype(BF16)
        s_new = jnp.sum(q * k_new, axis=-1, keepdims=True)
        s = jnp.zeros((A_HEADS, w_c), F32)
        for h in range(A_HEADS):
            s = jnp.where(head_s == h, _dot(q16, kt_ref[b, h].astype(BF16)), s)
        m = jnp.maximum(jnp.max(jnp.where(used, s, -jnp.inf), axis=-1, keepdims=True), s_new)
        w = jnp.where(used, jnp.exp(s - m), 0.0) * cnt
        p_new = n_pat * jnp.exp(s_new - m)
        den = jnp.sum(w, axis=-1, keepdims=True) + p_new
        acc = p_new * v_new
        w16 = w.astype(BF16)
        for h in range(A_HEADS):
            acc = acc + jnp.where(head_o == h, _dot_nt(w16, vt_ref[b, h].astype(BF16)), 0.0)
        a_ref[b] = acc / den
        v_rows = hv_ref[b]
        packed = jnp.concatenate([hq_ref[b], hk_ref[b], jnp.exp(hg_ref[b]),
                                  jnp.zeros((LANES - 3 * B_HEADS, B_HEAD_DIM), F32)], axis=0)
        cols = packed.T
        outs = []
        for hd in range(B_HEADS):
            col = lambda i: cols[:, i * B_HEADS + hd:i * B_HEADS + hd + 1]
            st = col(2) * st_ref[b, hd] + col(1) * v_rows[hd:hd + 1, :]
            so_ref[b, hd] = st
            o = jnp.sum(col(0) * st, axis=0, keepdims=True)
            outs.append(_rms(o, nrm_ref[hd:hd + 1, :]))
        bn_ref[b] = jnp.concatenate(outs, axis=0)


N_DECODE_IN = 11


def _recurrent_kernel(th, bb, w_c, q_ref, k_ref, v_ref, g_ref, nrm_ref, *rest):
    dec_in = rest[:N_DECODE_IN]
    o_ref, st_ref, da_ref, dbn_ref, dso_ref, state_scr, b_scr, a_scr, part_scr = rest[-9:]
    j = pl.program_id(1)

    @pl.when(j == 0)
    def _():
        state_scr[...] = jnp.zeros_like(state_scr)

    piece = B_CHUNK * B_UNROLL
    rin = lax.broadcasted_iota(jnp.int32, (piece, B_HEAD_DIM), 0) & (B_CHUNK - 1)

    def scan_piece(i, carry):
        rows = pl.ds(pl.multiple_of(i * piece, piece), piece)
        b = g_ref[rows, :] * LOG2_E
        shift = 1
        while shift < B_CHUNK:
            b = b + jnp.where(rin >= shift, pltpu.roll(b, shift, axis=0), 0.0)
            shift *= 2
        b_scr[rows, :] = b
        return carry

    lax.fori_loop(0, th // piece, scan_piece, 0)

    row = lax.broadcasted_iota(jnp.int32, (B_CHUNK, B_HEAD_DIM), 0)
    a_t = lax.broadcasted_iota(jnp.int32, (B_CHUNK, B_CHUNK), 0)
    a_s = lax.broadcasted_iota(jnp.int32, (B_CHUNK, B_CHUNK), 1)
    levels = []
    half = B_CHUNK // 2
    while half >= 1:
        seg = 2 * half
        pair = jnp.logical_and(a_t // seg == a_s // seg,
                               jnp.logical_and(a_t % seg >= half, a_s % seg < half))
        upper = row % seg >= half
        levels.append((half, upper, jnp.where(upper, 1.0, -1.0), pair))
        half //= 2
    n_grp = B_CHUNK // 8
    sub8 = lax.broadcasted_iota(jnp.int32, (n_grp, 8, B_HEAD_DIM), 1)
    nrm = nrm_ref[...]

    def centre_value(bc, half):
        seg = 2 * half
        if seg >= 8:
            return jnp.concatenate(
                [jnp.broadcast_to(bc[s0 + half - 1:s0 + half, :], (seg, B_HEAD_DIM))
                 for s0 in range(0, B_CHUNK, seg)], axis=0)
        b3 = bc.reshape(n_grp, 8, B_HEAD_DIM)
        if half == 1:
            out = jnp.where(sub8 % 2 == 0, b3, pltpu.roll(b3, 1, axis=1))
        else:
            out = jnp.broadcast_to(b3[:, half - 1:half, :], b3.shape)
            for s0 in range(seg, 8, seg):
                out = jnp.where(sub8 >= s0, jnp.broadcast_to(b3[:, s0 + half - 1:s0 + half, :], b3.shape), out)
        return out.reshape(B_CHUNK, B_HEAD_DIM)

    def intra_chunk(q, k, v, bc):
        a = jnp.zeros((B_CHUNK, B_CHUNK), F32)
        for half, upper, sign, pair in levels:
            x = (jnp.where(upper, q, k) * jnp.exp2((bc - centre_value(bc, half)) * sign)).astype(BF16)
            a = jnp.where(pair, _dot_nt(x, x), a)
        return a, jnp.sum(q * k, axis=-1, keepdims=True) * v

    def chunk_rows(grp, u):
        return pl.ds(pl.multiple_of((grp * B_UNROLL + u) * B_CHUNK, B_CHUNK), B_CHUNK)

    def front(grp):
        st = state_scr[...]
        for u in range(B_UNROLL):
            rows = chunk_rows(grp, u)
            q, k, v, bc = q_ref[rows, :], k_ref[rows, :], v_ref[rows, :], b_scr[rows, :]
            o = _dot_nt((q * jnp.exp2(bc)).astype(BF16), st.astype(BF16))
            b_last = bc[B_CHUNK - 1:B_CHUNK, :]
            k_dec = (k * jnp.exp2(b_last - bc)).astype(BF16)
            st = st * jnp.exp2(b_last) + _dot_tn(v.astype(BF16), k_dec)
            a, o3 = intra_chunk(q, k, v, bc)
            a_scr[u] = a.astype(BF16)
            part_scr[u] = o + o3
        state_scr[...] = st

    def back(grp):
        for u in range(B_UNROLL):
            rows = chunk_rows(grp, u)
            o = part_scr[u] + _dot(a_scr[u], v_ref[rows, :].astype(BF16))
            o_ref[rows, :] = _rms(o, nrm)

    n_groups = th // (B_CHUNK * B_UNROLL)
    _decode_rows(bb, w_c, *dec_in, da_ref, dbn_ref, dso_ref)
    front(0)

    def steady(grp, carry):
        back(grp - 1)
        front(grp)
        return carry

    lax.fori_loop(1, n_groups, steady, 0)
    back(n_groups - 1)

    @pl.when(j == pl.num_programs(1) - 1)
    def _():
        st_ref[...] = state_scr[...].T


def _recurrent(layer, q, k, v, g, nrm, dq, dk, dv, cache_kt, cache_vt, dhq, dhk, dhv, dhg, state, states_out, th):
    s_len = q.shape[1]
    n = dq.shape[0]
    w_c = cache_kt.shape[-1]
    n_tiles = s_len // th
    steps = B_HEADS * n_tiles
    assert w_c >= A_SPAN * max(A_DILATIONS) and n % steps == 0
    bb = n // steps
    step = lambda h, j: h * n_tiles + j
    blk = pl.BlockSpec((None, th, B_HEAD_DIM), lambda h, j: (h, j, 0))
    a_spec = pl.BlockSpec((bb, A_HEADS, A_HEAD_DIM), lambda h, j: (step(h, j), 0, 0))
    headspec = pl.BlockSpec((bb, B_HEADS, B_HEAD_DIM), lambda h, j: (step(h, j), 0, 0))
    c_spec = pl.BlockSpec((None, bb, A_HEADS, A_HEAD_DIM, w_c), lambda h, j: (layer, step(h, j), 0, 0, 0))
    st_spec = pl.BlockSpec((None, bb, B_HEADS, B_HEAD_DIM, B_HEAD_DIM), lambda h, j: (layer, step(h, j), 0, 0, 0))
    a_heads = lambda a: a.reshape(n, A_HEADS, A_HEAD_DIM)
    heads = lambda a: a.reshape(n, B_HEADS, B_HEAD_DIM)
    in_specs = [blk] * 4 + [pl.BlockSpec((None, 1, B_HEAD_DIM), lambda h, j: (layer, 0, h))]
    in_specs += [a_spec] * 3 + [c_spec] * 2 + [headspec] * 4 + [st_spec, _layer_block((B_HEADS, B_HEAD_DIM), layer)]
    args = [q, k, v, g, nrm, a_heads(dq), a_heads(dk), a_heads(dv), cache_kt, cache_vt,
            heads(dhq), heads(dhk), heads(dhv), heads(dhg), state, nrm.reshape(DEPTH, B_HEADS, B_HEAD_DIM)]
    assert len(args) == 5 + N_DECODE_IN
    aliases = {}
    if states_out is not None:
        aliases = {len(args): 4}
        in_specs.append(pl.BlockSpec(memory_space=pl.ANY))
        args.append(states_out)
    bn, st_p, a_out, dbn, states = pl.pallas_call(
        functools.partial(_recurrent_kernel, th, bb, w_c),
        grid=(B_HEADS, n_tiles),
        in_specs=in_specs,
        out_specs=[pl.BlockSpec((th, B_HEAD_DIM), lambda h, j: (j, h)),
                   pl.BlockSpec((None, B_HEAD_DIM, B_HEAD_DIM), lambda h, j: (h, 0, 0)),
                   a_spec, headspec, st_spec],
        out_shape=[jax.ShapeDtypeStruct((s_len, B_WIDTH), F32),
                   jax.ShapeDtypeStruct((B_HEADS, B_HEAD_DIM, B_HEAD_DIM), F32),
                   jax.ShapeDtypeStruct((n, A_HEADS, A_HEAD_DIM), F32),
                   jax.ShapeDtypeStruct((n, B_HEADS, B_HEAD_DIM), F32),
                   jax.ShapeDtypeStruct(state.shape, F32)],
        scratch_shapes=[pltpu.VMEM((B_HEAD_DIM, B_HEAD_DIM), F32), pltpu.VMEM((th, B_HEAD_DIM), F32),
                        pltpu.VMEM((B_UNROLL, B_CHUNK, B_CHUNK), BF16),
                        pltpu.VMEM((B_UNROLL, B_CHUNK, B_HEAD_DIM), F32)],
        input_output_aliases=aliases,
        compiler_params=_params(("arbitrary", "arbitrary")),
        name="recurrent",
    )(*args)
    return bn, st_p, a_out.reshape(n, A_WIDTH).astype(BF16), dbn.reshape(n, B_WIDTH), states


def _mix(x, a16, bn, pre, post, wg0, wg1, wa, wb, wo):
    h = _rms(x, pre).astype(BF16)
    z0 = _dot(h, wg0)
    z1 = _dot(h, wg1)
    split = W_BLOCK - B_WIDTH
    g_b = z0[:, :B_WIDTH]
    gate_a = jnp.concatenate([z0[:, B_WIDTH:], z1[:, :split]], axis=1)
    gate_b = z1[:, split:]
    b_out = (bn * (g_b * jax.nn.sigmoid(g_b))).astype(BF16)
    mix = (jax.nn.sigmoid(gate_a) * _dot(a16, wa) + jax.nn.sigmoid(gate_b) * _dot(b_out, wb)).astype(BF16)
    return x + _rms(_dot(mix, wo), post)


def _mixout_kernel(x_ref, a_ref, bn_ref, pre_ref, post_ref, wg0_ref, wg1_ref, wa_ref, wb_ref, wo_ref, o_ref):
    o_ref[...] = _mix(x_ref[...], a_ref[...], bn_ref[...], pre_ref[...], post_ref[...],
                      wg0_ref[...], wg1_ref[...], wa_ref[...], wb_ref[...], wo_ref[...])


def _mixout(x, a_out, bn, layer, pre, post, w_in, wa, wb, wo, tm):
    n = x.shape[0]
    row = lambda width: pl.BlockSpec((tm, width), lambda i: (i, 0))
    return pl.pallas_call(
        _mixout_kernel,
        grid=(n // tm,),
        in_specs=[row(D_MODEL), row(A_WIDTH), row(B_WIDTH), _vec(layer), _vec(layer), _col_block(3), _col_block(4),
                  _resident((A_WIDTH, D_MODEL)), _resident((B_WIDTH, D_MODEL)), _resident((D_MODEL, D_MODEL))],
        out_specs=row(D_MODEL),
        out_shape=jax.ShapeDtypeStruct((n, D_MODEL), F32),
        compiler_params=_params(("parallel",)),
        name="mixout",
    )(x, a_out, bn, pre, post, w_in, w_in, wa, wb, wo)


def _mixout_sample_kernel(x_ref, a_ref, bn_ref, pre_ref, post_ref, wg0_ref, wg1_ref, wa_ref, wb_ref, wo_ref,
                          o_ref, wa16_ref, wb16_ref, wo16_ref):
    wa, wb, wo = wa_ref[...].astype(BF16), wb_ref[...].astype(BF16), wo_ref[...].astype(BF16)
    wa16_ref[...] = wa
    wb16_ref[...] = wb
    wo16_ref[...] = wo
    o_ref[...] = _mix(x_ref[...], a_ref[...], bn_ref[...], pre_ref[...], post_ref[...],
                      wg0_ref[...], wg1_ref[...], wa, wb, wo)


def _mixout_sample(x, a_out, bn, layer, pre, post, w_in16, wa, wb, wo):
    n = x.shape[0]
    full = lambda *shape: pl.BlockSpec(shape, lambda i: (0,) * len(shape))
    shapes = ((A_WIDTH, D_MODEL), (B_WIDTH, D_MODEL), (D_MODEL, D_MODEL))
    return pl.pallas_call(
        _mixout_sample_kernel,
        grid=(1,),
        in_specs=[full(n, D_MODEL), full(n, A_WIDTH), full(n, B_WIDTH), _vec(layer), _vec(layer),
                  _col_block(3), _col_block(4)] + [_layer_block(s, layer) for s in shapes],
        out_specs=[full(n, D_MODEL)] + [full(*s) for s in shapes],
        out_shape=[jax.ShapeDtypeStruct((n, D_MODEL), F32)] + [jax.ShapeDtypeStruct(s, BF16) for s in shapes],
        compiler_params=_params(("arbitrary",)),
        name="mixout_sample",
    )(x, a_out, bn, pre, post, w_in16, w_in16, wa, wb, wo)


def _rope_tables(pos):
    half = ROPE_DIM // 2
    inv = ROPE_THETA ** (-jnp.arange(half, dtype=F32) / half)
    dim = jnp.arange(LANES) % A_HEAD_DIM
    ang = pos.astype(F32)[:, None] * inv[dim % half][None, :]
    cos, sin = jnp.cos(ang), jnp.sin(ang)
    lo, hi = (dim < half)[None, :], jnp.logical_and(dim >= half, dim < ROPE_DIM)[None, :]
    return (jnp.where(jnp.logical_or(lo, hi), cos, 1.0), jnp.where(lo, -sin, 0.0), jnp.where(hi, sin, 0.0))


def kernel(x_prompt, x_sample, cache_k, cache_v, state_hgrn, ffn1_norm_pre, ffn1_norm_post, ffn1_w_gate, ffn1_w_up, ffn1_w_down, mix_norm_pre, mix_norm_post, w_in, hgrn_lb_logits, hgrn_out_norm, w_a_out, w_b_out, w_mix_out, ffn2_norm_pre, ffn2_norm_post, ffn2_w_gate, ffn2_w_up, ffn2_w_down):
    batch, s_len, _ = x_prompt.shape
    n_dec, t_dec, _ = x_sample.shape
    assert batch == 1 and t_dec == 1 and s_len % ATTN_TILE == 0
    tm = 512
    th = min(4096, s_len)
    cache_kt = jnp.transpose(cache_k, (0, 1, 3, 4, 2))
    cache_vt = jnp.transpose(cache_v, (0, 1, 3, 4, 2))
    yp = x_prompt.reshape(s_len, D_MODEL)
    ys = x_sample.reshape(n_dec, D_MODEL)
    tabs_p = _rope_tables(jnp.arange(s_len))
    tabs_s = _rope_tables(jnp.full((n_dec,), PAST_LEN))
    lbl = hgrn_lb_logits.astype(F32)
    vecs = lambda a: a.reshape(DEPTH, 1, -1).astype(F32)
    w_len = min(A_MAX_WINDOW, s_len)
    f1_pre, f1_post, f2_pre, f2_post = (vecs(a) for a in (ffn1_norm_pre, ffn1_norm_post, ffn2_norm_pre, ffn2_norm_post))
    m_pre, m_post, nrm = vecs(mix_norm_pre), vecs(mix_norm_post), vecs(hgrn_out_norm)
    kp, vp, sp, ksn, vsn = [], [], [], [], []
    states = None
    for l in range(DEPTH):
        ys, *f1 = _ffn_sample(ys, l, f1_pre, f1_post, ffn1_w_gate, ffn1_w_up, ffn1_w_down)
        qa, ka, va, sq, sk, sv, sg, win = _inproj_sample(ys, m_pre, tabs_s, lbl, w_in, l)
        yp = _ffn(yp, l, f1_pre, f1_post, *f1, tm)
        q1, q4, q16, k1, k4, k16, v1, v4, v16, kf, vf = _inproj_attn(yp, m_pre, tabs_p, win, l, tm)
        hq, hk, hv, hg = _inproj_rec(yp, m_pre, lbl, win, l, tm)
        a_out = _attn_prompt(q1, q4, q16, k1, k4, k16, v1, v4, v16)
        bn, st_p, a_s, bn_s, states = _recurrent(l, hq, hk, hv, hg, nrm, qa, ka, va, cache_kt, cache_vt,
                                                 sq, sk, sv, sg, state_hgrn, states, th)
        ys, *mo = _mixout_sample(ys, a_s, bn_s, l, m_pre, m_post, win, w_a_out, w_b_out, w_mix_out)
        ys, *f2 = _ffn_sample(ys, l, f2_pre, f2_post, ffn2_w_gate, ffn2_w_up, ffn2_w_down)
        yp = _mixout(yp, a_out, bn, l, m_pre, m_post, win, *mo, tm)
        yp = _ffn(yp, l, f2_pre, f2_post, *f2, tm)
        kp.append(kf.reshape(batch, w_len, A_HEADS, A_HEAD_DIM))
        vp.append(vf.reshape(batch, w_len, A_HEADS, A_HEAD_DIM))
        sp.append(st_p.reshape(batch, B_HEADS, B_HEAD_DIM, B_HEAD_DIM))
        ksn.append(ka.reshape(n_dec, t_dec, A_HEADS, A_HEAD_DIM))
        vsn.append(va.reshape(n_dec, t_dec, A_HEADS, A_HEAD_DIM))
    return (yp.reshape(batch, s_len, D_MODEL), ys.reshape(n_dec, t_dec, D_MODEL),
            jnp.stack(kp), jnp.stack(vp), jnp.stack(sp), jnp.stack(ksn), jnp.stack(vsn), states)
```

```python
import functools

import jax
import jax.numpy as jnp
from jax import lax
from jax.experimental import pallas as pl
from jax.experimental.pallas import tpu as pltpu

F32 = jnp.float32
BF16 = jnp.bfloat16

D_MODEL = 1024
DEPTH = 2
PAST_LEN = 16384
A_HEADS = 8
A_HEAD_DIM = 64
A_WIDTH = A_HEADS * A_HEAD_DIM
A_DILATIONS = (1, 4, 16)
A_SPAN = 128
A_MAX_WINDOW = 2048
A_SCALE = A_HEAD_DIM ** -0.5
ROPE_THETA = 500000.0
ROPE_DIM = A_HEAD_DIM // 4
B_HEADS = 8
B_HEAD_DIM = 128
B_WIDTH = B_HEADS * B_HEAD_DIM
B_CHUNK = 64
B_UNROLL = 8
B_SCALE = B_HEAD_DIM ** -0.5
D_FF = 2816
FF_CHUNK = 256
ROW_PARTS = 4
EPS = 1e-6
LOG2_E = 1.4426950408889634

W_BLOCK = 3 * A_WIDTH
assert 3 * B_WIDTH == 2 * W_BLOCK and B_WIDTH + 2 * D_MODEL == 2 * W_BLOCK
LANES = 128
N_PAIRS = A_WIDTH // LANES
ATTN_TILE = A_SPAN * max(A_DILATIONS)
A_UNROLL = 8
VMEM_LIMIT = 56 * 1024 * 1024


def _params(sem, vmem=VMEM_LIMIT):
    return pltpu.CompilerParams(dimension_semantics=sem, vmem_limit_bytes=vmem)


def _resident(shape):
    nd = len(shape)
    return pl.BlockSpec(shape, lambda *_: (0,) * nd, pipeline_mode=pl.Buffered(1))


def _layer_block(shape, layer, col=0):
    index = (layer,) + (0,) * (len(shape) - 1) + (col,)
    return pl.BlockSpec((None,) + tuple(shape), lambda *_: index, pipeline_mode=pl.Buffered(1))


def _vec(layer):
    return _layer_block((1, D_MODEL), layer)


def _rms(x, g):
    y = x * lax.rsqrt(jnp.mean(x * x, axis=-1, keepdims=True) + EPS)
    return y * g


def _dot(a, b):
    return jnp.dot(a, b, preferred_element_type=F32)


def _dot_nt(a, b):
    return lax.dot_general(a, b, (((1,), (1,)), ((), ())), preferred_element_type=F32)


def _dot_tn(a, b):
    return lax.dot_general(a, b, (((0,), (0,)), ((), ())), preferred_element_type=F32)


def _row_parts(rows):
    parts = ROW_PARTS if rows % (8 * ROW_PARTS) == 0 else 1
    return [pl.ds(p * (rows // parts), rows // parts) for p in range(parts)]


def _ffn_kernel(x_ref, pre_ref, post_ref, wg_ref, wu_ref, wd_ref, o_ref):
    for sl in _row_parts(x_ref.shape[0]):
        x = x_ref[sl, :]
        h = _rms(x, pre_ref[...]).astype(BF16)
        g = _dot(h, wg_ref[...])
        u = _dot(h, wu_ref[...])
        a = (g * jax.nn.sigmoid(g) * u).astype(BF16)
        y = _dot(a, wd_ref[...])
        o_ref[sl, :] = x + 0.5 * _rms(y, post_ref[...])


def _ffn(x, layer, pre, post, wg, wu, wd, tm):
    n = x.shape[0]
    row = pl.BlockSpec((tm, D_MODEL), lambda i: (i, 0))
    return pl.pallas_call(
        _ffn_kernel,
        grid=(n // tm,),
        in_specs=[row, _vec(layer), _vec(layer), _resident((D_MODEL, D_FF)),
                  _resident((D_MODEL, D_FF)), _resident((D_FF, D_MODEL))],
        out_specs=row,
        out_shape=jax.ShapeDtypeStruct((n, D_MODEL), F32),
        compiler_params=_params(("parallel",)),
        name="ffn",
    )(x, pre, post, wg, wu, wd)


def _ffn_sample_kernel(x_ref, pre_ref, post_ref, wg_ref, wu_ref, wd_ref,
                       o_ref, wg16_ref, wu16_ref, wd16_ref, h_scr, acc_scr):
    c = pl.program_id(0)

    @pl.when(c == 0)
    def _():
        h_scr[...] = _rms(x_ref[...], pre_ref[...]).astype(BF16)
        acc_scr[...] = jnp.zeros_like(acc_scr)

    wg, wu, wd = wg_ref[...].astype(BF16), wu_ref[...].astype(BF16), wd_ref[...].astype(BF16)
    wg16_ref[...] = wg
    wu16_ref[...] = wu
    wd16_ref[...] = wd
    h = h_scr[...]
    g = _dot(h, wg)
    u = _dot(h, wu)
    acc_scr[...] += _dot((g * jax.nn.sigmoid(g) * u).astype(BF16), wd)

    @pl.when(c == pl.num_programs(0) - 1)
    def _():
        o_ref[...] = x_ref[...] + 0.5 * _rms(acc_scr[...], post_ref[...])


def _ffn_sample(x, layer, pre, post, wg, wu, wd):
    n = x.shape[0]
    full = pl.BlockSpec((n, D_MODEL), lambda c: (0, 0))
    col = lambda: pl.BlockSpec((None, D_MODEL, FF_CHUNK), lambda c: (layer, 0, c))
    col16 = lambda: pl.BlockSpec((D_MODEL, FF_CHUNK), lambda c: (0, c))
    return pl.pallas_call(
        _ffn_sample_kernel,
        grid=(D_FF // FF_CHUNK,),
        in_specs=[full, _vec(layer), _vec(layer), col(), col(),
                  pl.BlockSpec((None, FF_CHUNK, D_MODEL), lambda c: (layer, c, 0))],
        out_specs=[full, col16(), col16(), pl.BlockSpec((FF_CHUNK, D_MODEL), lambda c: (c, 0))],
        out_shape=[jax.ShapeDtypeStruct((n, D_MODEL), F32), jax.ShapeDtypeStruct((D_MODEL, D_FF), BF16),
                   jax.ShapeDtypeStruct((D_MODEL, D_FF), BF16), jax.ShapeDtypeStruct((D_FF, D_MODEL), BF16)],
        scratch_shapes=[pltpu.VMEM((n, D_MODEL), BF16), pltpu.VMEM((n, D_MODEL), F32)],
        compiler_params=_params(("arbitrary",)),
        name="ffn_sample",
    )(x, pre, post, wg, wu, wd)


def _lower_bound(logits, layer):
    e = jnp.exp(logits - jnp.max(logits, axis=0, keepdims=True))
    sm = e / jnp.sum(e, axis=0, keepdims=True)
    lb = jnp.zeros((1, B_WIDTH), F32)
    for i in range(1, layer + 1):
        lb = lb + sm[i:i + 1, :]
    return lb


def _rope_slab(t, cos, sin_lo, sin_hi):
    return t * cos + pltpu.roll(t, LANES - ROPE_DIM // 2, axis=1) * sin_lo + pltpu.roll(t, ROPE_DIM // 2, axis=1) * sin_hi


def _col_block(col):
    return pl.BlockSpec((D_MODEL, W_BLOCK), lambda *_: (0, col), pipeline_mode=pl.Buffered(1))


def _hgrn_gates(z0, z1, lb):
    split = W_BLOCK - B_WIDTH
    q = z0[:, :B_WIDTH] * B_SCALE
    f_raw = jnp.concatenate([z0[:, B_WIDTH:], z1[:, :split]], axis=1)
    i_raw = z1[:, split:]
    f = lb + (1.0 - lb) * jax.nn.sigmoid(f_raw)
    return q, 1.0 - f, i_raw * jax.nn.sigmoid(i_raw), jnp.log(f)


def _inproj_attn_kernel(tm, x_ref, pre_ref, cos_ref, slo_ref, shi_ref, wa_ref,
                        q1_ref, q4_ref, q16_ref, k1_ref, k4_ref, k16_ref, v1_ref, v4_ref, v16_ref,
                        kf_ref, vf_ref, stage_ref, stage4_ref):
    h = _rms(x_ref[...], pre_ref[...]).astype(BF16)
    za = _dot(h, wa_ref[...])
    cos, slo, shi = cos_ref[...], slo_ref[...], shi_ref[...]
    outs = ((q1_ref, q4_ref, q16_ref), (k1_ref, k4_ref, k16_ref), (v1_ref, v4_ref, v16_ref))
    for s in range(3 * N_PAIRS):
        kind, hp = divmod(s, N_PAIRS)
        t = za[:, s * LANES:(s + 1) * LANES]
        if kind < 2:
            t = _rope_slab(t, cos, slo, shi)
        if kind == 0:
            t = t * (A_SCALE * LOG2_E)
        if kind == 1:
            kf_ref[:, hp * LANES:(hp + 1) * LANES] = t
        if kind == 2:
            vf_ref[:, hp * LANES:(hp + 1) * LANES] = t
        stage_ref[s] = t
        o1, o4, o16 = outs[kind]
        o1[hp] = t.astype(BF16)
        for r in range(4):
            c4 = stage_ref[s, pl.ds(r, tm // 4, stride=4), :]
            o4[hp, r] = c4.astype(BF16)
            stage4_ref[s, r] = c4
        for r in range(4):
            for a in range(4):
                o16[hp, r + 4 * a] = stage4_ref[s, r, pl.ds(a, tm // 16, stride=4), :].astype(BF16)


def _inproj_attn(x, pre, tabs, w_in, layer, tm):
    s_len = x.shape[0]
    w_len = min(A_MAX_WINDOW, s_len)
    first = (s_len - w_len) // tm
    row = pl.BlockSpec((tm, D_MODEL), lambda i: (i, 0))
    tab = pl.BlockSpec((tm, LANES), lambda i: (i, 0))
    l1 = pl.BlockSpec((N_PAIRS, tm, LANES), lambda i: (0, i, 0))
    l4 = pl.BlockSpec((N_PAIRS, 4, tm // 4, LANES), lambda i: (0, 0, i, 0))
    l16 = pl.BlockSpec((N_PAIRS, 16, tm // 16, LANES), lambda i: (0, 0, i, 0))
    win = pl.BlockSpec((tm, A_WIDTH), lambda i: (jnp.maximum(i - first, 0), 0))
    s1 = jax.ShapeDtypeStruct((N_PAIRS, s_len, LANES), BF16)
    s4 = jax.ShapeDtypeStruct((N_PAIRS, 4, s_len // 4, LANES), BF16)
    s16 = jax.ShapeDtypeStruct((N_PAIRS, 16, s_len // 16, LANES), BF16)
    sw = jax.ShapeDtypeStruct((w_len, A_WIDTH), F32)
    return pl.pallas_call(
        functools.partial(_inproj_attn_kernel, tm),
        grid=(s_len // tm,),
        in_specs=[row, _vec(layer), tab, tab, tab, _col_block(0)],
        out_specs=[l1, l4, l16, l1, l4, l16, l1, l4, l16, win, win],
        out_shape=[s1, s4, s16, s1, s4, s16, s1, s4, s16, sw, sw],
        scratch_shapes=[pltpu.VMEM((3 * N_PAIRS, tm, LANES), F32), pltpu.VMEM((3 * N_PAIRS, 4, tm // 4, LANES), F32)],
        compiler_params=_params(("arbitrary",)),
        name="inproj_attn",
    )(x, pre, *tabs, w_in)


def _inproj_rec_kernel(layer, x_ref, pre_ref, lbl_ref, wb0_ref, wb1_ref, hq_ref, hk_ref, hv_ref, hg_ref):
    lb = _lower_bound(lbl_ref[...], layer)
    for rows in _row_parts(x_ref.shape[0]):
        h = _rms(x_ref[rows, :], pre_ref[...]).astype(BF16)
        q, k, v, g = _hgrn_gates(_dot(h, wb0_ref[...]), _dot(h, wb1_ref[...]), lb)
        for hd in range(B_HEADS):
            sl = slice(hd * B_HEAD_DIM, (hd + 1) * B_HEAD_DIM)
            hq_ref[hd, rows, :] = q[:, sl]
            hk_ref[hd, rows, :] = k[:, sl]
            hv_ref[hd, rows, :] = v[:, sl]
            hg_ref[hd, rows, :] = g[:, sl]


def _inproj_rec(x, pre, lbl, w_in, layer, tm):
    s_len = x.shape[0]
    row = pl.BlockSpec((tm, D_MODEL), lambda i: (i, 0))
    hd = pl.BlockSpec((B_HEADS, tm, B_HEAD_DIM), lambda i: (0, i, 0))
    sh = jax.ShapeDtypeStruct((B_HEADS, s_len, B_HEAD_DIM), F32)
    return pl.pallas_call(
        functools.partial(_inproj_rec_kernel, layer),
        grid=(s_len // tm,),
        in_specs=[row, _vec(layer), _resident((DEPTH, B_WIDTH)), _col_block(1), _col_block(2)],
        out_specs=[hd] * 4,
        out_shape=[sh] * 4,
        compiler_params=_params(("parallel",)),
        name="inproj_rec",
    )(x, pre, lbl, w_in, w_in)


def _inproj_sample_kernel(layer, x_ref, pre_ref, cos_ref, slo_ref, shi_ref, lbl_ref, w_ref,
                          qa_ref, ka_ref, va_ref, hq_ref, hk_ref, hv_ref, hg_ref, w16_ref, h_scr, z_scr):
    c = pl.program_id(0)

    @pl.when(c == 0)
    def _():
        h_scr[...] = _rms(x_ref[...], pre_ref[...]).astype(BF16)

    w = w_ref[...].astype(BF16)
    w16_ref[...] = w

    @pl.when(c == 0)
    def _():
        za = _dot(h_scr[...], w)
        cos, slo, shi = cos_ref[...], slo_ref[...], shi_ref[...]
        outs = (qa_ref, ka_ref, va_ref)
        for s in range(3 * N_PAIRS):
            kind, hp = divmod(s, N_PAIRS)
            t = za[:, s * LANES:(s + 1) * LANES]
            if kind < 2:
                t = _rope_slab(t, cos, slo, shi)
            if kind == 0:
                t = t * A_SCALE
            outs[kind][:, hp * LANES:(hp + 1) * LANES] = t

    @pl.when(c == 1)
    def _():
        z_scr[...] = _dot(h_scr[...], w)

    @pl.when(c == 2)
    def _():
        q, k, v, g = _hgrn_gates(z_scr[...], _dot(h_scr[...], w), _lower_bound(lbl_ref[...], layer))
        hq_ref[...] = q
        hk_ref[...] = k
        hv_ref[...] = v
        hg_ref[...] = g


def _inproj_sample(x, pre, tabs, lbl, w_in, layer):
    n = x.shape[0]
    n_col = w_in.shape[-1] // W_BLOCK
    full = lambda *shape: pl.BlockSpec(shape, lambda c: (0,) * len(shape))
    sa = jax.ShapeDtypeStruct((n, A_WIDTH), F32)
    sb = jax.ShapeDtypeStruct((n, B_WIDTH), F32)
    return pl.pallas_call(
        functools.partial(_inproj_sample_kernel, layer),
        grid=(n_col,),
        in_specs=[full(n, D_MODEL), _vec(layer), full(n, LANES), full(n, LANES), full(n, LANES),
                  full(DEPTH, B_WIDTH), pl.BlockSpec((None, D_MODEL, W_BLOCK), lambda c: (layer, 0, c))],
        out_specs=[full(n, A_WIDTH)] * 3 + [full(n, B_WIDTH)] * 4
                  + [pl.BlockSpec((D_MODEL, W_BLOCK), lambda c: (0, c))],
        out_shape=[sa] * 3 + [sb] * 4 + [jax.ShapeDtypeStruct((D_MODEL, n_col * W_BLOCK), BF16)],
        scratch_shapes=[pltpu.VMEM((n, D_MODEL), BF16), pltpu.VMEM((n, W_BLOCK), F32)],
        compiler_params=_params(("arbitrary",)),
        name="inproj_sample",
    )(x, pre, *tabs, lbl, w_in)


def _attn_scores(q, k, bias):
    first = lax.broadcasted_iota(jnp.int32, (A_SPAN, LANES), 1) < A_HEAD_DIM
    zero = jnp.zeros_like(q)
    q2 = jnp.concatenate([jnp.where(first, q, zero), jnp.where(first, zero, q)], axis=0)
    s = _dot_nt(q2, k) + bias
    m = jnp.max(jnp.maximum(s[:, :A_SPAN], s[:, A_SPAN:]), axis=-1, keepdims=True)
    return jnp.exp2(s - m).astype(BF16), jnp.broadcast_to(m, (2 * A_SPAN, LANES))


def _attn_values(p, m, v_ext):
    first = lax.broadcasted_iota(jnp.int32, (A_SPAN, LANES), 1) < A_HEAD_DIM
    oe = _dot(p, v_ext)
    pick = lambda t: jnp.where(first, t[:A_SPAN], t[A_SPAN:])
    return pick(oe[:, :LANES]), pick(m), pick(oe[:, LANES:])


def _attn_prompt_kernel(q1_ref, q4_ref, q16_ref, k1_ref, k4_ref, k16_ref, v1_ref, v4_ref, v16_ref,
                        o_ref, ks1, ks4, ks16, vs1, vs4, vs16, bias_scr, o_scr, m_scr, d_scr, p_scr, mx_scr):
    j = pl.program_id(1)
    q_refs = (q1_ref, q4_ref, q16_ref)
    k_in = (k1_ref, k4_ref, k16_ref)
    v_in = (v1_ref, v4_ref, v16_ref)
    k_scr = (ks1, ks4, ks16)
    v_scr = (vs1, vs4, vs16)

    @pl.when(j == 0)
    def _():
        for scr in k_scr:
            scr[:, 0:A_SPAN, :] = jnp.zeros((scr.shape[0], A_SPAN, LANES), BF16)
        for scr in v_scr:
            scr[:, 0:A_SPAN, 0:LANES] = jnp.zeros((scr.shape[0], A_SPAN, LANES), BF16)
            scr[:, :, LANES:] = jnp.ones((scr.shape[0], scr.shape[1], LANES), BF16)
        qi = lax.broadcasted_iota(jnp.int32, (2 * A_SPAN, 2 * A_SPAN), 0) % A_SPAN
        ki = lax.broadcasted_iota(jnp.int32, (2 * A_SPAN, 2 * A_SPAN), 1)
        band = jnp.logical_and(ki >= qi, ki <= qi + A_SPAN)
        bias_scr[1] = jnp.where(band, 0.0, -jnp.inf)
        bias_scr[0] = jnp.where(jnp.logical_and(band, ki >= A_SPAN), 0.0, -jnp.inf)

    for p, d in enumerate(A_DILATIONS):
        k_scr[p][:, A_SPAN:, :] = k_in[p][...]
        v_scr[p][:, A_SPAN:, 0:LANES] = v_in[p][...]

    def locate(p, grp, u):
        per_class = ATTN_TILE // A_DILATIONS[p] // A_SPAN
        b = grp * A_UNROLL + u
        n = b % per_class
        return b // per_class, pl.multiple_of(n * A_SPAN, A_SPAN), n

    def front(p, grp):
        for u in range(A_UNROLL):
            r, lo, n = locate(p, grp, u)
            has_prev = jnp.logical_or(j > 0, n > 0).astype(jnp.int32)
            p_scr[u], mx_scr[u] = _attn_scores(q_refs[p][r, pl.ds(lo, A_SPAN), :],
                                               k_scr[p][r, pl.ds(lo, 2 * A_SPAN), :], bias_scr[has_prev])

    def back(p, grp):
        d = A_DILATIONS[p]
        for u in range(A_UNROLL):
            r, lo, n = locate(p, grp, u)
            acc, m, den = _attn_values(p_scr[u], mx_scr[u], v_scr[p][r, pl.ds(lo, 2 * A_SPAN), :])
            if d > 1:
                rows = pl.ds(n * (A_SPAN * d) + r, A_SPAN, stride=d)
                o_scr[p - 1, rows, :] = acc
                m_scr[p - 1, rows, :] = m
                d_scr[p - 1, rows, :] = den
            else:
                rows = pl.ds(lo, A_SPAN)
                m1, m2 = m_scr[0, rows, :], m_scr[1, rows, :]
                top = jnp.maximum(jnp.maximum(m, m1), m2)
                w0, w1, w2 = jnp.exp2(m - top), jnp.exp2(m1 - top), jnp.exp2(m2 - top)
                num = w0 * acc + w1 * o_scr[0, rows, :] + w2 * o_scr[1, rows, :]
                tot = w0 * den + w1 * d_scr[0, rows, :] + w2 * d_scr[1, rows, :]
                o_ref[rows, :] = (num / tot).astype(BF16)

    n_grp = ATTN_TILE // A_SPAN // A_UNROLL
    order = tuple(range(1, len(A_DILATIONS))) + (0,)
    n_dyn = n_grp + jnp.minimum(j, 0)
    front(order[0], 0)
    for idx, p in enumerate(order):

        def steady(grp, carry, p=p):
            back(p, grp - 1)
            front(p, grp)
            return carry

        lax.fori_loop(1, n_dyn, steady, 0)
        back(p, n_grp - 1)
        if idx + 1 < len(order):
            front(order[idx + 1], 0)

    for p, d in enumerate(A_DILATIONS):
        tail = ATTN_TILE // d
        k_scr[p][:, 0:A_SPAN, :] = k_scr[p][:, tail:tail + A_SPAN, :]
        v_scr[p][:, 0:A_SPAN, 0:LANES] = v_scr[p][:, tail:tail + A_SPAN, 0:LANES]


def _attn_prompt(q1, q4, q16, k1, k4, k16, v1, v4, v16):
    s_len = q1.shape[1]
    t = ATTN_TILE
    b1 = pl.BlockSpec((None, 1, t, LANES), lambda hp, j: (hp, 0, j, 0))
    b4 = pl.BlockSpec((None, 4, t // 4, LANES), lambda hp, j: (hp, 0, j, 0))
    b16 = pl.BlockSpec((None, 16, t // 16, LANES), lambda hp, j: (hp, 0, j, 0))
    as4 = lambda a: a.reshape(N_PAIRS, 1, s_len, LANES)
    scr = lambda d, width: pltpu.VMEM((d, A_SPAN + t // d, width), BF16)
    n_dil = len(A_DILATIONS) - 1
    return pl.pallas_call(
        _attn_prompt_kernel,
        grid=(N_PAIRS, s_len // t),
        in_specs=[b1, b4, b16] * 3,
        out_specs=pl.BlockSpec((t, LANES), lambda hp, j: (j, hp)),
        out_shape=jax.ShapeDtypeStruct((s_len, A_WIDTH), BF16),
        scratch_shapes=[scr(d, LANES) for d in A_DILATIONS] + [scr(d, 2 * LANES) for d in A_DILATIONS]
                       + [pltpu.VMEM((2, 2 * A_SPAN, 2 * A_SPAN), F32)] + [pltpu.VMEM((n_dil, t, LANES), F32)] * 3
                       + [pltpu.VMEM((A_UNROLL, 2 * A_SPAN, 2 * A_SPAN), BF16),
                          pltpu.VMEM((A_UNROLL, 2 * A_SPAN, LANES), F32)],
        compiler_params=_params(("arbitrary", "arbitrary")),
        name="attn_prompt",
    )(as4(q1), q4, q16, as4(k1), k4, k16, as4(v1), v4, v16)


def _decode_rows(bb, w_c, q_ref, k_ref, v_ref, kt_ref, vt_ref,
                 hq_ref, hk_ref, hv_ref, hg_ref, st_ref, nrm_ref, a_ref, bn_ref, so_ref):
    n_pat = len(A_DILATIONS)
    dist = w_c - lax.broadcasted_iota(jnp.int32, (A_HEADS, w_c), 1)
    cnt = jnp.zeros((A_HEADS, w_c), F32)
    for d in A_DILATIONS:
        cnt = cnt + jnp.logical_and(dist % d == 0, dist <= A_SPAN * d).astype(F32)
    used = cnt > 0.0
    head_s = lax.broadcasted_iota(jnp.int32, (A_HEADS, w_c), 0)
    head_o = lax.broadcasted_iota(jnp.int32, (A_HEADS, A_HEAD_DIM), 0)
    for b in range(bb):
        q = q_ref[b]
        k_new, v_new = k_ref[b], v_ref[b]
        q16 = q.astype(BF16)
        s_new = jnp.sum(q * k_new, axis=-1, keepdims=True)
        s = jnp.zeros((A_HEADS, w_c), F32)
        for h in range(A_HEADS):
            s = jnp.where(head_s == h, _dot(q16, kt_ref[b, h].astype(BF16)), s)
        m = jnp.maximum(jnp.max(jnp.where(used, s, -jnp.inf), axis=-1, keepdims=True), s_new)
        w = jnp.where(used, jnp.exp(s - m), 0.0) * cnt
        p_new = n_pat * jnp.exp(s_new - m)
        den = jnp.sum(w, axis=-1, keepdims=True) + p_new
        acc = p_new * v_new
        w16 = w.astype(BF16)
        for h in range(A_HEADS):
            acc = acc + jnp.where(head_o == h, _dot_nt(w16, vt_ref[b, h].astype(BF16)), 0.0)
        a_ref[b] = acc / den
        v_rows = hv_ref[b]
        packed = jnp.concatenate([hq_ref[b], hk_ref[b], jnp.exp(hg_ref[b]),
                                  jnp.zeros((LANES - 3 * B_HEADS, B_HEAD_DIM), F32)], axis=0)
        cols = packed.T
        outs = []
        for hd in range(B_HEADS):
            col = lambda i: cols[:, i * B_HEADS + hd:i * B_HEADS + hd + 1]
            st = col(2) * st_ref[b, hd] + col(1) * v_rows[hd:hd + 1, :]
            so_ref[b, hd] = st
            o = jnp.sum(col(0) * st, axis=0, keepdims=True)
            outs.append(_rms(o, nrm_ref[hd:hd + 1, :]))
        bn_ref[b] = jnp.concatenate(outs, axis=0)


N_DECODE_IN = 11


def _recurrent_kernel(th, bb, w_c, q_ref, k_ref, v_ref, g_ref, nrm_ref, *rest):
    dec_in = rest[:N_DECODE_IN]
    o_ref, st_ref, da_ref, dbn_ref, dso_ref, state_scr, b_scr, a_scr, part_scr = rest[-9:]
    j = pl.program_id(1)

    @pl.when(j == 0)
    def _():
        state_scr[...] = jnp.zeros_like(state_scr)

    piece = B_CHUNK * B_UNROLL
    rin = lax.broadcasted_iota(jnp.int32, (piece, B_HEAD_DIM), 0) & (B_CHUNK - 1)

    def scan_piece(i, carry):
        rows = pl.ds(pl.multiple_of(i * piece, piece), piece)
        b = g_ref[rows, :] * LOG2_E
        shift = 1
        while shift < B_CHUNK:
            b = b + jnp.where(rin >= shift, pltpu.roll(b, shift, axis=0), 0.0)
            shift *= 2
        b_scr[rows, :] = b
        return carry

    lax.fori_loop(0, th // piece, scan_piece, 0)

    row = lax.broadcasted_iota(jnp.int32, (B_CHUNK, B_HEAD_DIM), 0)
    a_t = lax.broadcasted_iota(jnp.int32, (B_CHUNK, B_CHUNK), 0)
    a_s = lax.broadcasted_iota(jnp.int32, (B_CHUNK, B_CHUNK), 1)
    levels = []
    half = B_CHUNK // 2
    while half >= 1:
        seg = 2 * half
        pair = jnp.logical_and(a_t // seg == a_s // seg,
                               jnp.logical_and(a_t % seg >= half, a_s % seg < half))
        upper = row % seg >= half
        levels.append((half, upper, jnp.where(upper, 1.0, -1.0), pair))
        half //= 2
    n_grp = B_CHUNK // 8
    sub8 = lax.broadcasted_iota(jnp.int32, (n_grp, 8, B_HEAD_DIM), 1)
    nrm = nrm_ref[...]

    def centre_value(bc, half):
        seg = 2 * half
        if seg >= 8:
            return jnp.concatenate(
                [jnp.broadcast_to(bc[s0 + half - 1:s0 + half, :], (seg, B_HEAD_DIM))
                 for s0 in range(0, B_CHUNK, seg)], axis=0)
        b3 = bc.reshape(n_grp, 8, B_HEAD_DIM)
        if half == 1:
            out = jnp.where(sub8 % 2 == 0, b3, pltpu.roll(b3, 1, axis=1))
        else:
            out = jnp.broadcast_to(b3[:, half - 1:half, :], b3.shape)
            for s0 in range(seg, 8, seg):
                out = jnp.where(sub8 >= s0, jnp.broadcast_to(b3[:, s0 + half - 1:s0 + half, :], b3.shape), out)
        return out.reshape(B_CHUNK, B_HEAD_DIM)

    def intra_chunk(q, k, v, bc):
        a = jnp.zeros((B_CHUNK, B_CHUNK), F32)
        for half, upper, sign, pair in levels:
            x = (jnp.where(upper, q, k) * jnp.exp2((bc - centre_value(bc, half)) * sign)).astype(BF16)
            a = jnp.where(pair, _dot_nt(x, x), a)
        return a, jnp.sum(q * k, axis=-1, keepdims=True) * v

    def chunk_rows(grp, u):
        return pl.ds(pl.multiple_of((grp * B_UNROLL + u) * B_CHUNK, B_CHUNK), B_CHUNK)

    def front(grp):
        st = state_scr[...]
        for u in range(B_UNROLL):
            rows = chunk_rows(grp, u)
            q, k, v, bc = q_ref[rows, :], k_ref[rows, :], v_ref[rows, :], b_scr[rows, :]
            o = _dot_nt((q * jnp.exp2(bc)).astype(BF16), st.astype(BF16))
            b_last = bc[B_CHUNK - 1:B_CHUNK, :]
            k_dec = (k * jnp.exp2(b_last - bc)).astype(BF16)
            st = st * jnp.exp2(b_last) + _dot_tn(v.astype(BF16), k_dec)
            a, o3 = intra_chunk(q, k, v, bc)
            a_scr[u] = a.astype(BF16)
            part_scr[u] = o + o3
        state_scr[...] = st

    def back(grp):
        for u in range(B_UNROLL):
            rows = chunk_rows(grp, u)
            o = part_scr[u] + _dot(a_scr[u], v_ref[rows, :].astype(BF16))
            o_ref[rows, :] = _rms(o, nrm)

    n_groups = th // (B_CHUNK * B_UNROLL)
    _decode_rows(bb, w_c, *dec_in, da_ref, dbn_ref, dso_ref)
    front(0)

    def steady(grp, carry):
        back(grp - 1)
        front(grp)
        return carry

    lax.fori_loop(1, n_groups, steady, 0)
    back(n_groups - 1)

    @pl.when(j == pl.num_programs(1) - 1)
    def _():
        st_ref[...] = state_scr[...].T


def _recurrent(layer, q, k, v, g, nrm, dq, dk, dv, cache_kt, cache_vt, dhq, dhk, dhv, dhg, state, states_out, th):
    s_len = q.shape[1]
    n = dq.shape[0]
    w_c = cache_kt.shape[-1]
    n_tiles = s_len // th
    steps = B_HEADS * n_tiles
    assert w_c >= A_SPAN * max(A_DILATIONS) and n % steps == 0
    bb = n // steps
    step = lambda h, j: h * n_tiles + j
    blk = pl.BlockSpec((None, th, B_HEAD_DIM), lambda h, j: (h, j, 0))
    a_spec = pl.BlockSpec((bb, A_HEADS, A_HEAD_DIM), lambda h, j: (step(h, j), 0, 0))
    headspec = pl.BlockSpec((bb, B_HEADS, B_HEAD_DIM), lambda h, j: (step(h, j), 0, 0))
    c_spec = pl.BlockSpec((None, bb, A_HEADS, A_HEAD_DIM, w_c), lambda h, j: (layer, step(h, j), 0, 0, 0))
    st_spec = pl.BlockSpec((None, bb, B_HEADS, B_HEAD_DIM, B_HEAD_DIM), lambda h, j: (layer, step(h, j), 0, 0, 0))
    a_heads = lambda a: a.reshape(n, A_HEADS, A_HEAD_DIM)
    heads = lambda a: a.reshape(n, B_HEADS, B_HEAD_DIM)
    in_specs = [blk] * 4 + [pl.BlockSpec((None, 1, B_HEAD_DIM), lambda h, j: (layer, 0, h))]
    in_specs += [a_spec] * 3 + [c_spec] * 2 + [headspec] * 4 + [st_spec, _layer_block((B_HEADS, B_HEAD_DIM), layer)]
    args = [q, k, v, g, nrm, a_heads(dq), a_heads(dk), a_heads(dv), cache_kt, cache_vt,
            heads(dhq), heads(dhk), heads(dhv), heads(dhg), state, nrm.reshape(DEPTH, B_HEADS, B_HEAD_DIM)]
    assert len(args) == 5 + N_DECODE_IN
    aliases = {}
    if states_out is not None:
        aliases = {len(args): 4}
        in_specs.append(pl.BlockSpec(memory_space=pl.ANY))
        args.append(states_out)
    bn, st_p, a_out, dbn, states = pl.pallas_call(
        functools.partial(_recurrent_kernel, th, bb, w_c),
        grid=(B_HEADS, n_tiles),
        in_specs=in_specs,
        out_specs=[pl.BlockSpec((th, B_HEAD_DIM), lambda h, j: (j, h)),
                   pl.BlockSpec((None, B_HEAD_DIM, B_HEAD_DIM), lambda h, j: (h, 0, 0)),
                   a_spec, headspec, st_spec],
        out_shape=[jax.ShapeDtypeStruct((s_len, B_WIDTH), F32),
                   jax.ShapeDtypeStruct((B_HEADS, B_HEAD_DIM, B_HEAD_DIM), F32),
                   jax.ShapeDtypeStruct((n, A_HEADS, A_HEAD_DIM), F32),
                   jax.ShapeDtypeStruct((n, B_HEADS, B_HEAD_DIM), F32),
                   jax.ShapeDtypeStruct(state.shape, F32)],
        scratch_shapes=[pltpu.VMEM((B_HEAD_DIM, B_HEAD_DIM), F32), pltpu.VMEM((th, B_HEAD_DIM), F32),
                        pltpu.VMEM((B_UNROLL, B_CHUNK, B_CHUNK), BF16),
                        pltpu.VMEM((B_UNROLL, B_CHUNK, B_HEAD_DIM), F32)],
        input_output_aliases=aliases,
        compiler_params=_params(("arbitrary", "arbitrary")),
        name="recurrent",
    )(*args)
    return bn, st_p, a_out.reshape(n, A_WIDTH).astype(BF16), dbn.reshape(n, B_WIDTH), states


def _mix(x, a16, bn, pre, post, wg0, wg1, wa, wb, wo):
    h = _rms(x, pre).astype(BF16)
    z0 = _dot(h, wg0)
    z1 = _dot(h, wg1)
    split = W_BLOCK - B_WIDTH
    g_b = z0[:, :B_WIDTH]
    gate_a = jnp.concatenate([z0[:, B_WIDTH:], z1[:, :split]], axis=1)
    gate_b = z1[:, split:]
    b_out = (bn * (g_b * jax.nn.sigmoid(g_b))).astype(BF16)
    mix = (jax.nn.sigmoid(gate_a) * _dot(a16, wa) + jax.nn.sigmoid(gate_b) * _dot(b_out, wb)).astype(BF16)
    return x + _rms(_dot(mix, wo), post)


def _mixout_kernel(x_ref, a_ref, bn_ref, pre_ref, post_ref, wg0_ref, wg1_ref, wa_ref, wb_ref, wo_ref, o_ref):
    for sl in _row_parts(x_ref.shape[0]):
        o_ref[sl, :] = _mix(x_ref[sl, :], a_ref[sl, :], bn_ref[sl, :], pre_ref[...], post_ref[...],
                            wg0_ref[...], wg1_ref[...], wa_ref[...], wb_ref[...], wo_ref[...])


def _mixout(x, a_out, bn, layer, pre, post, w_in, wa, wb, wo, tm):
    n = x.shape[0]
    row = lambda width: pl.BlockSpec((tm, width), lambda i: (i, 0))
    return pl.pallas_call(
        _mixout_kernel,
        grid=(n // tm,),
        in_specs=[row(D_MODEL), row(A_WIDTH), row(B_WIDTH), _vec(layer), _vec(layer), _col_block(3), _col_block(4),
                  _resident((A_WIDTH, D_MODEL)), _resident((B_WIDTH, D_MODEL)), _resident((D_MODEL, D_MODEL))],
        out_specs=row(D_MODEL),
        out_shape=jax.ShapeDtypeStruct((n, D_MODEL), F32),
        compiler_params=_params(("parallel",)),
        name="mixout",
    )(x, a_out, bn, pre, post, w_in, w_in, wa, wb, wo)


def _mixout_sample_kernel(x_ref, a_ref, bn_ref, pre_ref, post_ref, wg0_ref, wg1_ref, wa_ref, wb_ref, wo_ref,
                          o_ref, wa16_ref, wb16_ref, wo16_ref):
    wa, wb, wo = wa_ref[...].astype(BF16), wb_ref[...].astype(BF16), wo_ref[...].astype(BF16)
    wa16_ref[...] = wa
    wb16_ref[...] = wb
    wo16_ref[...] = wo
    o_ref[...] = _mix(x_ref[...], a_ref[...], bn_ref[...], pre_ref[...], post_ref[...],
                      wg0_ref[...], wg1_ref[...], wa, wb, wo)


def _mixout_sample(x, a_out, bn, layer, pre, post, w_in16, wa, wb, wo):
    n = x.shape[0]
    full = lambda *shape: pl.BlockSpec(shape, lambda i: (0,) * len(shape))
    shapes = ((A_WIDTH, D_MODEL), (B_WIDTH, D_MODEL), (D_MODEL, D_MODEL))
    return pl.pallas_call(
        _mixout_sample_kernel,
        grid=(1,),
        in_specs=[full(n, D_MODEL), full(n, A_WIDTH), full(n, B_WIDTH), _vec(layer), _vec(layer),
                  _col_block(3), _col_block(4)] + [_layer_block(s, layer) for s in shapes],
        out_specs=[full(n, D_MODEL)] + [full(*s) for s in shapes],
        out_shape=[jax.ShapeDtypeStruct((n, D_MODEL), F32)] + [jax.ShapeDtypeStruct(s, BF16) for s in shapes],
        compiler_params=_params(("arbitrary",)),
        name="mixout_sample",
    )(x, a_out, bn, pre, post, w_in16, w_in16, wa, wb, wo)


def _rope_tables(pos):
    half = ROPE_DIM // 2
    inv = ROPE_THETA ** (-jnp.arange(half, dtype=F32) / half)
    dim = jnp.arange(LANES) % A_HEAD_DIM
    ang = pos.astype(F32)[:, None] * inv[dim % half][None, :]
    cos, sin = jnp.cos(ang), jnp.sin(ang)
    lo, hi = (dim < half)[None, :], jnp.logical_and(dim >= half, dim < ROPE_DIM)[None, :]
    return (jnp.where(jnp.logical_or(lo, hi), cos, 1.0), jnp.where(lo, -sin, 0.0), jnp.where(hi, sin, 0.0))


def kernel(x_prompt, x_sample, cache_k, cache_v, state_hgrn, ffn1_norm_pre, ffn1_norm_post, ffn1_w_gate, ffn1_w_up, ffn1_w_down, mix_norm_pre, mix_norm_post, w_in, hgrn_lb_logits, hgrn_out_norm, w_a_out, w_b_out, w_mix_out, ffn2_norm_pre, ffn2_norm_post, ffn2_w_gate, ffn2_w_up, ffn2_w_down):
    batch, s_len, _ = x_prompt.shape
    n_dec, t_dec, _ = x_sample.shape
    assert batch == 1 and t_dec == 1 and s_len % ATTN_TILE == 0
    tm = 512
    tm_wide = 1024
    th = min(4096, s_len)
    cache_kt = jnp.transpose(cache_k, (0, 1, 3, 4, 2))
    cache_vt = jnp.transpose(cache_v, (0, 1, 3, 4, 2))
    yp = x_prompt.reshape(s_len, D_MODEL)
    ys = x_sample.reshape(n_dec, D_MODEL)
    tabs_p = _rope_tables(jnp.arange(s_len))
    tabs_s = _rope_tables(jnp.full((n_dec,), PAST_LEN))
    lbl = hgrn_lb_logits.astype(F32)
    vecs = lambda a: a.reshape(DEPTH, 1, -1).astype(F32)
    w_len = min(A_MAX_WINDOW, s_len)
    f1_pre, f1_post, f2_pre, f2_post = (vecs(a) for a in (ffn1_norm_pre, ffn1_norm_post, ffn2_norm_pre, ffn2_norm_post))
    m_pre, m_post, nrm = vecs(mix_norm_pre), vecs(mix_norm_post), vecs(hgrn_out_norm)
    kp, vp, sp, ksn, vsn = [], [], [], [], []
    states = None
    for l in range(DEPTH):
        ys, *f1 = _ffn_sample(ys, l, f1_pre, f1_post, ffn1_w_gate, ffn1_w_up, ffn1_w_down)
        qa, ka, va, sq, sk, sv, sg, win = _inproj_sample(ys, m_pre, tabs_s, lbl, w_in, l)
        yp = _ffn(yp, l, f1_pre, f1_post, *f1, tm_wide)
        q1, q4, q16, k1, k4, k16, v1, v4, v16, kf, vf = _inproj_attn(yp, m_pre, tabs_p, win, l, tm)
        hq, hk, hv, hg = _inproj_rec(yp, m_pre, lbl, win, l, tm)
        a_out = _attn_prompt(q1, q4, q16, k1, k4, k16, v1, v4, v16)
        bn, st_p, a_s, bn_s, states = _recurrent(l, hq, hk, hv, hg, nrm, qa, ka, va, cache_kt, cache_vt,
                                                 sq, sk, sv, sg, state_hgrn, states, th)
        ys, *mo = _mixout_sample(ys, a_s, bn_s, l, m_pre, m_post, win, w_a_out, w_b_out, w_mix_out)
        ys, *f2 = _ffn_sample(ys, l, f2_pre, f2_post, ffn2_w_gate, ffn2_w_up, ffn2_w_down)
        yp = _mixout(yp, a_out, bn, l, m_pre, m_post, win, *mo, tm_wide)
        yp = _ffn(yp, l, f2_pre, f2_post, *f2, tm_wide)
        kp.append(kf.reshape(batch, w_len, A_HEADS, A_HEAD_DIM))
        vp.append(vf.reshape(batch, w_len, A_HEADS, A_HEAD_DIM))
        sp.append(st_p.reshape(batch, B_HEADS, B_HEAD_DIM, B_HEAD_DIM))
        ksn.append(ka.reshape(n_dec, t_dec, A_HEADS, A_HEAD_DIM))
        vsn.append(va.reshape(n_dec, t_dec, A_HEADS, A_HEAD_DIM))
    return (yp.reshape(batch, s_len, D_MODEL), ys.reshape(n_dec, t_dec, D_MODEL),
            jnp.stack(kp), jnp.stack(vp), jnp.stack(sp), jnp.stack(ksn), jnp.stack(vsn), states)
```

```python
import functools

import jax
import jax.numpy as jnp
from jax import lax
from jax.experimental import pallas as pl
from jax.experimental.pallas import tpu as pltpu

F32 = jnp.float32
BF16 = jnp.bfloat16

D_MODEL = 1024
DEPTH = 2
PAST_LEN = 16384
A_HEADS = 8
A_HEAD_DIM = 64
A_WIDTH = A_HEADS * A_HEAD_DIM
A_DILATIONS = (1, 4, 16)
A_SPAN = 128
A_MAX_WINDOW = 2048
A_SCALE = A_HEAD_DIM ** -0.5
ROPE_THETA = 500000.0
ROPE_DIM = A_HEAD_DIM // 4
B_HEADS = 8
B_HEAD_DIM = 128
B_WIDTH = B_HEADS * B_HEAD_DIM
B_CHUNK = 64
B_UNROLL = 8
B_SCALE = B_HEAD_DIM ** -0.5
D_FF = 2816
FF_CHUNK = 256
ROW_PARTS = 4
EPS = 1e-6
LOG2_E = 1.4426950408889634

W_BLOCK = 3 * A_WIDTH
assert 3 * B_WIDTH == 2 * W_BLOCK and B_WIDTH + 2 * D_MODEL == 2 * W_BLOCK
LANES = 128
N_PAIRS = A_WIDTH // LANES
ATTN_TILE = A_SPAN * max(A_DILATIONS)
A_UNROLL = 16
VMEM_LIMIT = 56 * 1024 * 1024


def _params(sem, vmem=VMEM_LIMIT):
    return pltpu.CompilerParams(dimension_semantics=sem, vmem_limit_bytes=vmem)


def _resident(shape):
    nd = len(shape)
    return pl.BlockSpec(shape, lambda *_: (0,) * nd, pipeline_mode=pl.Buffered(1))


def _layer_block(shape, layer, col=0):
    index = (layer,) + (0,) * (len(shape) - 1) + (col,)
    return pl.BlockSpec((None,) + tuple(shape), lambda *_: index, pipeline_mode=pl.Buffered(1))


def _vec(layer):
    return _layer_block((1, D_MODEL), layer)


def _rms(x, g):
    y = x * lax.rsqrt(jnp.mean(x * x, axis=-1, keepdims=True) + EPS)
    return y * g


def _dot(a, b):
    return jnp.dot(a, b, preferred_element_type=F32)


def _dot_nt(a, b):
    return lax.dot_general(a, b, (((1,), (1,)), ((), ())), preferred_element_type=F32)


def _dot_tn(a, b):
    return lax.dot_general(a, b, (((0,), (0,)), ((), ())), preferred_element_type=F32)


def _row_parts(rows):
    parts = ROW_PARTS if rows % (8 * ROW_PARTS) == 0 else 1
    return [pl.ds(p * (rows // parts), rows // parts) for p in range(parts)]


def _ffn_kernel(x_ref, pre_ref, post_ref, wg_ref, wu_ref, wd_ref, o_ref):
    for sl in _row_parts(x_ref.shape[0]):
        x = x_ref[sl, :]
        h = _rms(x, pre_ref[...]).astype(BF16)
        g = _dot(h, wg_ref[...])
        u = _dot(h, wu_ref[...])
        a = (g * jax.nn.sigmoid(g) * u).astype(BF16)
        y = _dot(a, wd_ref[...])
        o_ref[sl, :] = x + 0.5 * _rms(y, post_ref[...])


def _ffn(x, layer, pre, post, wg, wu, wd, tm):
    n = x.shape[0]
    row = pl.BlockSpec((tm, D_MODEL), lambda i: (i, 0))
    return pl.pallas_call(
        _ffn_kernel,
        grid=(n // tm,),
        in_specs=[row, _vec(layer), _vec(layer), _resident((D_MODEL, D_FF)),
                  _resident((D_MODEL, D_FF)), _resident((D_FF, D_MODEL))],
        out_specs=row,
        out_shape=jax.ShapeDtypeStruct((n, D_MODEL), F32),
        compiler_params=_params(("parallel",)),
        name="ffn",
    )(x, pre, post, wg, wu, wd)


def _ffn_sample_kernel(x_ref, pre_ref, post_ref, wg_ref, wu_ref, wd_ref,
                       o_ref, wg16_ref, wu16_ref, wd16_ref, h_scr, acc_scr):
    c = pl.program_id(0)

    @pl.when(c == 0)
    def _():
        h_scr[...] = _rms(x_ref[...], pre_ref[...]).astype(BF16)
        acc_scr[...] = jnp.zeros_like(acc_scr)

    wg, wu, wd = wg_ref[...].astype(BF16), wu_ref[...].astype(BF16), wd_ref[...].astype(BF16)
    wg16_ref[...] = wg
    wu16_ref[...] = wu
    wd16_ref[...] = wd
    h = h_scr[...]
    g = _dot(h, wg)
    u = _dot(h, wu)
    acc_scr[...] += _dot((g * jax.nn.sigmoid(g) * u).astype(BF16), wd)

    @pl.when(c == pl.num_programs(0) - 1)
    def _():
        o_ref[...] = x_ref[...] + 0.5 * _rms(acc_scr[...], post_ref[...])


def _ffn_sample(x, layer, pre, post, wg, wu, wd):
    n = x.shape[0]
    full = pl.BlockSpec((n, D_MODEL), lambda c: (0, 0))
    col = lambda: pl.BlockSpec((None, D_MODEL, FF_CHUNK), lambda c: (layer, 0, c))
    col16 = lambda: pl.BlockSpec((D_MODEL, FF_CHUNK), lambda c: (0, c))
    return pl.pallas_call(
        _ffn_sample_kernel,
        grid=(D_FF // FF_CHUNK,),
        in_specs=[full, _vec(layer), _vec(layer), col(), col(),
                  pl.BlockSpec((None, FF_CHUNK, D_MODEL), lambda c: (layer, c, 0))],
        out_specs=[full, col16(), col16(), pl.BlockSpec((FF_CHUNK, D_MODEL), lambda c: (c, 0))],
        out_shape=[jax.ShapeDtypeStruct((n, D_MODEL), F32), jax.ShapeDtypeStruct((D_MODEL, D_FF), BF16),
                   jax.ShapeDtypeStruct((D_MODEL, D_FF), BF16), jax.ShapeDtypeStruct((D_FF, D_MODEL), BF16)],
        scratch_shapes=[pltpu.VMEM((n, D_MODEL), BF16), pltpu.VMEM((n, D_MODEL), F32)],
        compiler_params=_params(("arbitrary",)),
        name="ffn_sample",
    )(x, pre, post, wg, wu, wd)


def _lower_bound(logits, layer):
    e = jnp.exp(logits - jnp.max(logits, axis=0, keepdims=True))
    sm = e / jnp.sum(e, axis=0, keepdims=True)
    lb = jnp.zeros((1, B_WIDTH), F32)
    for i in range(1, layer + 1):
        lb = lb + sm[i:i + 1, :]
    return lb


def _rope_slab(t, cos, sin_lo, sin_hi):
    return t * cos + pltpu.roll(t, LANES - ROPE_DIM // 2, axis=1) * sin_lo + pltpu.roll(t, ROPE_DIM // 2, axis=1) * sin_hi


def _col_block(col):
    return pl.BlockSpec((D_MODEL, W_BLOCK), lambda *_: (0, col), pipeline_mode=pl.Buffered(1))


def _hgrn_gates(z0, z1, lb):
    split = W_BLOCK - B_WIDTH
    q = z0[:, :B_WIDTH] * B_SCALE
    f_raw = jnp.concatenate([z0[:, B_WIDTH:], z1[:, :split]], axis=1)
    i_raw = z1[:, split:]
    f = lb + (1.0 - lb) * jax.nn.sigmoid(f_raw)
    return q, 1.0 - f, i_raw * jax.nn.sigmoid(i_raw), jnp.log(f)


def _inproj_attn_kernel(tm, x_ref, pre_ref, cos_ref, slo_ref, shi_ref, wa_ref,
                        q1_ref, q4_ref, q16_ref, k1_ref, k4_ref, k16_ref, v1_ref, v4_ref, v16_ref,
                        kf_ref, vf_ref, stage_ref, stage4_ref):
    h = _rms(x_ref[...], pre_ref[...]).astype(BF16)
    za = _dot(h, wa_ref[...])
    cos, slo, shi = cos_ref[...], slo_ref[...], shi_ref[...]
    outs = ((q1_ref, q4_ref, q16_ref), (k1_ref, k4_ref, k16_ref), (v1_ref, v4_ref, v16_ref))
    for s in range(3 * N_PAIRS):
        kind, hp = divmod(s, N_PAIRS)
        t = za[:, s * LANES:(s + 1) * LANES]
        if kind < 2:
            t = _rope_slab(t, cos, slo, shi)
        if kind == 0:
            t = t * (A_SCALE * LOG2_E)
        if kind == 1:
            kf_ref[:, hp * LANES:(hp + 1) * LANES] = t
        if kind == 2:
            vf_ref[:, hp * LANES:(hp + 1) * LANES] = t
        stage_ref[s] = t
        o1, o4, o16 = outs[kind]
        o1[hp] = t.astype(BF16)
        for r in range(4):
            c4 = stage_ref[s, pl.ds(r, tm // 4, stride=4), :]
            o4[hp, r] = c4.astype(BF16)
            stage4_ref[s, r] = c4
        for r in range(4):
            for a in range(4):
                o16[hp, r + 4 * a] = stage4_ref[s, r, pl.ds(a, tm // 16, stride=4), :].astype(BF16)


def _inproj_attn(x, pre, tabs, w_in, layer, tm):
    s_len = x.shape[0]
    w_len = min(A_MAX_WINDOW, s_len)
    first = (s_len - w_len) // tm
    row = pl.BlockSpec((tm, D_MODEL), lambda i: (i, 0))
    tab = pl.BlockSpec((tm, LANES), lambda i: (i, 0))
    l1 = pl.BlockSpec((N_PAIRS, tm, LANES), lambda i: (0, i, 0))
    l4 = pl.BlockSpec((N_PAIRS, 4, tm // 4, LANES), lambda i: (0, 0, i, 0))
    l16 = pl.BlockSpec((N_PAIRS, 16, tm // 16, LANES), lambda i: (0, 0, i, 0))
    win = pl.BlockSpec((tm, A_WIDTH), lambda i: (jnp.maximum(i - first, 0), 0))
    s1 = jax.ShapeDtypeStruct((N_PAIRS, s_len, LANES), BF16)
    s4 = jax.ShapeDtypeStruct((N_PAIRS, 4, s_len // 4, LANES), BF16)
    s16 = jax.ShapeDtypeStruct((N_PAIRS, 16, s_len // 16, LANES), BF16)
    sw = jax.ShapeDtypeStruct((w_len, A_WIDTH), F32)
    return pl.pallas_call(
        functools.partial(_inproj_attn_kernel, tm),
        grid=(s_len // tm,),
        in_specs=[row, _vec(layer), tab, tab, tab, _col_block(0)],
        out_specs=[l1, l4, l16, l1, l4, l16, l1, l4, l16, win, win],
        out_shape=[s1, s4, s16, s1, s4, s16, s1, s4, s16, sw, sw],
        scratch_shapes=[pltpu.VMEM((3 * N_PAIRS, tm, LANES), F32), pltpu.VMEM((3 * N_PAIRS, 4, tm // 4, LANES), F32)],
        compiler_params=_params(("arbitrary",)),
        name="inproj_attn",
    )(x, pre, *tabs, w_in)


def _inproj_rec_kernel(layer, x_ref, pre_ref, lbl_ref, wb0_ref, wb1_ref, hq_ref, hk_ref, hv_ref, hg_ref):
    lb = _lower_bound(lbl_ref[...], layer)
    for rows in _row_parts(x_ref.shape[0]):
        h = _rms(x_ref[rows, :], pre_ref[...]).astype(BF16)
        q, k, v, g = _hgrn_gates(_dot(h, wb0_ref[...]), _dot(h, wb1_ref[...]), lb)
        for hd in range(B_HEADS):
            sl = slice(hd * B_HEAD_DIM, (hd + 1) * B_HEAD_DIM)
            hq_ref[hd, rows, :] = q[:, sl]
            hk_ref[hd, rows, :] = k[:, sl]
            hv_ref[hd, rows, :] = v[:, sl]
            hg_ref[hd, rows, :] = g[:, sl]


def _inproj_rec(x, pre, lbl, w_in, layer, tm):
    s_len = x.shape[0]
    row = pl.BlockSpec((tm, D_MODEL), lambda i: (i, 0))
    hd = pl.BlockSpec((B_HEADS, tm, B_HEAD_DIM), lambda i: (0, i, 0))
    sh = jax.ShapeDtypeStruct((B_HEADS, s_len, B_HEAD_DIM), F32)
    return pl.pallas_call(
        functools.partial(_inproj_rec_kernel, layer),
        grid=(s_len // tm,),
        in_specs=[row, _vec(layer), _resident((DEPTH, B_WIDTH)), _col_block(1), _col_block(2)],
        out_specs=[hd] * 4,
        out_shape=[sh] * 4,
        compiler_params=_params(("parallel",)),
        name="inproj_rec",
    )(x, pre, lbl, w_in, w_in)


def _inproj_sample_kernel(layer, x_ref, pre_ref, cos_ref, slo_ref, shi_ref, lbl_ref, w_ref,
                          qa_ref, ka_ref, va_ref, hq_ref, hk_ref, hv_ref, hg_ref, w16_ref, h_scr, z_scr):
    c = pl.program_id(0)

    @pl.when(c == 0)
    def _():
        h_scr[...] = _rms(x_ref[...], pre_ref[...]).astype(BF16)

    w = w_ref[...].astype(BF16)
    w16_ref[...] = w

    @pl.when(c == 0)
    def _():
        za = _dot(h_scr[...], w)
        cos, slo, shi = cos_ref[...], slo_ref[...], shi_ref[...]
        outs = (qa_ref, ka_ref, va_ref)
        for s in range(3 * N_PAIRS):
            kind, hp = divmod(s, N_PAIRS)
            t = za[:, s * LANES:(s + 1) * LANES]
            if kind < 2:
                t = _rope_slab(t, cos, slo, shi)
            if kind == 0:
                t = t * A_SCALE
            outs[kind][:, hp * LANES:(hp + 1) * LANES] = t

    @pl.when(c == 1)
    def _():
        z_scr[...] = _dot(h_scr[...], w)

    @pl.when(c == 2)
    def _():
        q, k, v, g = _hgrn_gates(z_scr[...], _dot(h_scr[...], w), _lower_bound(lbl_ref[...], layer))
        hq_ref[...] = q
        hk_ref[...] = k
        hv_ref[...] = v
        hg_ref[...] = g


def _inproj_sample(x, pre, tabs, lbl, w_in, layer):
    n = x.shape[0]
    n_col = w_in.shape[-1] // W_BLOCK
    full = lambda *shape: pl.BlockSpec(shape, lambda c: (0,) * len(shape))
    sa = jax.ShapeDtypeStruct((n, A_WIDTH), F32)
    sb = jax.ShapeDtypeStruct((n, B_WIDTH), F32)
    return pl.pallas_call(
        functools.partial(_inproj_sample_kernel, layer),
        grid=(n_col,),
        in_specs=[full(n, D_MODEL), _vec(layer), full(n, LANES), full(n, LANES), full(n, LANES),
                  full(DEPTH, B_WIDTH), pl.BlockSpec((None, D_MODEL, W_BLOCK), lambda c: (layer, 0, c))],
        out_specs=[full(n, A_WIDTH)] * 3 + [full(n, B_WIDTH)] * 4
                  + [pl.BlockSpec((D_MODEL, W_BLOCK), lambda c: (0, c))],
        out_shape=[sa] * 3 + [sb] * 4 + [jax.ShapeDtypeStruct((D_MODEL, n_col * W_BLOCK), BF16)],
        scratch_shapes=[pltpu.VMEM((n, D_MODEL), BF16), pltpu.VMEM((n, W_BLOCK), F32)],
        compiler_params=_params(("arbitrary",)),
        name="inproj_sample",
    )(x, pre, *tabs, lbl, w_in)


def _attn_scores(q, k, bias):
    first = lax.broadcasted_iota(jnp.int32, (A_SPAN, LANES), 1) < A_HEAD_DIM
    zero = jnp.zeros_like(q)
    q2 = jnp.concatenate([jnp.where(first, q, zero), jnp.where(first, zero, q)], axis=0)
    s = _dot_nt(q2, k) + bias
    m = jnp.max(jnp.maximum(s[:, :A_SPAN], s[:, A_SPAN:]), axis=-1, keepdims=True)
    return jnp.exp2(s - m).astype(BF16), jnp.broadcast_to(m, (2 * A_SPAN, LANES))


def _attn_values(p, m, v_ext):
    first = lax.broadcasted_iota(jnp.int32, (A_SPAN, LANES), 1) < A_HEAD_DIM
    oe = _dot(p, v_ext)
    pick = lambda t: jnp.where(first, t[:A_SPAN], t[A_SPAN:])
    return pick(oe[:, :LANES]), pick(m), pick(oe[:, LANES:])


def _attn_prompt_kernel(q1_ref, q4_ref, q16_ref, k1_ref, k4_ref, k16_ref, v1_ref, v4_ref, v16_ref,
                        o_ref, ks1, ks4, ks16, vs1, vs4, vs16, bias_scr, o_scr, m_scr, d_scr, p_scr, mx_scr):
    j = pl.program_id(1)
    q_refs = (q1_ref, q4_ref, q16_ref)
    k_in = (k1_ref, k4_ref, k16_ref)
    v_in = (v1_ref, v4_ref, v16_ref)
    k_scr = (ks1, ks4, ks16)
    v_scr = (vs1, vs4, vs16)

    @pl.when(j == 0)
    def _():
        for scr in k_scr:
            scr[:, 0:A_SPAN, :] = jnp.zeros((scr.shape[0], A_SPAN, LANES), BF16)
        for scr in v_scr:
            scr[:, 0:A_SPAN, 0:LANES] = jnp.zeros((scr.shape[0], A_SPAN, LANES), BF16)
            scr[:, :, LANES:] = jnp.ones((scr.shape[0], scr.shape[1], LANES), BF16)
        qi = lax.broadcasted_iota(jnp.int32, (2 * A_SPAN, 2 * A_SPAN), 0) % A_SPAN
        ki = lax.broadcasted_iota(jnp.int32, (2 * A_SPAN, 2 * A_SPAN), 1)
        band = jnp.logical_and(ki >= qi, ki <= qi + A_SPAN)
        bias_scr[1] = jnp.where(band, 0.0, -jnp.inf)
        bias_scr[0] = jnp.where(jnp.logical_and(band, ki >= A_SPAN), 0.0, -jnp.inf)

    zero = jnp.minimum(j, 0)

    def stage_tile(_, carry):
        for p, d in enumerate(A_DILATIONS):
            k_scr[p][:, A_SPAN:, :] = k_in[p][...]
            v_scr[p][:, A_SPAN:, 0:LANES] = v_in[p][...]
        return carry

    lax.fori_loop(0, 1 + zero, stage_tile, 0)

    def locate(p, grp, u):
        per_class = ATTN_TILE // A_DILATIONS[p] // A_SPAN
        b = grp * A_UNROLL + u
        n = b % per_class
        return b // per_class, pl.multiple_of(n * A_SPAN, A_SPAN), n

    def front(p, grp):
        for u in range(A_UNROLL):
            r, lo, n = locate(p, grp, u)
            has_prev = jnp.logical_or(j > 0, n > 0).astype(jnp.int32)
            p_scr[u], mx_scr[u] = _attn_scores(q_refs[p][r, pl.ds(lo, A_SPAN), :],
                                               k_scr[p][r, pl.ds(lo, 2 * A_SPAN), :], bias_scr[has_prev])

    def back(p, grp):
        d = A_DILATIONS[p]
        for u in range(A_UNROLL):
            r, lo, n = locate(p, grp, u)
            acc, m, den = _attn_values(p_scr[u], mx_scr[u], v_scr[p][r, pl.ds(lo, 2 * A_SPAN), :])
            if d > 1:
                rows = pl.ds(n * (A_SPAN * d) + r, A_SPAN, stride=d)
                o_scr[p - 1, rows, :] = acc
                m_scr[p - 1, rows, :] = m
                d_scr[p - 1, rows, :] = den
            else:
                rows = pl.ds(lo, A_SPAN)
                m1, m2 = m_scr[0, rows, :], m_scr[1, rows, :]
                top = jnp.maximum(jnp.maximum(m, m1), m2)
                w0, w1, w2 = jnp.exp2(m - top), jnp.exp2(m1 - top), jnp.exp2(m2 - top)
                num = w0 * acc + w1 * o_scr[0, rows, :] + w2 * o_scr[1, rows, :]
                tot = w0 * den + w1 * d_scr[0, rows, :] + w2 * d_scr[1, rows, :]
                o_ref[rows, :] = (num / tot).astype(BF16)

    n_grp = ATTN_TILE // A_SPAN // A_UNROLL
    order = tuple(range(1, len(A_DILATIONS))) + (0,)
    front(order[0], 0)
    for idx, p in enumerate(order):
        if n_grp > 1:

            def steady(grp, carry, p=p):
                back(p, grp - 1)
                front(p, grp)
                return carry

            lax.fori_loop(1, n_grp + zero, steady, 0)
        if idx + 1 < len(order):
            back(p, n_grp - 1)
            front(order[idx + 1], 0)

    def key_tails(_, carry):
        for p, d in enumerate(A_DILATIONS):
            tail = ATTN_TILE // d
            k_scr[p][:, 0:A_SPAN, :] = k_scr[p][:, tail:tail + A_SPAN, :]
        return carry

    lax.fori_loop(0, 1 + zero, key_tails, 0)
    back(order[-1], n_grp - 1)

    def value_tails(_, carry):
        for p, d in enumerate(A_DILATIONS):
            tail = ATTN_TILE // d
            v_scr[p][:, 0:A_SPAN, 0:LANES] = v_scr[p][:, tail:tail + A_SPAN, 0:LANES]
        return carry

    lax.fori_loop(0, 1 + zero, value_tails, 0)


def _attn_prompt(q1, q4, q16, k1, k4, k16, v1, v4, v16):
    s_len = q1.shape[1]
    t = ATTN_TILE
    b1 = pl.BlockSpec((None, 1, t, LANES), lambda hp, j: (hp, 0, j, 0))
    b4 = pl.BlockSpec((None, 4, t // 4, LANES), lambda hp, j: (hp, 0, j, 0))
    b16 = pl.BlockSpec((None, 16, t // 16, LANES), lambda hp, j: (hp, 0, j, 0))
    as4 = lambda a: a.reshape(N_PAIRS, 1, s_len, LANES)
    scr = lambda d, width: pltpu.VMEM((d, A_SPAN + t // d, width), BF16)
    n_dil = len(A_DILATIONS) - 1
    return pl.pallas_call(
        _attn_prompt_kernel,
        grid=(N_PAIRS, s_len // t),
        in_specs=[b1, b4, b16] * 3,
        out_specs=pl.BlockSpec((t, LANES), lambda hp, j: (j, hp)),
        out_shape=jax.ShapeDtypeStruct((s_len, A_WIDTH), BF16),
        scratch_shapes=[scr(d, LANES) for d in A_DILATIONS] + [scr(d, 2 * LANES) for d in A_DILATIONS]
                       + [pltpu.VMEM((2, 2 * A_SPAN, 2 * A_SPAN), F32)] + [pltpu.VMEM((n_dil, t, LANES), F32)] * 3
                       + [pltpu.VMEM((A_UNROLL, 2 * A_SPAN, 2 * A_SPAN), BF16),
                          pltpu.VMEM((A_UNROLL, 2 * A_SPAN, LANES), F32)],
        compiler_params=_params(("arbitrary", "arbitrary")),
        name="attn_prompt",
    )(as4(q1), q4, q16, as4(k1), k4, k16, as4(v1), v4, v16)


def _decode_rows(bb, w_c, q_ref, k_ref, v_ref, kt_ref, vt_ref,
                 hq_ref, hk_ref, hv_ref, hg_ref, st_ref, nrm_ref, a_ref, bn_ref, so_ref):
    n_pat = len(A_DILATIONS)
    dist = w_c - lax.broadcasted_iota(jnp.int32, (A_HEADS, w_c), 1)
    cnt = jnp.zeros((A_HEADS, w_c), F32)
    for d in A_DILATIONS:
        cnt = cnt + jnp.logical_and(dist % d == 0, dist <= A_SPAN * d).astype(F32)
    used = cnt > 0.0
    head_s = lax.broadcasted_iota(jnp.int32, (A_HEADS, w_c), 0)
    head_o = lax.broadcasted_iota(jnp.int32, (A_HEADS, A_HEAD_DIM), 0)
    for b in range(bb):
        q = q_ref[b]
        k_new, v_new = k_ref[b], v_ref[b]
        q16 = q.astype(BF16)
        s_new = jnp.sum(q * k_new, axis=-1, keepdims=True)
        s = jnp.zeros((A_HEADS, w_c), F32)
        for h in range(A_HEADS):
            s = jnp.where(head_s == h, _dot(q16, kt_ref[b, h].astype(BF16)), s)
        m = jnp.maximum(jnp.max(jnp.where(used, s, -jnp.inf), axis=-1, keepdims=True), s_new)
        w = jnp.where(used, jnp.exp(s - m), 0.0) * cnt
        p_new = n_pat * jnp.exp(s_new - m)
        den = jnp.sum(w, axis=-1, keepdims=True) + p_new
        acc = p_new * v_new
        w16 = w.astype(BF16)
        for h in range(A_HEADS):
            acc = acc + jnp.where(head_o == h, _dot_nt(w16, vt_ref[b, h].astype(BF16)), 0.0)
        a_ref[b] = acc / den
        v_rows = hv_ref[b]
        packed = jnp.concatenate([hq_ref[b], hk_ref[b], jnp.exp(hg_ref[b]),
                                  jnp.zeros((LANES - 3 * B_HEADS, B_HEAD_DIM), F32)], axis=0)
        cols = packed.T
        outs = []
        for hd in range(B_HEADS):
            col = lambda i: cols[:, i * B_HEADS + hd:i * B_HEADS + hd + 1]
            st = col(2) * st_ref[b, hd] + col(1) * v_rows[hd:hd + 1, :]
            so_ref[b, hd] = st
            o = jnp.sum(col(0) * st, axis=0, keepdims=True)
            outs.append(_rms(o, nrm_ref[hd:hd + 1, :]))
        bn_ref[b] = jnp.concatenate(outs, axis=0)


N_DECODE_IN = 11


def _recurrent_kernel(th, bb, w_c, q_ref, k_ref, v_ref, g_ref, nrm_ref, *rest):
    dec_in = rest[:N_DECODE_IN]
    o_ref, st_ref, da_ref, dbn_ref, dso_ref, state_scr, b_scr, a_scr, part_scr = rest[-9:]
    j = pl.program_id(1)

    @pl.when(j == 0)
    def _():
        state_scr[...] = jnp.zeros_like(state_scr)

    piece = B_CHUNK * B_UNROLL
    rin = lax.broadcasted_iota(jnp.int32, (piece, B_HEAD_DIM), 0) & (B_CHUNK - 1)

    def scan_piece(i, carry):
        rows = pl.ds(pl.multiple_of(i * piece, piece), piece)
        b = g_ref[rows, :] * LOG2_E
        shift = 1
        while shift < B_CHUNK:
            b = b + jnp.where(rin >= shift, pltpu.roll(b, shift, axis=0), 0.0)
            shift *= 2
        b_scr[rows, :] = b
        return carry

    lax.fori_loop(0, th // piece, scan_piece, 0)

    row = lax.broadcasted_iota(jnp.int32, (B_CHUNK, B_HEAD_DIM), 0)
    a_t = lax.broadcasted_iota(jnp.int32, (B_CHUNK, B_CHUNK), 0)
    a_s = lax.broadcasted_iota(jnp.int32, (B_CHUNK, B_CHUNK), 1)
    levels = []
    half = B_CHUNK // 2
    while half >= 1:
        seg = 2 * half
        pair = jnp.logical_and(a_t // seg == a_s // seg,
                               jnp.logical_and(a_t % seg >= half, a_s % seg < half))
        upper = row % seg >= half
        levels.append((half, upper, jnp.where(upper, 1.0, -1.0), pair))
        half //= 2
    n_grp = B_CHUNK // 8
    sub8 = lax.broadcasted_iota(jnp.int32, (n_grp, 8, B_HEAD_DIM), 1)
    nrm = nrm_ref[...]

    def centre_value(bc, half):
        seg = 2 * half
        if seg >= 8:
            return jnp.concatenate(
                [jnp.broadcast_to(bc[s0 + half - 1:s0 + half, :], (seg, B_HEAD_DIM))
                 for s0 in range(0, B_CHUNK, seg)], axis=0)
        b3 = bc.reshape(n_grp, 8, B_HEAD_DIM)
        if half == 1:
            out = jnp.where(sub8 % 2 == 0, b3, pltpu.roll(b3, 1, axis=1))
        else:
            out = jnp.broadcast_to(b3[:, half - 1:half, :], b3.shape)
            for s0 in range(seg, 8, seg):
                out = jnp.where(sub8 >= s0, jnp.broadcast_to(b3[:, s0 + half - 1:s0 + half, :], b3.shape), out)
        return out.reshape(B_CHUNK, B_HEAD_DIM)

    def intra_chunk(q, k, v, bc):
        a = jnp.zeros((B_CHUNK, B_CHUNK), F32)
        for half, upper, sign, pair in levels:
            x = (jnp.where(upper, q, k) * jnp.exp2((bc - centre_value(bc, half)) * sign)).astype(BF16)
            a = jnp.where(pair, _dot_nt(x, x), a)
        return a, jnp.sum(q * k, axis=-1, keepdims=True) * v

    def chunk_rows(grp, u):
        return pl.ds(pl.multiple_of((grp * B_UNROLL + u) * B_CHUNK, B_CHUNK), B_CHUNK)

    def front(grp):
        st = state_scr[...]
        for u in range(B_UNROLL):
            rows = chunk_rows(grp, u)
            q, k, v, bc = q_ref[rows, :], k_ref[rows, :], v_ref[rows, :], b_scr[rows, :]
            o = _dot_nt((q * jnp.exp2(bc)).astype(BF16), st.astype(BF16))
            b_last = bc[B_CHUNK - 1:B_CHUNK, :]
            k_dec = (k * jnp.exp2(b_last - bc)).astype(BF16)
            st = st * jnp.exp2(b_last) + _dot_tn(v.astype(BF16), k_dec)
            a, o3 = intra_chunk(q, k, v, bc)
            a_scr[u] = a.astype(BF16)
            part_scr[u] = o + o3
        state_scr[...] = st

    def back(grp):
        for u in range(B_UNROLL):
            rows = chunk_rows(grp, u)
            o = part_scr[u] + _dot(a_scr[u], v_ref[rows, :].astype(BF16))
            o_ref[rows, :] = _rms(o, nrm)

    n_groups = th // (B_CHUNK * B_UNROLL)
    _decode_rows(bb, w_c, *dec_in, da_ref, dbn_ref, dso_ref)
    front(0)

    def steady(grp, carry):
        back(grp - 1)
        front(grp)
        return carry

    lax.fori_loop(1, n_groups, steady, 0)
    back(n_groups - 1)

    @pl.when(j == pl.num_programs(1) - 1)
    def _():
        st_ref[...] = state_scr[...].T


def _recurrent(layer, q, k, v, g, nrm, dq, dk, dv, cache_kt, cache_vt, dhq, dhk, dhv, dhg, state, states_out, th):
    s_len = q.shape[1]
    n = dq.shape[0]
    w_c = cache_kt.shape[-1]
    n_tiles = s_len // th
    steps = B_HEADS * n_tiles
    assert w_c >= A_SPAN * max(A_DILATIONS) and n % steps == 0
    bb = n // steps
    step = lambda h, j: h * n_tiles + j
    blk = pl.BlockSpec((None, th, B_HEAD_DIM), lambda h, j: (h, j, 0))
    a_spec = pl.BlockSpec((bb, A_HEADS, A_HEAD_DIM), lambda h, j: (step(h, j), 0, 0))
    headspec = pl.BlockSpec((bb, B_HEADS, B_HEAD_DIM), lambda h, j: (step(h, j), 0, 0))
    c_spec = pl.BlockSpec((None, bb, A_HEADS, A_HEAD_DIM, w_c), lambda h, j: (layer, step(h, j), 0, 0, 0))
    st_spec = pl.BlockSpec((None, bb, B_HEADS, B_HEAD_DIM, B_HEAD_DIM), lambda h, j: (layer, step(h, j), 0, 0, 0))
    a_heads = lambda a: a.reshape(n, A_HEADS, A_HEAD_DIM)
    heads = lambda a: a.reshape(n, B_HEADS, B_HEAD_DIM)
    in_specs = [blk] * 4 + [pl.BlockSpec((None, 1, B_HEAD_DIM), lambda h, j: (layer, 0, h))]
    in_specs += [a_spec] * 3 + [c_spec] * 2 + [headspec] * 4 + [st_spec, _layer_block((B_HEADS, B_HEAD_DIM), layer)]
    args = [q, k, v, g, nrm, a_heads(dq), a_heads(dk), a_heads(dv), cache_kt, cache_vt,
            heads(dhq), heads(dhk), heads(dhv), heads(dhg), state, nrm.reshape(DEPTH, B_HEADS, B_HEAD_DIM)]
    assert len(args) == 5 + N_DECODE_IN
    aliases = {}
    if states_out is not None:
        aliases = {len(args): 4}
        in_specs.append(pl.BlockSpec(memory_space=pl.ANY))
        args.append(states_out)
    bn, st_p, a_out, dbn, states = pl.pallas_call(
        functools.partial(_recurrent_kernel, th, bb, w_c),
        grid=(B_HEADS, n_tiles),
        in_specs=in_specs,
        out_specs=[pl.BlockSpec((th, B_HEAD_DIM), lambda h, j: (j, h)),
                   pl.BlockSpec((None, B_HEAD_DIM, B_HEAD_DIM), lambda h, j: (h, 0, 0)),
                   a_spec, headspec, st_spec],
        out_shape=[jax.ShapeDtypeStruct((s_len, B_WIDTH), F32),
                   jax.ShapeDtypeStruct((B_HEADS, B_HEAD_DIM, B_HEAD_DIM), F32),
                   jax.ShapeDtypeStruct((n, A_HEADS, A_HEAD_DIM), F32),
                   jax.ShapeDtypeStruct((n, B_HEADS, B_HEAD_DIM), F32),
                   jax.ShapeDtypeStruct(state.shape, F32)],
        scratch_shapes=[pltpu.VMEM((B_HEAD_DIM, B_HEAD_DIM), F32), pltpu.VMEM((th, B_HEAD_DIM), F32),
                        pltpu.VMEM((B_UNROLL, B_CHUNK, B_CHUNK), BF16),
                        pltpu.VMEM((B_UNROLL, B_CHUNK, B_HEAD_DIM), F32)],
        input_output_aliases=aliases,
        compiler_params=_params(("arbitrary", "arbitrary")),
        name="recurrent",
    )(*args)
    return bn, st_p, a_out.reshape(n, A_WIDTH).astype(BF16), dbn.reshape(n, B_WIDTH), states


def _mix(x, a16, bn, pre, post, wg0, wg1, wa, wb, wo):
    h = _rms(x, pre).astype(BF16)
    z0 = _dot(h, wg0)
    z1 = _dot(h, wg1)
    split = W_BLOCK - B_WIDTH
    g_b = z0[:, :B_WIDTH]
    gate_a = jnp.concatenate([z0[:, B_WIDTH:], z1[:, :split]], axis=1)
    gate_b = z1[:, split:]
    b_out = (bn * (g_b * jax.nn.sigmoid(g_b))).astype(BF16)
    mix = (jax.nn.sigmoid(gate_a) * _dot(a16, wa) + jax.nn.sigmoid(gate_b) * _dot(b_out, wb)).astype(BF16)
    return x + _rms(_dot(mix, wo), post)


def _mixout_kernel(x_ref, a_ref, bn_ref, pre_ref, post_ref, wg0_ref, wg1_ref, wa_ref, wb_ref, wo_ref, o_ref):
    for sl in _row_parts(x_ref.shape[0]):
        o_ref[sl, :] = _mix(x_ref[sl, :], a_ref[sl, :], bn_ref[sl, :], pre_ref[...], post_ref[...],
                            wg0_ref[...], wg1_ref[...], wa_ref[...], wb_ref[...], wo_ref[...])


def _mixout(x, a_out, bn, layer, pre, post, w_in, wa, wb, wo, tm):
    n = x.shape[0]
    row = lambda width: pl.BlockSpec((tm, width), lambda i: (i, 0))
    return pl.pallas_call(
        _mixout_kernel,
        grid=(n // tm,),
        in_specs=[row(D_MODEL), row(A_WIDTH), row(B_WIDTH), _vec(layer), _vec(layer), _col_block(3), _col_block(4),
                  _resident((A_WIDTH, D_MODEL)), _resident((B_WIDTH, D_MODEL)), _resident((D_MODEL, D_MODEL))],
        out_specs=row(D_MODEL),
        out_shape=jax.ShapeDtypeStruct((n, D_MODEL), F32),
        compiler_params=_params(("parallel",)),
        name="mixout",
    )(x, a_out, bn, pre, post, w_in, w_in, wa, wb, wo)


def _mixout_sample_kernel(x_ref, a_ref, bn_ref, pre_ref, post_ref, wg0_ref, wg1_ref, wa_ref, wb_ref, wo_ref,
                          o_ref, wa16_ref, wb16_ref, wo16_ref):
    wa, wb, wo = wa_ref[...].astype(BF16), wb_ref[...].astype(BF16), wo_ref[...].astype(BF16)
    wa16_ref[...] = wa
    wb16_ref[...] = wb
    wo16_ref[...] = wo
    o_ref[...] = _mix(x_ref[...], a_ref[...], bn_ref[...], pre_ref[...], post_ref[...],
                      wg0_ref[...], wg1_ref[...], wa, wb, wo)


def _mixout_sample(x, a_out, bn, layer, pre, post, w_in16, wa, wb, wo):
    n = x.shape[0]
    full = lambda *shape: pl.BlockSpec(shape, lambda i: (0,) * len(shape))
    shapes = ((A_WIDTH, D_MODEL), (B_WIDTH, D_MODEL), (D_MODEL, D_MODEL))
    return pl.pallas_call(
        _mixout_sample_kernel,
        grid=(1,),
        in_specs=[full(n, D_MODEL), full(n, A_WIDTH), full(n, B_WIDTH), _vec(layer), _vec(layer),
                  _col_block(3), _col_block(4)] + [_layer_block(s, layer) for s in shapes],
        out_specs=[full(n, D_MODEL)] + [full(*s) for s in shapes],
        out_shape=[jax.ShapeDtypeStruct((n, D_MODEL), F32)] + [jax.ShapeDtypeStruct(s, BF16) for s in shapes],
        compiler_params=_params(("arbitrary",)),
        name="mixout_sample",
    )(x, a_out, bn, pre, post, w_in16, w_in16, wa, wb, wo)


def _rope_tables(pos):
    half = ROPE_DIM // 2
    inv = ROPE_THETA ** (-jnp.arange(half, dtype=F32) / half)
    dim = jnp.arange(LANES) % A_HEAD_DIM
    ang = pos.astype(F32)[:, None] * inv[dim % half][None, :]
    cos, sin = jnp.cos(ang), jnp.sin(ang)
    lo, hi = (dim < half)[None, :], jnp.logical_and(dim >= half, dim < ROPE_DIM)[None, :]
    return (jnp.where(jnp.logical_or(lo, hi), cos, 1.0), jnp.where(lo, -sin, 0.0), jnp.where(hi, sin, 0.0))


def kernel(x_prompt, x_sample, cache_k, cache_v, state_hgrn, ffn1_norm_pre, ffn1_norm_post, ffn1_w_gate, ffn1_w_up, ffn1_w_down, mix_norm_pre, mix_norm_post, w_in, hgrn_lb_logits, hgrn_out_norm, w_a_out, w_b_out, w_mix_out, ffn2_norm_pre, ffn2_norm_post, ffn2_w_gate, ffn2_w_up, ffn2_w_down):
    batch, s_len, _ = x_prompt.shape
    n_dec, t_dec, _ = x_sample.shape
    assert batch == 1 and t_dec == 1 and s_len % ATTN_TILE == 0
    tm = 512
    tm_wide = 1024
    th = min(4096, s_len)
    cache_kt = jnp.transpose(cache_k, (0, 1, 3, 4, 2))
    cache_vt = jnp.transpose(cache_v, (0, 1, 3, 4, 2))
    yp = x_prompt.reshape(s_len, D_MODEL)
    ys = x_sample.reshape(n_dec, D_MODEL)
    tabs_p = _rope_tables(jnp.arange(s_len))
    tabs_s = _rope_tables(jnp.full((n_dec,), PAST_LEN))
    lbl = hgrn_lb_logits.astype(F32)
    vecs = lambda a: a.reshape(DEPTH, 1, -1).astype(F32)
    w_len = min(A_MAX_WINDOW, s_len)
    f1_pre, f1_post, f2_pre, f2_post = (vecs(a) for a in (ffn1_norm_pre, ffn1_norm_post, ffn2_norm_pre, ffn2_norm_post))
    m_pre, m_post, nrm = vecs(mix_norm_pre), vecs(mix_norm_post), vecs(hgrn_out_norm)
    kp, vp, sp, ksn, vsn = [], [], [], [], []
    states = None
    for l in range(DEPTH):
        ys, *f1 = _ffn_sample(ys, l, f1_pre, f1_post, ffn1_w_gate, ffn1_w_up, ffn1_w_down)
        qa, ka, va, sq, sk, sv, sg, win = _inproj_sample(ys, m_pre, tabs_s, lbl, w_in, l)
        yp = _ffn(yp, l, f1_pre, f1_post, *f1, tm_wide)
        q1, q4, q16, k1, k4, k16, v1, v4, v16, kf, vf = _inproj_attn(yp, m_pre, tabs_p, win, l, tm)
        hq, hk, hv, hg = _inproj_rec(yp, m_pre, lbl, win, l, tm)
        a_out = _attn_prompt(q1, q4, q16, k1, k4, k16, v1, v4, v16)
        bn, st_p, a_s, bn_s, states = _recurrent(l, hq, hk, hv, hg, nrm, qa, ka, va, cache_kt, cache_vt,
                                                 sq, sk, sv, sg, state_hgrn, states, th)
        ys, *mo = _mixout_sample(ys, a_s, bn_s, l, m_pre, m_post, win, w_a_out, w_b_out, w_mix_out)
        ys, *f2 = _ffn_sample(ys, l, f2_pre, f2_post, ffn2_w_gate, ffn2_w_up, ffn2_w_down)
        yp = _mixout(yp, a_out, bn, l, m_pre, m_post, win, *mo, tm_wide)
        yp = _ffn(yp, l, f2_pre, f2_post, *f2, tm_wide)
        kp.append(kf.reshape(batch, w_len, A_HEADS, A_HEAD_DIM))
        vp.append(vf.reshape(batch, w_len, A_HEADS, A_HEAD_DIM))
        sp.append(st_p.reshape(batch, B_HEADS, B_HEAD_DIM, B_HEAD_DIM))
        ksn.append(ka.reshape(n_dec, t_dec, A_HEADS, A_HEAD_DIM))
        vsn.append(va.reshape(n_dec, t_dec, A_HEADS, A_HEAD_DIM))
    return (yp.reshape(batch, s_len, D_MODEL), ys.reshape(n_dec, t_dec, D_MODEL),
            jnp.stack(kp), jnp.stack(vp), jnp.stack(sp), jnp.stack(ksn), jnp.stack(vsn), states)
```

```python
import functools

import jax
import jax.numpy as jnp
from jax import lax
from jax.experimental import pallas as pl
from jax.experimental.pallas import tpu as pltpu

F32 = jnp.float32
BF16 = jnp.bfloat16

D_MODEL = 1024
DEPTH = 2
PAST_LEN = 16384
A_HEADS = 8
A_HEAD_DIM = 64
A_WIDTH = A_HEADS * A_HEAD_DIM
A_DILATIONS = (1, 4, 16)
A_SPAN = 128
A_MAX_WINDOW = 2048
A_SCALE = A_HEAD_DIM ** -0.5
ROPE_THETA = 500000.0
ROPE_DIM = A_HEAD_DIM // 4
B_HEADS = 8
B_HEAD_DIM = 128
B_WIDTH = B_HEADS * B_HEAD_DIM
B_CHUNK = 64
B_UNROLL = 16
B_SCALE = B_HEAD_DIM ** -0.5
D_FF = 2816
FF_CHUNK = 256
ROW_PARTS = 4
EPS = 1e-6
LOG2_E = 1.4426950408889634

W_BLOCK = 3 * A_WIDTH
assert 3 * B_WIDTH == 2 * W_BLOCK and B_WIDTH + 2 * D_MODEL == 2 * W_BLOCK
LANES = 128
N_PAIRS = A_WIDTH // LANES
ATTN_TILE = A_SPAN * max(A_DILATIONS)
A_UNROLL = 16
VMEM_LIMIT = 56 * 1024 * 1024


def _params(sem, vmem=VMEM_LIMIT):
    return pltpu.CompilerParams(dimension_semantics=sem, vmem_limit_bytes=vmem)


def _resident(shape):
    nd = len(shape)
    return pl.BlockSpec(shape, lambda *_: (0,) * nd, pipeline_mode=pl.Buffered(1))


def _layer_block(shape, layer, col=0):
    index = (layer,) + (0,) * (len(shape) - 1) + (col,)
    return pl.BlockSpec((None,) + tuple(shape), lambda *_: index, pipeline_mode=pl.Buffered(1))


def _vec(layer):
    return _layer_block((1, D_MODEL), layer)


def _rms(x, g):
    y = x * lax.rsqrt(jnp.mean(x * x, axis=-1, keepdims=True) + EPS)
    return y * g


def _dot(a, b):
    return jnp.dot(a, b, preferred_element_type=F32)


def _dot_nt(a, b):
    return lax.dot_general(a, b, (((1,), (1,)), ((), ())), preferred_element_type=F32)


def _dot_tn(a, b):
    return lax.dot_general(a, b, (((0,), (0,)), ((), ())), preferred_element_type=F32)


def _row_parts(rows):
    parts = ROW_PARTS if rows % (8 * ROW_PARTS) == 0 else 1
    return [pl.ds(p * (rows // parts), rows // parts) for p in range(parts)]


def _ffn_kernel(x_ref, pre_ref, post_ref, wg_ref, wu_ref, wd_ref, o_ref):
    for sl in _row_parts(x_ref.shape[0]):
        x = x_ref[sl, :]
        h = _rms(x, pre_ref[...]).astype(BF16)
        g = _dot(h, wg_ref[...])
        u = _dot(h, wu_ref[...])
        a = (g * jax.nn.sigmoid(g) * u).astype(BF16)
        y = _dot(a, wd_ref[...])
        o_ref[sl, :] = x + 0.5 * _rms(y, post_ref[...])


def _ffn(x, layer, pre, post, wg, wu, wd, tm):
    n = x.shape[0]
    row = pl.BlockSpec((tm, D_MODEL), lambda i: (i, 0))
    return pl.pallas_call(
        _ffn_kernel,
        grid=(n // tm,),
        in_specs=[row, _vec(layer), _vec(layer), _resident((D_MODEL, D_FF)),
                  _resident((D_MODEL, D_FF)), _resident((D_FF, D_MODEL))],
        out_specs=row,
        out_shape=jax.ShapeDtypeStruct((n, D_MODEL), F32),
        compiler_params=_params(("parallel",)),
        name="ffn",
    )(x, pre, post, wg, wu, wd)


def _ffn_sample_kernel(x_ref, pre_ref, post_ref, wg_ref, wu_ref, wd_ref,
                       o_ref, wg16_ref, wu16_ref, wd16_ref, h_scr, acc_scr):
    c = pl.program_id(0)

    @pl.when(c == 0)
    def _():
        h_scr[...] = _rms(x_ref[...], pre_ref[...]).astype(BF16)
        acc_scr[...] = jnp.zeros_like(acc_scr)

    wg, wu, wd = wg_ref[...].astype(BF16), wu_ref[...].astype(BF16), wd_ref[...].astype(BF16)
    wg16_ref[...] = wg
    wu16_ref[...] = wu
    wd16_ref[...] = wd
    h = h_scr[...]
    g = _dot(h, wg)
    u = _dot(h, wu)
    acc_scr[...] += _dot((g * jax.nn.sigmoid(g) * u).astype(BF16), wd)

    @pl.when(c == pl.num_programs(0) - 1)
    def _():
        o_ref[...] = x_ref[...] + 0.5 * _rms(acc_scr[...], post_ref[...])


def _ffn_sample(x, layer, pre, post, wg, wu, wd):
    n = x.shape[0]
    full = pl.BlockSpec((n, D_MODEL), lambda c: (0, 0))
    col = lambda: pl.BlockSpec((None, D_MODEL, FF_CHUNK), lambda c: (layer, 0, c))
    col16 = lambda: pl.BlockSpec((D_MODEL, FF_CHUNK), lambda c: (0, c))
    return pl.pallas_call(
        _ffn_sample_kernel,
        grid=(D_FF // FF_CHUNK,),
        in_specs=[full, _vec(layer), _vec(layer), col(), col(),
                  pl.BlockSpec((None, FF_CHUNK, D_MODEL), lambda c: (layer, c, 0))],
        out_specs=[full, col16(), col16(), pl.BlockSpec((FF_CHUNK, D_MODEL), lambda c: (c, 0))],
        out_shape=[jax.ShapeDtypeStruct((n, D_MODEL), F32), jax.ShapeDtypeStruct((D_MODEL, D_FF), BF16),
                   jax.ShapeDtypeStruct((D_MODEL, D_FF), BF16), jax.ShapeDtypeStruct((D_FF, D_MODEL), BF16)],
        scratch_shapes=[pltpu.VMEM((n, D_MODEL), BF16), pltpu.VMEM((n, D_MODEL), F32)],
        compiler_params=_params(("arbitrary",)),
        name="ffn_sample",
    )(x, pre, post, wg, wu, wd)


def _lower_bound(logits, layer):
    e = jnp.exp(logits - jnp.max(logits, axis=0, keepdims=True))
    sm = e / jnp.sum(e, axis=0, keepdims=True)
    lb = jnp.zeros((1, B_WIDTH), F32)
    for i in range(1, layer + 1):
        lb = lb + sm[i:i + 1, :]
    return lb


def _rope_slab(t, cos, sin_lo, sin_hi):
    return t * cos + pltpu.roll(t, LANES - ROPE_DIM // 2, axis=1) * sin_lo + pltpu.roll(t, ROPE_DIM // 2, axis=1) * sin_hi


def _col_block(col):
    return pl.BlockSpec((D_MODEL, W_BLOCK), lambda *_: (0, col), pipeline_mode=pl.Buffered(1))


def _hgrn_gates(z0, z1, lb):
    split = W_BLOCK - B_WIDTH
    q = z0[:, :B_WIDTH] * B_SCALE
    f_raw = jnp.concatenate([z0[:, B_WIDTH:], z1[:, :split]], axis=1)
    i_raw = z1[:, split:]
    f = lb + (1.0 - lb) * jax.nn.sigmoid(f_raw)
    return q, 1.0 - f, i_raw * jax.nn.sigmoid(i_raw), jnp.log(f)


def _inproj_attn_kernel(tm, x_ref, pre_ref, cos_ref, slo_ref, shi_ref, wa_ref,
                        q1_ref, q4_ref, q16_ref, k1_ref, k4_ref, k16_ref, v1_ref, v4_ref, v16_ref,
                        kf_ref, vf_ref, stage_ref, stage4_ref):
    h = _rms(x_ref[...], pre_ref[...]).astype(BF16)
    za = _dot(h, wa_ref[...])
    cos, slo, shi = cos_ref[...], slo_ref[...], shi_ref[...]
    outs = ((q1_ref, q4_ref, q16_ref), (k1_ref, k4_ref, k16_ref), (v1_ref, v4_ref, v16_ref))
    for s in range(3 * N_PAIRS):
        kind, hp = divmod(s, N_PAIRS)
        t = za[:, s * LANES:(s + 1) * LANES]
        if kind < 2:
            t = _rope_slab(t, cos, slo, shi)
        if kind == 0:
            t = t * (A_SCALE * LOG2_E)
        if kind == 1:
            kf_ref[:, hp * LANES:(hp + 1) * LANES] = t
        if kind == 2:
            vf_ref[:, hp * LANES:(hp + 1) * LANES] = t
        stage_ref[s] = t
        o1, o4, o16 = outs[kind]
        o1[hp] = t.astype(BF16)
        for r in range(4):
            c4 = stage_ref[s, pl.ds(r, tm // 4, stride=4), :]
            o4[hp, r] = c4.astype(BF16)
            stage4_ref[s, r] = c4
        for r in range(4):
            for a in range(4):
                o16[hp, r + 4 * a] = stage4_ref[s, r, pl.ds(a, tm // 16, stride=4), :].astype(BF16)


def _inproj_attn(x, pre, tabs, w_in, layer, tm):
    s_len = x.shape[0]
    w_len = min(A_MAX_WINDOW, s_len)
    first = (s_len - w_len) // tm
    row = pl.BlockSpec((tm, D_MODEL), lambda i: (i, 0))
    tab = pl.BlockSpec((tm, LANES), lambda i: (i, 0))
    l1 = pl.BlockSpec((N_PAIRS, tm, LANES), lambda i: (0, i, 0))
    l4 = pl.BlockSpec((N_PAIRS, 4, tm // 4, LANES), lambda i: (0, 0, i, 0))
    l16 = pl.BlockSpec((N_PAIRS, 16, tm // 16, LANES), lambda i: (0, 0, i, 0))
    win = pl.BlockSpec((tm, A_WIDTH), lambda i: (jnp.maximum(i - first, 0), 0))
    s1 = jax.ShapeDtypeStruct((N_PAIRS, s_len, LANES), BF16)
    s4 = jax.ShapeDtypeStruct((N_PAIRS, 4, s_len // 4, LANES), BF16)
    s16 = jax.ShapeDtypeStruct((N_PAIRS, 16, s_len // 16, LANES), BF16)
    sw = jax.ShapeDtypeStruct((w_len, A_WIDTH), F32)
    return pl.pallas_call(
        functools.partial(_inproj_attn_kernel, tm),
        grid=(s_len // tm,),
        in_specs=[row, _vec(layer), tab, tab, tab, _col_block(0)],
        out_specs=[l1, l4, l16, l1, l4, l16, l1, l4, l16, win, win],
        out_shape=[s1, s4, s16, s1, s4, s16, s1, s4, s16, sw, sw],
        scratch_shapes=[pltpu.VMEM((3 * N_PAIRS, tm, LANES), F32), pltpu.VMEM((3 * N_PAIRS, 4, tm // 4, LANES), F32)],
        compiler_params=_params(("arbitrary",)),
        name="inproj_attn",
    )(x, pre, *tabs, w_in)


def _inproj_rec_kernel(layer, x_ref, pre_ref, lbl_ref, wb0_ref, wb1_ref, hq_ref, hk_ref, hv_ref, hg_ref):
    lb = _lower_bound(lbl_ref[...], layer)
    for rows in _row_parts(x_ref.shape[0]):
        h = _rms(x_ref[rows, :], pre_ref[...]).astype(BF16)
        q, k, v, g = _hgrn_gates(_dot(h, wb0_ref[...]), _dot(h, wb1_ref[...]), lb)
        for hd in range(B_HEADS):
            sl = slice(hd * B_HEAD_DIM, (hd + 1) * B_HEAD_DIM)
            hq_ref[hd, rows, :] = q[:, sl]
            hk_ref[hd, rows, :] = k[:, sl]
            hv_ref[hd, rows, :] = v[:, sl]
            hg_ref[hd, rows, :] = g[:, sl]


def _inproj_rec(x, pre, lbl, w_in, layer, tm):
    s_len = x.shape[0]
    row = pl.BlockSpec((tm, D_MODEL), lambda i: (i, 0))
    hd = pl.BlockSpec((B_HEADS, tm, B_HEAD_DIM), lambda i: (0, i, 0))
    sh = jax.ShapeDtypeStruct((B_HEADS, s_len, B_HEAD_DIM), F32)
    return pl.pallas_call(
        functools.partial(_inproj_rec_kernel, layer),
        grid=(s_len // tm,),
        in_specs=[row, _vec(layer), _resident((DEPTH, B_WIDTH)), _col_block(1), _col_block(2)],
        out_specs=[hd] * 4,
        out_shape=[sh] * 4,
        compiler_params=_params(("parallel",)),
        name="inproj_rec",
    )(x, pre, lbl, w_in, w_in)


def _inproj_sample_kernel(layer, x_ref, pre_ref, cos_ref, slo_ref, shi_ref, lbl_ref, w_ref,
                          qa_ref, ka_ref, va_ref, hq_ref, hk_ref, hv_ref, hg_ref, w16_ref, h_scr, z_scr):
    c = pl.program_id(0)

    @pl.when(c == 0)
    def _():
        h_scr[...] = _rms(x_ref[...], pre_ref[...]).astype(BF16)

    w = w_ref[...].astype(BF16)
    w16_ref[...] = w

    @pl.when(c == 0)
    def _():
        za = _dot(h_scr[...], w)
        cos, slo, shi = cos_ref[...], slo_ref[...], shi_ref[...]
        outs = (qa_ref, ka_ref, va_ref)
        for s in range(3 * N_PAIRS):
            kind, hp = divmod(s, N_PAIRS)
            t = za[:, s * LANES:(s + 1) * LANES]
            if kind < 2:
                t = _rope_slab(t, cos, slo, shi)
            if kind == 0:
                t = t * A_SCALE
            outs[kind][:, hp * LANES:(hp + 1) * LANES] = t

    @pl.when(c == 1)
    def _():
        z_scr[...] = _dot(h_scr[...], w)

    @pl.when(c == 2)
    def _():
        q, k, v, g = _hgrn_gates(z_scr[...], _dot(h_scr[...], w), _lower_bound(lbl_ref[...], layer))
        hq_ref[...] = q
        hk_ref[...] = k
        hv_ref[...] = v
        hg_ref[...] = g


def _inproj_sample(x, pre, tabs, lbl, w_in, layer):
    n = x.shape[0]
    n_col = w_in.shape[-1] // W_BLOCK
    full = lambda *shape: pl.BlockSpec(shape, lambda c: (0,) * len(shape))
    sa = jax.ShapeDtypeStruct((n, A_WIDTH), F32)
    sb = jax.ShapeDtypeStruct((n, B_WIDTH), F32)
    return pl.pallas_call(
        functools.partial(_inproj_sample_kernel, layer),
        grid=(n_col,),
        in_specs=[full(n, D_MODEL), _vec(layer), full(n, LANES), full(n, LANES), full(n, LANES),
                  full(DEPTH, B_WIDTH), pl.BlockSpec((None, D_MODEL, W_BLOCK), lambda c: (layer, 0, c))],
        out_specs=[full(n, A_WIDTH)] * 3 + [full(n, B_WIDTH)] * 4
                  + [pl.BlockSpec((D_MODEL, W_BLOCK), lambda c: (0, c))],
        out_shape=[sa] * 3 + [sb] * 4 + [jax.ShapeDtypeStruct((D_MODEL, n_col * W_BLOCK), BF16)],
        scratch_shapes=[pltpu.VMEM((n, D_MODEL), BF16), pltpu.VMEM((n, W_BLOCK), F32)],
        compiler_params=_params(("arbitrary",)),
        name="inproj_sample",
    )(x, pre, *tabs, lbl, w_in)


def _attn_scores(q, k, bias):
    first = lax.broadcasted_iota(jnp.int32, (A_SPAN, LANES), 1) < A_HEAD_DIM
    zero = jnp.zeros_like(q)
    q2 = jnp.concatenate([jnp.where(first, q, zero), jnp.where(first, zero, q)], axis=0)
    s = _dot_nt(q2, k) + bias
    m = jnp.max(jnp.maximum(s[:, :A_SPAN], s[:, A_SPAN:]), axis=-1, keepdims=True)
    return jnp.exp2(s - m).astype(BF16), jnp.broadcast_to(m, (2 * A_SPAN, LANES))


def _attn_values(p, m, v_ext):
    first = lax.broadcasted_iota(jnp.int32, (A_SPAN, LANES), 1) < A_HEAD_DIM
    oe = _dot(p, v_ext)
    pick = lambda t: jnp.where(first, t[:A_SPAN], t[A_SPAN:])
    return pick(oe[:, :LANES]), pick(m), pick(oe[:, LANES:])


def _attn_prompt_kernel(q1_ref, q4_ref, q16_ref, k1_ref, k4_ref, k16_ref, v1_ref, v4_ref, v16_ref,
                        o_ref, ks1, ks4, ks16, vs1, vs4, vs16, bias_scr, o_scr, m_scr, d_scr, p_scr, mx_scr):
    j = pl.program_id(1)
    q_refs = (q1_ref, q4_ref, q16_ref)
    k_in = (k1_ref, k4_ref, k16_ref)
    v_in = (v1_ref, v4_ref, v16_ref)
    k_scr = (ks1, ks4, ks16)
    v_scr = (vs1, vs4, vs16)

    @pl.when(j == 0)
    def _():
        for scr in k_scr:
            scr[:, 0:A_SPAN, :] = jnp.zeros((scr.shape[0], A_SPAN, LANES), BF16)
        for scr in v_scr:
            scr[:, 0:A_SPAN, 0:LANES] = jnp.zeros((scr.shape[0], A_SPAN, LANES), BF16)
            scr[:, :, LANES:] = jnp.ones((scr.shape[0], scr.shape[1], LANES), BF16)
        qi = lax.broadcasted_iota(jnp.int32, (2 * A_SPAN, 2 * A_SPAN), 0) % A_SPAN
        ki = lax.broadcasted_iota(jnp.int32, (2 * A_SPAN, 2 * A_SPAN), 1)
        band = jnp.logical_and(ki >= qi, ki <= qi + A_SPAN)
        bias_scr[1] = jnp.where(band, 0.0, -jnp.inf)
        bias_scr[0] = jnp.where(jnp.logical_and(band, ki >= A_SPAN), 0.0, -jnp.inf)

    zero = jnp.minimum(j, 0)

    def stage_tile(_, carry):
        for p, d in enumerate(A_DILATIONS):
            k_scr[p][:, A_SPAN:, :] = k_in[p][...]
            v_scr[p][:, A_SPAN:, 0:LANES] = v_in[p][...]
        return carry

    lax.fori_loop(0, 1 + zero, stage_tile, 0)

    def locate(p, grp, u):
        per_class = ATTN_TILE // A_DILATIONS[p] // A_SPAN
        b = grp * A_UNROLL + u
        n = b % per_class
        return b // per_class, pl.multiple_of(n * A_SPAN, A_SPAN), n

    def front(p, grp):
        for u in range(A_UNROLL):
            r, lo, n = locate(p, grp, u)
            has_prev = jnp.logical_or(j > 0, n > 0).astype(jnp.int32)
            p_scr[u], mx_scr[u] = _attn_scores(q_refs[p][r, pl.ds(lo, A_SPAN), :],
                                               k_scr[p][r, pl.ds(lo, 2 * A_SPAN), :], bias_scr[has_prev])

    def back(p, grp):
        d = A_DILATIONS[p]
        for u in range(A_UNROLL):
            r, lo, n = locate(p, grp, u)
            acc, m, den = _attn_values(p_scr[u], mx_scr[u], v_scr[p][r, pl.ds(lo, 2 * A_SPAN), :])
            if d > 1:
                rows = pl.ds(n * (A_SPAN * d) + r, A_SPAN, stride=d)
                o_scr[p - 1, rows, :] = acc
                m_scr[p - 1, rows, :] = m
                d_scr[p - 1, rows, :] = den
            else:
                rows = pl.ds(lo, A_SPAN)
                m1, m2 = m_scr[0, rows, :], m_scr[1, rows, :]
                top = jnp.maximum(jnp.maximum(m, m1), m2)
                w0, w1, w2 = jnp.exp2(m - top), jnp.exp2(m1 - top), jnp.exp2(m2 - top)
                num = w0 * acc + w1 * o_scr[0, rows, :] + w2 * o_scr[1, rows, :]
                tot = w0 * den + w1 * d_scr[0, rows, :] + w2 * d_scr[1, rows, :]
                o_ref[rows, :] = (num / tot).astype(BF16)

    n_grp = ATTN_TILE // A_SPAN // A_UNROLL
    order = tuple(range(1, len(A_DILATIONS))) + (0,)
    front(order[0], 0)
    for idx, p in enumerate(order):
        if n_grp > 1:

            def steady(grp, carry, p=p):
                back(p, grp - 1)
                front(p, grp)
                return carry

            lax.fori_loop(1, n_grp + zero, steady, 0)
        if idx + 1 < len(order):
            back(p, n_grp - 1)
            front(order[idx + 1], 0)

    def key_tails(_, carry):
        for p, d in enumerate(A_DILATIONS):
            tail = ATTN_TILE // d
            k_scr[p][:, 0:A_SPAN, :] = k_scr[p][:, tail:tail + A_SPAN, :]
        return carry

    lax.fori_loop(0, 1 + zero, key_tails, 0)
    back(order[-1], n_grp - 1)

    def value_tails(_, carry):
        for p, d in enumerate(A_DILATIONS):
            tail = ATTN_TILE // d
            v_scr[p][:, 0:A_SPAN, 0:LANES] = v_scr[p][:, tail:tail + A_SPAN, 0:LANES]
        return carry

    lax.fori_loop(0, 1 + zero, value_tails, 0)


def _attn_prompt(q1, q4, q16, k1, k4, k16, v1, v4, v16):
    s_len = q1.shape[1]
    t = ATTN_TILE
    b1 = pl.BlockSpec((None, 1, t, LANES), lambda hp, j: (hp, 0, j, 0))
    b4 = pl.BlockSpec((None, 4, t // 4, LANES), lambda hp, j: (hp, 0, j, 0))
    b16 = pl.BlockSpec((None, 16, t // 16, LANES), lambda hp, j: (hp, 0, j, 0))
    as4 = lambda a: a.reshape(N_PAIRS, 1, s_len, LANES)
    scr = lambda d, width: pltpu.VMEM((d, A_SPAN + t // d, width), BF16)
    n_dil = len(A_DILATIONS) - 1
    return pl.pallas_call(
        _attn_prompt_kernel,
        grid=(N_PAIRS, s_len // t),
        in_specs=[b1, b4, b16] * 3,
        out_specs=pl.BlockSpec((t, LANES), lambda hp, j: (j, hp)),
        out_shape=jax.ShapeDtypeStruct((s_len, A_WIDTH), BF16),
        scratch_shapes=[scr(d, LANES) for d in A_DILATIONS] + [scr(d, 2 * LANES) for d in A_DILATIONS]
                       + [pltpu.VMEM((2, 2 * A_SPAN, 2 * A_SPAN), F32)] + [pltpu.VMEM((n_dil, t, LANES), F32)] * 3
                       + [pltpu.VMEM((A_UNROLL, 2 * A_SPAN, 2 * A_SPAN), BF16),
                          pltpu.VMEM((A_UNROLL, 2 * A_SPAN, LANES), F32)],
        compiler_params=_params(("arbitrary", "arbitrary")),
        name="attn_prompt",
    )(as4(q1), q4, q16, as4(k1), k4, k16, as4(v1), v4, v16)


def _decode_rows(bb, w_c, q_ref, k_ref, v_ref, kt_ref, vt_ref,
                 hq_ref, hk_ref, hv_ref, hg_ref, st_ref, nrm_ref, a_ref, bn_ref, so_ref):
    n_pat = len(A_DILATIONS)
    dist = w_c - lax.broadcasted_iota(jnp.int32, (A_HEADS, w_c), 1)
    cnt = jnp.zeros((A_HEADS, w_c), F32)
    for d in A_DILATIONS:
        cnt = cnt + jnp.logical_and(dist % d == 0, dist <= A_SPAN * d).astype(F32)
    used = cnt > 0.0
    head_s = lax.broadcasted_iota(jnp.int32, (A_HEADS, w_c), 0)
    head_o = lax.broadcasted_iota(jnp.int32, (A_HEADS, A_HEAD_DIM), 0)
    for b in range(bb):
        q = q_ref[b]
        k_new, v_new = k_ref[b], v_ref[b]
        q16 = q.astype(BF16)
        s_new = jnp.sum(q * k_new, axis=-1, keepdims=True)
        s = jnp.zeros((A_HEADS, w_c), F32)
        for h in range(A_HEADS):
            s = jnp.where(head_s == h, _dot(q16, kt_ref[b, h].astype(BF16)), s)
        m = jnp.maximum(jnp.max(jnp.where(used, s, -jnp.inf), axis=-1, keepdims=True), s_new)
        w = jnp.where(used, jnp.exp(s - m), 0.0) * cnt
        p_new = n_pat * jnp.exp(s_new - m)
        den = jnp.sum(w, axis=-1, keepdims=True) + p_new
        acc = p_new * v_new
        w16 = w.astype(BF16)
        for h in range(A_HEADS):
            acc = acc + jnp.where(head_o == h, _dot_nt(w16, vt_ref[b, h].astype(BF16)), 0.0)
        a_ref[b] = acc / den
        v_rows = hv_ref[b]
        packed = jnp.concatenate([hq_ref[b], hk_ref[b], jnp.exp(hg_ref[b]),
                                  jnp.zeros((LANES - 3 * B_HEADS, B_HEAD_DIM), F32)], axis=0)
        cols = packed.T
        outs = []
        for hd in range(B_HEADS):
            col = lambda i: cols[:, i * B_HEADS + hd:i * B_HEADS + hd + 1]
            st = col(2) * st_ref[b, hd] + col(1) * v_rows[hd:hd + 1, :]
            so_ref[b, hd] = st
            o = jnp.sum(col(0) * st, axis=0, keepdims=True)
            outs.append(_rms(o, nrm_ref[hd:hd + 1, :]))
        bn_ref[b] = jnp.concatenate(outs, axis=0)


N_DECODE_IN = 11


def _recurrent_kernel(th, bb, w_c, q_ref, k_ref, v_ref, g_ref, nrm_ref, *rest):
    dec_in = rest[:N_DECODE_IN]
    o_ref, st_ref, da_ref, dbn_ref, dso_ref, state_scr, b_scr, a_scr, part_scr = rest[-9:]
    j = pl.program_id(1)

    @pl.when(j == 0)
    def _():
        state_scr[...] = jnp.zeros_like(state_scr)

    piece = B_CHUNK * B_UNROLL
    rin = lax.broadcasted_iota(jnp.int32, (piece, B_HEAD_DIM), 0) & (B_CHUNK - 1)

    def scan_piece(i, carry):
        rows = pl.ds(pl.multiple_of(i * piece, piece), piece)
        b = g_ref[rows, :] * LOG2_E
        shift = 1
        while shift < B_CHUNK:
            b = b + jnp.where(rin >= shift, pltpu.roll(b, shift, axis=0), 0.0)
            shift *= 2
        b_scr[rows, :] = b
        return carry

    lax.fori_loop(0, th // piece, scan_piece, 0)

    row = lax.broadcasted_iota(jnp.int32, (B_CHUNK, B_HEAD_DIM), 0)
    a_t = lax.broadcasted_iota(jnp.int32, (B_CHUNK, B_CHUNK), 0)
    a_s = lax.broadcasted_iota(jnp.int32, (B_CHUNK, B_CHUNK), 1)
    levels = []
    half = B_CHUNK // 2
    while half >= 1:
        seg = 2 * half
        pair = jnp.logical_and(a_t // seg == a_s // seg,
                               jnp.logical_and(a_t % seg >= half, a_s % seg < half))
        upper = row % seg >= half
        levels.append((half, upper, jnp.where(upper, 1.0, -1.0), pair))
        half //= 2
    n_grp = B_CHUNK // 8
    sub8 = lax.broadcasted_iota(jnp.int32, (n_grp, 8, B_HEAD_DIM), 1)
    nrm = nrm_ref[...]

    def centre_value(bc, half):
        seg = 2 * half
        if seg >= 8:
            return jnp.concatenate(
                [jnp.broadcast_to(bc[s0 + half - 1:s0 + half, :], (seg, B_HEAD_DIM))
                 for s0 in range(0, B_CHUNK, seg)], axis=0)
        b3 = bc.reshape(n_grp, 8, B_HEAD_DIM)
        if half == 1:
            out = jnp.where(sub8 % 2 == 0, b3, pltpu.roll(b3, 1, axis=1))
        else:
            out = jnp.broadcast_to(b3[:, half - 1:half, :], b3.shape)
            for s0 in range(seg, 8, seg):
                out = jnp.where(sub8 >= s0, jnp.broadcast_to(b3[:, s0 + half - 1:s0 + half, :], b3.shape), out)
        return out.reshape(B_CHUNK, B_HEAD_DIM)

    def intra_chunk(q, k, v, bc):
        a = jnp.zeros((B_CHUNK, B_CHUNK), F32)
        for half, upper, sign, pair in levels:
            x = (jnp.where(upper, q, k) * jnp.exp2((bc - centre_value(bc, half)) * sign)).astype(BF16)
            a = jnp.where(pair, _dot_nt(x, x), a)
        return a, jnp.sum(q * k, axis=-1, keepdims=True) * v

    def chunk_rows(grp, u):
        return pl.ds(pl.multiple_of((grp * B_UNROLL + u) * B_CHUNK, B_CHUNK), B_CHUNK)

    def front(grp):
        st = state_scr[...]
        for u in range(B_UNROLL):
            rows = chunk_rows(grp, u)
            q, k, v, bc = q_ref[rows, :], k_ref[rows, :], v_ref[rows, :], b_scr[rows, :]
            o = _dot_nt((q * jnp.exp2(bc)).astype(BF16), st.astype(BF16))
            b_last = bc[B_CHUNK - 1:B_CHUNK, :]
            k_dec = (k * jnp.exp2(b_last - bc)).astype(BF16)
            st = st * jnp.exp2(b_last) + _dot_tn(v.astype(BF16), k_dec)
            a, o3 = intra_chunk(q, k, v, bc)
            a_scr[u] = a.astype(BF16)
            part_scr[u] = o + o3
        state_scr[...] = st

    def back(grp):
        for u in range(B_UNROLL):
            rows = chunk_rows(grp, u)
            o = part_scr[u] + _dot(a_scr[u], v_ref[rows, :].astype(BF16))
            o_ref[rows, :] = _rms(o, nrm)

    n_groups = th // (B_CHUNK * B_UNROLL)
    _decode_rows(bb, w_c, *dec_in, da_ref, dbn_ref, dso_ref)
    front(0)

    def steady(grp, carry):
        back(grp - 1)
        front(grp)
        return carry

    lax.fori_loop(1, n_groups, steady, 0)
    back(n_groups - 1)

    @pl.when(j == pl.num_programs(1) - 1)
    def _():
        st_ref[...] = state_scr[...].T


def _recurrent(layer, q, k, v, g, nrm, dq, dk, dv, cache_kt, cache_vt, dhq, dhk, dhv, dhg, state, states_out, th):
    s_len = q.shape[1]
    n = dq.shape[0]
    w_c = cache_kt.shape[-1]
    n_tiles = s_len // th
    steps = B_HEADS * n_tiles
    assert w_c >= A_SPAN * max(A_DILATIONS) and n % steps == 0
    bb = n // steps
    step = lambda h, j: h * n_tiles + j
    blk = pl.BlockSpec((None, th, B_HEAD_DIM), lambda h, j: (h, j, 0))
    a_spec = pl.BlockSpec((bb, A_HEADS, A_HEAD_DIM), lambda h, j: (step(h, j), 0, 0))
    headspec = pl.BlockSpec((bb, B_HEADS, B_HEAD_DIM), lambda h, j: (step(h, j), 0, 0))
    c_spec = pl.BlockSpec((None, bb, A_HEADS, A_HEAD_DIM, w_c), lambda h, j: (layer, step(h, j), 0, 0, 0))
    st_spec = pl.BlockSpec((None, bb, B_HEADS, B_HEAD_DIM, B_HEAD_DIM), lambda h, j: (layer, step(h, j), 0, 0, 0))
    a_heads = lambda a: a.reshape(n, A_HEADS, A_HEAD_DIM)
    heads = lambda a: a.reshape(n, B_HEADS, B_HEAD_DIM)
    in_specs = [blk] * 4 + [pl.BlockSpec((None, 1, B_HEAD_DIM), lambda h, j: (layer, 0, h))]
    in_specs += [a_spec] * 3 + [c_spec] * 2 + [headspec] * 4 + [st_spec, _layer_block((B_HEADS, B_HEAD_DIM), layer)]
    args = [q, k, v, g, nrm, a_heads(dq), a_heads(dk), a_heads(dv), cache_kt, cache_vt,
            heads(dhq), heads(dhk), heads(dhv), heads(dhg), state, nrm.reshape(DEPTH, B_HEADS, B_HEAD_DIM)]
    assert len(args) == 5 + N_DECODE_IN
    aliases = {}
    if states_out is not None:
        aliases = {len(args): 4}
        in_specs.append(pl.BlockSpec(memory_space=pl.ANY))
        args.append(states_out)
    bn, st_p, a_out, dbn, states = pl.pallas_call(
        functools.partial(_recurrent_kernel, th, bb, w_c),
        grid=(B_HEADS, n_tiles),
        in_specs=in_specs,
        out_specs=[pl.BlockSpec((th, B_HEAD_DIM), lambda h, j: (j, h)),
                   pl.BlockSpec((None, B_HEAD_DIM, B_HEAD_DIM), lambda h, j: (h, 0, 0)),
                   a_spec, headspec, st_spec],
        out_shape=[jax.ShapeDtypeStruct((s_len, B_WIDTH), F32),
                   jax.ShapeDtypeStruct((B_HEADS, B_HEAD_DIM, B_HEAD_DIM), F32),
                   jax.ShapeDtypeStruct((n, A_HEADS, A_HEAD_DIM), F32),
                   jax.ShapeDtypeStruct((n, B_HEADS, B_HEAD_DIM), F32),
                   jax.ShapeDtypeStruct(state.shape, F32)],
        scratch_shapes=[pltpu.VMEM((B_HEAD_DIM, B_HEAD_DIM), F32), pltpu.VMEM((th, B_HEAD_DIM), F32),
                        pltpu.VMEM((B_UNROLL, B_CHUNK, B_CHUNK), BF16),
                        pltpu.VMEM((B_UNROLL, B_CHUNK, B_HEAD_DIM), F32)],
        input_output_aliases=aliases,
        compiler_params=_params(("arbitrary", "arbitrary")),
        name="recurrent",
    )(*args)
    return bn, st_p, a_out.reshape(n, A_WIDTH).astype(BF16), dbn.reshape(n, B_WIDTH), states


def _mix(x, a16, bn, pre, post, wg0, wg1, wa, wb, wo):
    h = _rms(x, pre).astype(BF16)
    z0 = _dot(h, wg0)
    z1 = _dot(h, wg1)
    split = W_BLOCK - B_WIDTH
    g_b = z0[:, :B_WIDTH]
    gate_a = jnp.concatenate([z0[:, B_WIDTH:], z1[:, :split]], axis=1)
    gate_b = z1[:, split:]
    b_out = (bn * (g_b * jax.nn.sigmoid(g_b))).astype(BF16)
    mix = (jax.nn.sigmoid(gate_a) * _dot(a16, wa) + jax.nn.sigmoid(gate_b) * _dot(b_out, wb)).astype(BF16)
    return x + _rms(_dot(mix, wo), post)


def _mixout_kernel(x_ref, a_ref, bn_ref, pre_ref, post_ref, wg0_ref, wg1_ref, wa_ref, wb_ref, wo_ref, o_ref):
    for sl in _row_parts(x_ref.shape[0]):
        o_ref[sl, :] = _mix(x_ref[sl, :], a_ref[sl, :], bn_ref[sl, :], pre_ref[...], post_ref[...],
                            wg0_ref[...], wg1_ref[...], wa_ref[...], wb_ref[...], wo_ref[...])


def _mixout(x, a_out, bn, layer, pre, post, w_in, wa, wb, wo, tm):
    n = x.shape[0]
    row = lambda width: pl.BlockSpec((tm, width), lambda i: (i, 0))
    return pl.pallas_call(
        _mixout_kernel,
        grid=(n // tm,),
        in_specs=[row(D_MODEL), row(A_WIDTH), row(B_WIDTH), _vec(layer), _vec(layer), _col_block(3), _col_block(4),
                  _resident((A_WIDTH, D_MODEL)), _resident((B_WIDTH, D_MODEL)), _resident((D_MODEL, D_MODEL))],
        out_specs=row(D_MODEL),
        out_shape=jax.ShapeDtypeStruct((n, D_MODEL), F32),
        compiler_params=_params(("parallel",)),
        name="mixout",
    )(x, a_out, bn, pre, post, w_in, w_in, wa, wb, wo)


def _mixout_sample_kernel(x_ref, a_ref, bn_ref, pre_ref, post_ref, wg0_ref, wg1_ref, wa_ref, wb_ref, wo_ref,
                          o_ref, wa16_ref, wb16_ref, wo16_ref):
    wa, wb, wo = wa_ref[...].astype(BF16), wb_ref[...].astype(BF16), wo_ref[...].astype(BF16)
    wa16_ref[...] = wa
    wb16_ref[...] = wb
    wo16_ref[...] = wo
    o_ref[...] = _mix(x_ref[...], a_ref[...], bn_ref[...], pre_ref[...], post_ref[...],
                      wg0_ref[...], wg1_ref[...], wa, wb, wo)


def _mixout_sample(x, a_out, bn, layer, pre, post, w_in16, wa, wb, wo):
    n = x.shape[0]
    full = lambda *shape: pl.BlockSpec(shape, lambda i: (0,) * len(shape))
    shapes = ((A_WIDTH, D_MODEL), (B_WIDTH, D_MODEL), (D_MODEL, D_MODEL))
    return pl.pallas_call(
        _mixout_sample_kernel,
        grid=(1,),
        in_specs=[full(n, D_MODEL), full(n, A_WIDTH), full(n, B_WIDTH), _vec(layer), _vec(layer),
                  _col_block(3), _col_block(4)] + [_layer_block(s, layer) for s in shapes],
        out_specs=[full(n, D_MODEL)] + [full(*s) for s in shapes],
        out_shape=[jax.ShapeDtypeStruct((n, D_MODEL), F32)] + [jax.ShapeDtypeStruct(s, BF16) for s in shapes],
        compiler_params=_params(("arbitrary",)),
        name="mixout_sample",
    )(x, a_out, bn, pre, post, w_in16, w_in16, wa, wb, wo)


def _rope_tables(pos):
    half = ROPE_DIM // 2
    inv = ROPE_THETA ** (-jnp.arange(half, dtype=F32) / half)
    dim = jnp.arange(LANES) % A_HEAD_DIM
    ang = pos.astype(F32)[:, None] * inv[dim % half][None, :]
    cos, sin = jnp.cos(ang), jnp.sin(ang)
    lo, hi = (dim < half)[None, :], jnp.logical_and(dim >= half, dim < ROPE_DIM)[None, :]
    return (jnp.where(jnp.logical_or(lo, hi), cos, 1.0), jnp.where(lo, -sin, 0.0), jnp.where(hi, sin, 0.0))


def kernel(x_prompt, x_sample, cache_k, cache_v, state_hgrn, ffn1_norm_pre, ffn1_norm_post, ffn1_w_gate, ffn1_w_up, ffn1_w_down, mix_norm_pre, mix_norm_post, w_in, hgrn_lb_logits, hgrn_out_norm, w_a_out, w_b_out, w_mix_out, ffn2_norm_pre, ffn2_norm_post, ffn2_w_gate, ffn2_w_up, ffn2_w_down):
    batch, s_len, _ = x_prompt.shape
    n_dec, t_dec, _ = x_sample.shape
    assert batch == 1 and t_dec == 1 and s_len % ATTN_TILE == 0
    tm = 512
    tm_wide = 1024
    th = min(4096, s_len)
    cache_kt = jnp.transpose(cache_k, (0, 1, 3, 4, 2))
    cache_vt = jnp.transpose(cache_v, (0, 1, 3, 4, 2))
    yp = x_prompt.reshape(s_len, D_MODEL)
    ys = x_sample.reshape(n_dec, D_MODEL)
    tabs_p = _rope_tables(jnp.arange(s_len))
    tabs_s = _rope_tables(jnp.full((n_dec,), PAST_LEN))
    lbl = hgrn_lb_logits.astype(F32)
    vecs = lambda a: a.reshape(DEPTH, 1, -1).astype(F32)
    w_len = min(A_MAX_WINDOW, s_len)
    f1_pre, f1_post, f2_pre, f2_post = (vecs(a) for a in (ffn1_norm_pre, ffn1_norm_post, ffn2_norm_pre, ffn2_norm_post))
    m_pre, m_post, nrm = vecs(mix_norm_pre), vecs(mix_norm_post), vecs(hgrn_out_norm)
    kp, vp, sp, ksn, vsn = [], [], [], [], []
    states = None
    for l in range(DEPTH):
        ys, *f1 = _ffn_sample(ys, l, f1_pre, f1_post, ffn1_w_gate, ffn1_w_up, ffn1_w_down)
        qa, ka, va, sq, sk, sv, sg, win = _inproj_sample(ys, m_pre, tabs_s, lbl, w_in, l)
        yp = _ffn(yp, l, f1_pre, f1_post, *f1, tm_wide)
        q1, q4, q16, k1, k4, k16, v1, v4, v16, kf, vf = _inproj_attn(yp, m_pre, tabs_p, win, l, tm)
        hq, hk, hv, hg = _inproj_rec(yp, m_pre, lbl, win, l, tm)
        a_out = _attn_prompt(q1, q4, q16, k1, k4, k16, v1, v4, v16)
        bn, st_p, a_s, bn_s, states = _recurrent(l, hq, hk, hv, hg, nrm, qa, ka, va, cache_kt, cache_vt,
                                                 sq, sk, sv, sg, state_hgrn, states, th)
        ys, *mo = _mixout_sample(ys, a_s, bn_s, l, m_pre, m_post, win, w_a_out, w_b_out, w_mix_out)
        ys, *f2 = _ffn_sample(ys, l, f2_pre, f2_post, ffn2_w_gate, ffn2_w_up, ffn2_w_down)
        yp = _mixout(yp, a_out, bn, l, m_pre, m_post, win, *mo, tm_wide)
        yp = _ffn(yp, l, f2_pre, f2_post, *f2, tm_wide)
        kp.append(kf.reshape(batch, w_len, A_HEADS, A_HEAD_DIM))
        vp.append(vf.reshape(batch, w_len, A_HEADS, A_HEAD_DIM))
        sp.append(st_p.reshape(batch, B_HEADS, B_HEAD_DIM, B_HEAD_DIM))
        ksn.append(ka.reshape(n_dec, t_dec, A_HEADS, A_HEAD_DIM))
        vsn.append(va.reshape(n_dec, t_dec, A_HEADS, A_HEAD_DIM))
    return (yp.reshape(batch, s_len, D_MODEL), ys.reshape(n_dec, t_dec, D_MODEL),
            jnp.stack(kp), jnp.stack(vp), jnp.stack(sp), jnp.stack(ksn), jnp.stack(vsn), states)
```

```python
import functools

import jax
import jax.numpy as jnp
from jax import lax
from jax.experimental import pallas as pl
from jax.experimental.pallas import tpu as pltpu

F32 = jnp.float32
BF16 = jnp.bfloat16

D_MODEL = 1024
DEPTH = 2
PAST_LEN = 16384
A_HEADS = 8
A_HEAD_DIM = 64
A_WIDTH = A_HEADS * A_HEAD_DIM
A_DILATIONS = (1, 4, 16)
A_SPAN = 128
A_MAX_WINDOW = 2048
A_SCALE = A_HEAD_DIM ** -0.5
ROPE_THETA = 500000.0
ROPE_DIM = A_HEAD_DIM // 4
B_HEADS = 8
B_HEAD_DIM = 128
B_WIDTH = B_HEADS * B_HEAD_DIM
B_CHUNK = 64
B_UNROLL = 32
B_SCALE = B_HEAD_DIM ** -0.5
D_FF = 2816
FF_CHUNK = 256
ROW_PARTS = 4
EPS = 1e-6
LOG2_E = 1.4426950408889634

W_BLOCK = 3 * A_WIDTH
assert 3 * B_WIDTH == 2 * W_BLOCK and B_WIDTH + 2 * D_MODEL == 2 * W_BLOCK
LANES = 128
N_PAIRS = A_WIDTH // LANES
ATTN_TILE = A_SPAN * max(A_DILATIONS)
A_UNROLL = 16
VMEM_LIMIT = 56 * 1024 * 1024


def _params(sem, vmem=VMEM_LIMIT):
    return pltpu.CompilerParams(dimension_semantics=sem, vmem_limit_bytes=vmem)


def _resident(shape):
    nd = len(shape)
    return pl.BlockSpec(shape, lambda *_: (0,) * nd, pipeline_mode=pl.Buffered(1))


def _layer_block(shape, layer, col=0):
    index = (layer,) + (0,) * (len(shape) - 1) + (col,)
    return pl.BlockSpec((None,) + tuple(shape), lambda *_: index, pipeline_mode=pl.Buffered(1))


def _vec(layer):
    return _layer_block((1, D_MODEL), layer)


def _rms(x, g):
    y = x * lax.rsqrt(jnp.mean(x * x, axis=-1, keepdims=True) + EPS)
    return y * g


def _dot(a, b):
    return jnp.dot(a, b, preferred_element_type=F32)


def _dot_nt(a, b):
    return lax.dot_general(a, b, (((1,), (1,)), ((), ())), preferred_element_type=F32)


def _dot_tn(a, b):
    return lax.dot_general(a, b, (((0,), (0,)), ((), ())), preferred_element_type=F32)


def _row_parts(rows):
    parts = ROW_PARTS if rows % (8 * ROW_PARTS) == 0 else 1
    return [pl.ds(p * (rows // parts), rows // parts) for p in range(parts)]


def _ffn_kernel(x_ref, pre_ref, post_ref, wg_ref, wu_ref, wd_ref, o_ref):
    for sl in _row_parts(x_ref.shape[0]):
        x = x_ref[sl, :]
        h = _rms(x, pre_ref[...]).astype(BF16)
        g = _dot(h, wg_ref[...])
        u = _dot(h, wu_ref[...])
        a = (g * jax.nn.sigmoid(g) * u).astype(BF16)
        y = _dot(a, wd_ref[...])
        o_ref[sl, :] = x + 0.5 * _rms(y, post_ref[...])


def _ffn(x, layer, pre, post, wg, wu, wd, tm):
    n = x.shape[0]
    row = pl.BlockSpec((tm, D_MODEL), lambda i: (i, 0))
    return pl.pallas_call(
        _ffn_kernel,
        grid=(n // tm,),
        in_specs=[row, _vec(layer), _vec(layer), _resident((D_MODEL, D_FF)),
                  _resident((D_MODEL, D_FF)), _resident((D_FF, D_MODEL))],
        out_specs=row,
        out_shape=jax.ShapeDtypeStruct((n, D_MODEL), F32),
        compiler_params=_params(("parallel",)),
        name="ffn",
    )(x, pre, post, wg, wu, wd)


def _ffn_sample_kernel(x_ref, pre_ref, post_ref, wg_ref, wu_ref, wd_ref,
                       o_ref, wg16_ref, wu16_ref, wd16_ref, h_scr, acc_scr):
    c = pl.program_id(0)

    @pl.when(c == 0)
    def _():
        h_scr[...] = _rms(x_ref[...], pre_ref[...]).astype(BF16)
        acc_scr[...] = jnp.zeros_like(acc_scr)

    wg, wu, wd = wg_ref[...].astype(BF16), wu_ref[...].astype(BF16), wd_ref[...].astype(BF16)
    wg16_ref[...] = wg
    wu16_ref[...] = wu
    wd16_ref[...] = wd
    h = h_scr[...]
    g = _dot(h, wg)
    u = _dot(h, wu)
    acc_scr[...] += _dot((g * jax.nn.sigmoid(g) * u).astype(BF16), wd)

    @pl.when(c == pl.num_programs(0) - 1)
    def _():
        o_ref[...] = x_ref[...] + 0.5 * _rms(acc_scr[...], post_ref[...])


def _ffn_sample(x, layer, pre, post, wg, wu, wd):
    n = x.shape[0]
    full = pl.BlockSpec((n, D_MODEL), lambda c: (0, 0))
    col = lambda: pl.BlockSpec((None, D_MODEL, FF_CHUNK), lambda c: (layer, 0, c))
    col16 = lambda: pl.BlockSpec((D_MODEL, FF_CHUNK), lambda c: (0, c))
    return pl.pallas_call(
        _ffn_sample_kernel,
        grid=(D_FF // FF_CHUNK,),
        in_specs=[full, _vec(layer), _vec(layer), col(), col(),
                  pl.BlockSpec((None, FF_CHUNK, D_MODEL), lambda c: (layer, c, 0))],
        out_specs=[full, col16(), col16(), pl.BlockSpec((FF_CHUNK, D_MODEL), lambda c: (c, 0))],
        out_shape=[jax.ShapeDtypeStruct((n, D_MODEL), F32), jax.ShapeDtypeStruct((D_MODEL, D_FF), BF16),
                   jax.ShapeDtypeStruct((D_MODEL, D_FF), BF16), jax.ShapeDtypeStruct((D_FF, D_MODEL), BF16)],
        scratch_shapes=[pltpu.VMEM((n, D_MODEL), BF16), pltpu.VMEM((n, D_MODEL), F32)],
        compiler_params=_params(("arbitrary",)),
        name="ffn_sample",
    )(x, pre, post, wg, wu, wd)


def _lower_bound(logits, layer):
    e = jnp.exp(logits - jnp.max(logits, axis=0, keepdims=True))
    sm = e / jnp.sum(e, axis=0, keepdims=True)
    lb = jnp.zeros((1, B_WIDTH), F32)
    for i in range(1, layer + 1):
        lb = lb + sm[i:i + 1, :]
    return lb


def _rope_slab(t, cos, sin_lo, sin_hi):
    return t * cos + pltpu.roll(t, LANES - ROPE_DIM // 2, axis=1) * sin_lo + pltpu.roll(t, ROPE_DIM // 2, axis=1) * sin_hi


def _col_block(col):
    return pl.BlockSpec((D_MODEL, W_BLOCK), lambda *_: (0, col), pipeline_mode=pl.Buffered(1))


def _hgrn_gates(z0, z1, lb):
    split = W_BLOCK - B_WIDTH
    q = z0[:, :B_WIDTH] * B_SCALE
    f_raw = jnp.concatenate([z0[:, B_WIDTH:], z1[:, :split]], axis=1)
    i_raw = z1[:, split:]
    f = lb + (1.0 - lb) * jax.nn.sigmoid(f_raw)
    return q, 1.0 - f, i_raw * jax.nn.sigmoid(i_raw), jnp.log(f)


def _inproj_attn_kernel(tm, x_ref, pre_ref, cos_ref, slo_ref, shi_ref, wa_ref,
                        q1_ref, q4_ref, q16_ref, k1_ref, k4_ref, k16_ref, v1_ref, v4_ref, v16_ref,
                        kf_ref, vf_ref, stage_ref, stage4_ref):
    h = _rms(x_ref[...], pre_ref[...]).astype(BF16)
    za = _dot(h, wa_ref[...])
    cos, slo, shi = cos_ref[...], slo_ref[...], shi_ref[...]
    outs = ((q1_ref, q4_ref, q16_ref), (k1_ref, k4_ref, k16_ref), (v1_ref, v4_ref, v16_ref))
    for s in range(3 * N_PAIRS):
        kind, hp = divmod(s, N_PAIRS)
        t = za[:, s * LANES:(s + 1) * LANES]
        if kind < 2:
            t = _rope_slab(t, cos, slo, shi)
        if kind == 0:
            t = t * (A_SCALE * LOG2_E)
        if kind == 1:
            kf_ref[:, hp * LANES:(hp + 1) * LANES] = t
        if kind == 2:
            vf_ref[:, hp * LANES:(hp + 1) * LANES] = t
        stage_ref[s] = t
        o1, o4, o16 = outs[kind]
        o1[hp] = t.astype(BF16)
        for r in range(4):
            c4 = stage_ref[s, pl.ds(r, tm // 4, stride=4), :]
            o4[hp, r] = c4.astype(BF16)
            stage4_ref[s, r] = c4
        for r in range(4):
            for a in range(4):
                o16[hp, r + 4 * a] = stage4_ref[s, r, pl.ds(a, tm // 16, stride=4), :].astype(BF16)


def _inproj_attn(x, pre, tabs, w_in, layer, tm):
    s_len = x.shape[0]
    w_len = min(A_MAX_WINDOW, s_len)
    first = (s_len - w_len) // tm
    row = pl.BlockSpec((tm, D_MODEL), lambda i: (i, 0))
    tab = pl.BlockSpec((tm, LANES), lambda i: (i, 0))
    l1 = pl.BlockSpec((N_PAIRS, tm, LANES), lambda i: (0, i, 0))
    l4 = pl.BlockSpec((N_PAIRS, 4, tm // 4, LANES), lambda i: (0, 0, i, 0))
    l16 = pl.BlockSpec((N_PAIRS, 16, tm // 16, LANES), lambda i: (0, 0, i, 0))
    win = pl.BlockSpec((tm, A_WIDTH), lambda i: (jnp.maximum(i - first, 0), 0))
    s1 = jax.ShapeDtypeStruct((N_PAIRS, s_len, LANES), BF16)
    s4 = jax.ShapeDtypeStruct((N_PAIRS, 4, s_len // 4, LANES), BF16)
    s16 = jax.ShapeDtypeStruct((N_PAIRS, 16, s_len // 16, LANES), BF16)
    sw = jax.ShapeDtypeStruct((w_len, A_WIDTH), F32)
    return pl.pallas_call(
        functools.partial(_inproj_attn_kernel, tm),
        grid=(s_len // tm,),
        in_specs=[row, _vec(layer), tab, tab, tab, _col_block(0)],
        out_specs=[l1, l4, l16, l1, l4, l16, l1, l4, l16, win, win],
        out_shape=[s1, s4, s16, s1, s4, s16, s1, s4, s16, sw, sw],
        scratch_shapes=[pltpu.VMEM((3 * N_PAIRS, tm, LANES), F32), pltpu.VMEM((3 * N_PAIRS, 4, tm // 4, LANES), F32)],
        compiler_params=_params(("arbitrary",)),
        name="inproj_attn",
    )(x, pre, *tabs, w_in)


def _inproj_rec_kernel(layer, x_ref, pre_ref, lbl_ref, wb0_ref, wb1_ref, hq_ref, hk_ref, hv_ref, hg_ref):
    lb = _lower_bound(lbl_ref[...], layer)
    for rows in _row_parts(x_ref.shape[0]):
        h = _rms(x_ref[rows, :], pre_ref[...]).astype(BF16)
        q, k, v, g = _hgrn_gates(_dot(h, wb0_ref[...]), _dot(h, wb1_ref[...]), lb)
        for hd in range(B_HEADS):
            sl = slice(hd * B_HEAD_DIM, (hd + 1) * B_HEAD_DIM)
            hq_ref[hd, rows, :] = q[:, sl]
            hk_ref[hd, rows, :] = k[:, sl]
            hv_ref[hd, rows, :] = v[:, sl]
            hg_ref[hd, rows, :] = g[:, sl]


def _inproj_rec(x, pre, lbl, w_in, layer, tm):
    s_len = x.shape[0]
    row = pl.BlockSpec((tm, D_MODEL), lambda i: (i, 0))
    hd = pl.BlockSpec((B_HEADS, tm, B_HEAD_DIM), lambda i: (0, i, 0))
    sh = jax.ShapeDtypeStruct((B_HEADS, s_len, B_HEAD_DIM), F32)
    return pl.pallas_call(
        functools.partial(_inproj_rec_kernel, layer),
        grid=(s_len // tm,),
        in_specs=[row, _vec(layer), _resident((DEPTH, B_WIDTH)), _col_block(1), _col_block(2)],
        out_specs=[hd] * 4,
        out_shape=[sh] * 4,
        compiler_params=_params(("parallel",)),
        name="inproj_rec",
    )(x, pre, lbl, w_in, w_in)


def _inproj_sample_kernel(layer, x_ref, pre_ref, cos_ref, slo_ref, shi_ref, lbl_ref, w_ref,
                          qa_ref, ka_ref, va_ref, hq_ref, hk_ref, hv_ref, hg_ref, w16_ref, h_scr, z_scr):
    c = pl.program_id(0)

    @pl.when(c == 0)
    def _():
        h_scr[...] = _rms(x_ref[...], pre_ref[...]).astype(BF16)

    w = w_ref[...].astype(BF16)
    w16_ref[...] = w

    @pl.when(c == 0)
    def _():
        za = _dot(h_scr[...], w)
        cos, slo, shi = cos_ref[...], slo_ref[...], shi_ref[...]
        outs = (qa_ref, ka_ref, va_ref)
        for s in range(3 * N_PAIRS):
            kind, hp = divmod(s, N_PAIRS)
            t = za[:, s * LANES:(s + 1) * LANES]
            if kind < 2:
                t = _rope_slab(t, cos, slo, shi)
            if kind == 0:
                t = t * A_SCALE
            outs[kind][:, hp * LANES:(hp + 1) * LANES] = t

    @pl.when(c == 1)
    def _():
        z_scr[...] = _dot(h_scr[...], w)

    @pl.when(c == 2)
    def _():
        q, k, v, g = _hgrn_gates(z_scr[...], _dot(h_scr[...], w), _lower_bound(lbl_ref[...], layer))
        hq_ref[...] = q
        hk_ref[...] = k
        hv_ref[...] = v
        hg_ref[...] = g


def _inproj_sample(x, pre, tabs, lbl, w_in, layer):
    n = x.shape[0]
    n_col = w_in.shape[-1] // W_BLOCK
    full = lambda *shape: pl.BlockSpec(shape, lambda c: (0,) * len(shape))
    sa = jax.ShapeDtypeStruct((n, A_WIDTH), F32)
    sb = jax.ShapeDtypeStruct((n, B_WIDTH), F32)
    return pl.pallas_call(
        functools.partial(_inproj_sample_kernel, layer),
        grid=(n_col,),
        in_specs=[full(n, D_MODEL), _vec(layer), full(n, LANES), full(n, LANES), full(n, LANES),
                  full(DEPTH, B_WIDTH), pl.BlockSpec((None, D_MODEL, W_BLOCK), lambda c: (layer, 0, c))],
        out_specs=[full(n, A_WIDTH)] * 3 + [full(n, B_WIDTH)] * 4
                  + [pl.BlockSpec((D_MODEL, W_BLOCK), lambda c: (0, c))],
        out_shape=[sa] * 3 + [sb] * 4 + [jax.ShapeDtypeStruct((D_MODEL, n_col * W_BLOCK), BF16)],
        scratch_shapes=[pltpu.VMEM((n, D_MODEL), BF16), pltpu.VMEM((n, W_BLOCK), F32)],
        compiler_params=_params(("arbitrary",)),
        name="inproj_sample",
    )(x, pre, *tabs, lbl, w_in)


def _attn_scores(q, k, bias):
    first = lax.broadcasted_iota(jnp.int32, (A_SPAN, LANES), 1) < A_HEAD_DIM
    zero = jnp.zeros_like(q)
    q2 = jnp.concatenate([jnp.where(first, q, zero), jnp.where(first, zero, q)], axis=0)
    s = _dot_nt(q2, k) + bias
    m = jnp.max(jnp.maximum(s[:, :A_SPAN], s[:, A_SPAN:]), axis=-1, keepdims=True)
    return jnp.exp2(s - m).astype(BF16), jnp.broadcast_to(m, (2 * A_SPAN, LANES))


def _attn_values(p, m, v_ext):
    first = lax.broadcasted_iota(jnp.int32, (A_SPAN, LANES), 1) < A_HEAD_DIM
    oe = _dot(p, v_ext)
    pick = lambda t: jnp.where(first, t[:A_SPAN], t[A_SPAN:])
    return pick(oe[:, :LANES]), pick(m), pick(oe[:, LANES:])


def _attn_prompt_kernel(q1_ref, q4_ref, q16_ref, k1_ref, k4_ref, k16_ref, v1_ref, v4_ref, v16_ref,
                        o_ref, ks1, ks4, ks16, vs1, vs4, vs16, bias_scr, o_scr, m_scr, d_scr, p_scr, mx_scr):
    j = pl.program_id(1)
    q_refs = (q1_ref, q4_ref, q16_ref)
    k_in = (k1_ref, k4_ref, k16_ref)
    v_in = (v1_ref, v4_ref, v16_ref)
    k_scr = (ks1, ks4, ks16)
    v_scr = (vs1, vs4, vs16)

    @pl.when(j == 0)
    def _():
        for scr in k_scr:
            scr[:, 0:A_SPAN, :] = jnp.zeros((scr.shape[0], A_SPAN, LANES), BF16)
        for scr in v_scr:
            scr[:, 0:A_SPAN, 0:LANES] = jnp.zeros((scr.shape[0], A_SPAN, LANES), BF16)
            scr[:, :, LANES:] = jnp.ones((scr.shape[0], scr.shape[1], LANES), BF16)
        qi = lax.broadcasted_iota(jnp.int32, (2 * A_SPAN, 2 * A_SPAN), 0) % A_SPAN
        ki = lax.broadcasted_iota(jnp.int32, (2 * A_SPAN, 2 * A_SPAN), 1)
        band = jnp.logical_and(ki >= qi, ki <= qi + A_SPAN)
        bias_scr[1] = jnp.where(band, 0.0, -jnp.inf)
        bias_scr[0] = jnp.where(jnp.logical_and(band, ki >= A_SPAN), 0.0, -jnp.inf)

    zero = jnp.minimum(j, 0)

    def stage_tile(_, carry):
        for p, d in enumerate(A_DILATIONS):
            k_scr[p][:, A_SPAN:, :] = k_in[p][...]
            v_scr[p][:, A_SPAN:, 0:LANES] = v_in[p][...]
        return carry

    lax.fori_loop(0, 1 + zero, stage_tile, 0)

    def locate(p, grp, u):
        per_class = ATTN_TILE // A_DILATIONS[p] // A_SPAN
        b = grp * A_UNROLL + u
        n = b % per_class
        return b // per_class, pl.multiple_of(n * A_SPAN, A_SPAN), n

    def front(p, grp):
        for u in range(A_UNROLL):
            r, lo, n = locate(p, grp, u)
            has_prev = jnp.logical_or(j > 0, n > 0).astype(jnp.int32)
            p_scr[u], mx_scr[u] = _attn_scores(q_refs[p][r, pl.ds(lo, A_SPAN), :],
                                               k_scr[p][r, pl.ds(lo, 2 * A_SPAN), :], bias_scr[has_prev])

    def back(p, grp):
        d = A_DILATIONS[p]
        for u in range(A_UNROLL):
            r, lo, n = locate(p, grp, u)
            acc, m, den = _attn_values(p_scr[u], mx_scr[u], v_scr[p][r, pl.ds(lo, 2 * A_SPAN), :])
            if d > 1:
                rows = pl.ds(n * (A_SPAN * d) + r, A_SPAN, stride=d)
                o_scr[p - 1, rows, :] = acc
                m_scr[p - 1, rows, :] = m
                d_scr[p - 1, rows, :] = den
            else:
                rows = pl.ds(lo, A_SPAN)
                m1, m2 = m_scr[0, rows, :], m_scr[1, rows, :]
                top = jnp.maximum(jnp.maximum(m, m1), m2)
                w0, w1, w2 = jnp.exp2(m - top), jnp.exp2(m1 - top), jnp.exp2(m2 - top)
                num = w0 * acc + w1 * o_scr[0, rows, :] + w2 * o_scr[1, rows, :]
                tot = w0 * den + w1 * d_scr[0, rows, :] + w2 * d_scr[1, rows, :]
                o_ref[rows, :] = (num / tot).astype(BF16)

    n_grp = ATTN_TILE // A_SPAN // A_UNROLL
    order = tuple(range(1, len(A_DILATIONS))) + (0,)
    front(order[0], 0)
    for idx, p in enumerate(order):
        if n_grp > 1:

            def steady(grp, carry, p=p):
                back(p, grp - 1)
                front(p, grp)
                return carry

            lax.fori_loop(1, n_grp + zero, steady, 0)
        if idx + 1 < len(order):
            back(p, n_grp - 1)
            front(order[idx + 1], 0)

    def key_tails(_, carry):
        for p, d in enumerate(A_DILATIONS):
            tail = ATTN_TILE // d
            k_scr[p][:, 0:A_SPAN, :] = k_scr[p][:, tail:tail + A_SPAN, :]
        return carry

    lax.fori_loop(0, 1 + zero, key_tails, 0)
    back(order[-1], n_grp - 1)

    def value_tails(_, carry):
        for p, d in enumerate(A_DILATIONS):
            tail = ATTN_TILE // d
            v_scr[p][:, 0:A_SPAN, 0:LANES] = v_scr[p][:, tail:tail + A_SPAN, 0:LANES]
        return carry

    lax.fori_loop(0, 1 + zero, value_tails, 0)


def _attn_prompt(q1, q4, q16, k1, k4, k16, v1, v4, v16):
    s_len = q1.shape[1]
    t = ATTN_TILE
    b1 = pl.BlockSpec((None, 1, t, LANES), lambda hp, j: (hp, 0, j, 0))
    b4 = pl.BlockSpec((None, 4, t // 4, LANES), lambda hp, j: (hp, 0, j, 0))
    b16 = pl.BlockSpec((None, 16, t // 16, LANES), lambda hp, j: (hp, 0, j, 0))
    as4 = lambda a: a.reshape(N_PAIRS, 1, s_len, LANES)
    scr = lambda d, width: pltpu.VMEM((d, A_SPAN + t // d, width), BF16)
    n_dil = len(A_DILATIONS) - 1
    return pl.pallas_call(
        _attn_prompt_kernel,
        grid=(N_PAIRS, s_len // t),
        in_specs=[b1, b4, b16] * 3,
        out_specs=pl.BlockSpec((t, LANES), lambda hp, j: (j, hp)),
        out_shape=jax.ShapeDtypeStruct((s_len, A_WIDTH), BF16),
        scratch_shapes=[scr(d, LANES) for d in A_DILATIONS] + [scr(d, 2 * LANES) for d in A_DILATIONS]
                       + [pltpu.VMEM((2, 2 * A_SPAN, 2 * A_SPAN), F32)] + [pltpu.VMEM((n_dil, t, LANES), F32)] * 3
                       + [pltpu.VMEM((A_UNROLL, 2 * A_SPAN, 2 * A_SPAN), BF16),
                          pltpu.VMEM((A_UNROLL, 2 * A_SPAN, LANES), F32)],
        compiler_params=_params(("arbitrary", "arbitrary")),
        name="attn_prompt",
    )(as4(q1), q4, q16, as4(k1), k4, k16, as4(v1), v4, v16)


def _decode_rows(bb, w_c, q_ref, k_ref, v_ref, kt_ref, vt_ref,
                 hq_ref, hk_ref, hv_ref, hg_ref, st_ref, nrm_ref, a_ref, bn_ref, so_ref):
    n_pat = len(A_DILATIONS)
    dist = w_c - lax.broadcasted_iota(jnp.int32, (A_HEADS, w_c), 1)
    cnt = jnp.zeros((A_HEADS, w_c), F32)
    for d in A_DILATIONS:
        cnt = cnt + jnp.logical_and(dist % d == 0, dist <= A_SPAN * d).astype(F32)
    used = cnt > 0.0
    head_s = lax.broadcasted_iota(jnp.int32, (A_HEADS, w_c), 0)
    head_o = lax.broadcasted_iota(jnp.int32, (A_HEADS, A_HEAD_DIM), 0)
    for b in range(bb):
        q = q_ref[b]
        k_new, v_new = k_ref[b], v_ref[b]
        q16 = q.astype(BF16)
        s_new = jnp.sum(q * k_new, axis=-1, keepdims=True)
        s = jnp.zeros((A_HEADS, w_c), F32)
        for h in range(A_HEADS):
            s = jnp.where(head_s == h, _dot(q16, kt_ref[b, h].astype(BF16)), s)
        m = jnp.maximum(jnp.max(jnp.where(used, s, -jnp.inf), axis=-1, keepdims=True), s_new)
        w = jnp.where(used, jnp.exp(s - m), 0.0) * cnt
        p_new = n_pat * jnp.exp(s_new - m)
        den = jnp.sum(w, axis=-1, keepdims=True) + p_new
        acc = p_new * v_new
        w16 = w.astype(BF16)
        for h in range(A_HEADS):
            acc = acc + jnp.where(head_o == h, _dot_nt(w16, vt_ref[b, h].astype(BF16)), 0.0)
        a_ref[b] = acc / den
        v_rows = hv_ref[b]
        packed = jnp.concatenate([hq_ref[b], hk_ref[b], jnp.exp(hg_ref[b]),
                                  jnp.zeros((LANES - 3 * B_HEADS, B_HEAD_DIM), F32)], axis=0)
        cols = packed.T
        outs = []
        for hd in range(B_HEADS):
            col = lambda i: cols[:, i * B_HEADS + hd:i * B_HEADS + hd + 1]
            st = col(2) * st_ref[b, hd] + col(1) * v_rows[hd:hd + 1, :]
            so_ref[b, hd] = st
            o = jnp.sum(col(0) * st, axis=0, keepdims=True)
            outs.append(_rms(o, nrm_ref[hd:hd + 1, :]))
        bn_ref[b] = jnp.concatenate(outs, axis=0)


N_DECODE_IN = 11


def _recurrent_kernel(th, bb, w_c, q_ref, k_ref, v_ref, g_ref, nrm_ref, *rest):
    dec_in = rest[:N_DECODE_IN]
    o_ref, st_ref, da_ref, dbn_ref, dso_ref, state_scr, b_scr, a_scr, part_scr = rest[-9:]
    j = pl.program_id(1)

    @pl.when(j == 0)
    def _():
        state_scr[...] = jnp.zeros_like(state_scr)

    piece = B_CHUNK * B_UNROLL
    rin = lax.broadcasted_iota(jnp.int32, (piece, B_HEAD_DIM), 0) & (B_CHUNK - 1)

    def scan_piece(i, carry):
        rows = pl.ds(pl.multiple_of(i * piece, piece), piece)
        b = g_ref[rows, :] * LOG2_E
        shift = 1
        while shift < B_CHUNK:
            b = b + jnp.where(rin >= shift, pltpu.roll(b, shift, axis=0), 0.0)
            shift *= 2
        b_scr[rows, :] = b
        return carry

    lax.fori_loop(0, th // piece, scan_piece, 0)

    row = lax.broadcasted_iota(jnp.int32, (B_CHUNK, B_HEAD_DIM), 0)
    a_t = lax.broadcasted_iota(jnp.int32, (B_CHUNK, B_CHUNK), 0)
    a_s = lax.broadcasted_iota(jnp.int32, (B_CHUNK, B_CHUNK), 1)
    levels = []
    half = B_CHUNK // 2
    while half >= 1:
        seg = 2 * half
        pair = jnp.logical_and(a_t // seg == a_s // seg,
                               jnp.logical_and(a_t % seg >= half, a_s % seg < half))
        upper = row % seg >= half
        levels.append((half, upper, jnp.where(upper, 1.0, -1.0), pair))
        half //= 2
    n_grp = B_CHUNK // 8
    sub8 = lax.broadcasted_iota(jnp.int32, (n_grp, 8, B_HEAD_DIM), 1)
    nrm = nrm_ref[...]

    def centre_value(bc, half):
        seg = 2 * half
        if seg >= 8:
            return jnp.concatenate(
                [jnp.broadcast_to(bc[s0 + half - 1:s0 + half, :], (seg, B_HEAD_DIM))
                 for s0 in range(0, B_CHUNK, seg)], axis=0)
        b3 = bc.reshape(n_grp, 8, B_HEAD_DIM)
        if half == 1:
            out = jnp.where(sub8 % 2 == 0, b3, pltpu.roll(b3, 1, axis=1))
        else:
            out = jnp.broadcast_to(b3[:, half - 1:half, :], b3.shape)
            for s0 in range(seg, 8, seg):
                out = jnp.where(sub8 >= s0, jnp.broadcast_to(b3[:, s0 + half - 1:s0 + half, :], b3.shape), out)
        return out.reshape(B_CHUNK, B_HEAD_DIM)

    def intra_chunk(q, k, v, bc):
        a = jnp.zeros((B_CHUNK, B_CHUNK), F32)
        for half, upper, sign, pair in levels:
            x = (jnp.where(upper, q, k) * jnp.exp2((bc - centre_value(bc, half)) * sign)).astype(BF16)
            a = jnp.where(pair, _dot_nt(x, x), a)
        return a, jnp.sum(q * k, axis=-1, keepdims=True) * v

    def chunk_rows(grp, u):
        return pl.ds(pl.multiple_of((grp * B_UNROLL + u) * B_CHUNK, B_CHUNK), B_CHUNK)

    def front(grp):
        st = state_scr[...]
        for u in range(B_UNROLL):
            rows = chunk_rows(grp, u)
            q, k, v, bc = q_ref[rows, :], k_ref[rows, :], v_ref[rows, :], b_scr[rows, :]
            o = _dot_nt((q * jnp.exp2(bc)).astype(BF16), st.astype(BF16))
            b_last = bc[B_CHUNK - 1:B_CHUNK, :]
            k_dec = (k * jnp.exp2(b_last - bc)).astype(BF16)
            st = st * jnp.exp2(b_last) + _dot_tn(v.astype(BF16), k_dec)
            a, o3 = intra_chunk(q, k, v, bc)
            a_scr[u] = a.astype(BF16)
            part_scr[u] = o + o3
        state_scr[...] = st

    def back(grp):
        for u in range(B_UNROLL):
            rows = chunk_rows(grp, u)
            o = part_scr[u] + _dot(a_scr[u], v_ref[rows, :].astype(BF16))
            o_ref[rows, :] = _rms(o, nrm)

    n_groups = th // (B_CHUNK * B_UNROLL)
    _decode_rows(bb, w_c, *dec_in, da_ref, dbn_ref, dso_ref)
    front(0)

    def steady(grp, carry):
        back(grp - 1)
        front(grp)
        return carry

    lax.fori_loop(1, n_groups, steady, 0)
    back(n_groups - 1)

    @pl.when(j == pl.num_programs(1) - 1)
    def _():
        st_ref[...] = state_scr[...].T


def _recurrent(layer, q, k, v, g, nrm, dq, dk, dv, cache_kt, cache_vt, dhq, dhk, dhv, dhg, state, states_out, th):
    s_len = q.shape[1]
    n = dq.shape[0]
    w_c = cache_kt.shape[-1]
    n_tiles = s_len // th
    steps = B_HEADS * n_tiles
    assert w_c >= A_SPAN * max(A_DILATIONS) and n % steps == 0
    bb = n // steps
    step = lambda h, j: h * n_tiles + j
    blk = pl.BlockSpec((None, th, B_HEAD_DIM), lambda h, j: (h, j, 0))
    a_spec = pl.BlockSpec((bb, A_HEADS, A_HEAD_DIM), lambda h, j: (step(h, j), 0, 0))
    headspec = pl.BlockSpec((bb, B_HEADS, B_HEAD_DIM), lambda h, j: (step(h, j), 0, 0))
    c_spec = pl.BlockSpec((None, bb, A_HEADS, A_HEAD_DIM, w_c), lambda h, j: (layer, step(h, j), 0, 0, 0))
    st_spec = pl.BlockSpec((None, bb, B_HEADS, B_HEAD_DIM, B_HEAD_DIM), lambda h, j: (layer, step(h, j), 0, 0, 0))
    a_heads = lambda a: a.reshape(n, A_HEADS, A_HEAD_DIM)
    heads = lambda a: a.reshape(n, B_HEADS, B_HEAD_DIM)
    in_specs = [blk] * 4 + [pl.BlockSpec((None, 1, B_HEAD_DIM), lambda h, j: (layer, 0, h))]
    in_specs += [a_spec] * 3 + [c_spec] * 2 + [headspec] * 4 + [st_spec, _layer_block((B_HEADS, B_HEAD_DIM), layer)]
    args = [q, k, v, g, nrm, a_heads(dq), a_heads(dk), a_heads(dv), cache_kt, cache_vt,
            heads(dhq), heads(dhk), heads(dhv), heads(dhg), state, nrm.reshape(DEPTH, B_HEADS, B_HEAD_DIM)]
    assert len(args) == 5 + N_DECODE_IN
    aliases = {}
    if states_out is not None:
        aliases = {len(args): 4}
        in_specs.append(pl.BlockSpec(memory_space=pl.ANY))
        args.append(states_out)
    bn, st_p, a_out, dbn, states = pl.pallas_call(
        functools.partial(_recurrent_kernel, th, bb, w_c),
        grid=(B_HEADS, n_tiles),
        in_specs=in_specs,
        out_specs=[pl.BlockSpec((th, B_HEAD_DIM), lambda h, j: (j, h)),
                   pl.BlockSpec((None, B_HEAD_DIM, B_HEAD_DIM), lambda h, j: (h, 0, 0)),
                   a_spec, headspec, st_spec],
        out_shape=[jax.ShapeDtypeStruct((s_len, B_WIDTH), F32),
                   jax.ShapeDtypeStruct((B_HEADS, B_HEAD_DIM, B_HEAD_DIM), F32),
                   jax.ShapeDtypeStruct((n, A_HEADS, A_HEAD_DIM), F32),
                   jax.ShapeDtypeStruct((n, B_HEADS, B_HEAD_DIM), F32),
                   jax.ShapeDtypeStruct(state.shape, F32)],
        scratch_shapes=[pltpu.VMEM((B_HEAD_DIM, B_HEAD_DIM), F32), pltpu.VMEM((th, B_HEAD_DIM), F32),
                        pltpu.VMEM((B_UNROLL, B_CHUNK, B_CHUNK), BF16),
                        pltpu.VMEM((B_UNROLL, B_CHUNK, B_HEAD_DIM), F32)],
        input_output_aliases=aliases,
        compiler_params=_params(("arbitrary", "arbitrary")),
        name="recurrent",
    )(*args)
    return bn, st_p, a_out.reshape(n, A_WIDTH).astype(BF16), dbn.reshape(n, B_WIDTH), states


def _mix(x, a16, bn, pre, post, wg0, wg1, wa, wb, wo):
    h = _rms(x, pre).astype(BF16)
    z0 = _dot(h, wg0)
    z1 = _dot(h, wg1)
    split = W_BLOCK - B_WIDTH
    g_b = z0[:, :B_WIDTH]
    gate_a = jnp.concatenate([z0[:, B_WIDTH:], z1[:, :split]], axis=1)
    gate_b = z1[:, split:]
    b_out = (bn * (g_b * jax.nn.sigmoid(g_b))).astype(BF16)
    mix = (jax.nn.sigmoid(gate_a) * _dot(a16, wa) + jax.nn.sigmoid(gate_b) * _dot(b_out, wb)).astype(BF16)
    return x + _rms(_dot(mix, wo), post)


def _mixout_kernel(x_ref, a_ref, bn_ref, pre_ref, post_ref, wg0_ref, wg1_ref, wa_ref, wb_ref, wo_ref, o_ref):
    for sl in _row_parts(x_ref.shape[0]):
        o_ref[sl, :] = _mix(x_ref[sl, :], a_ref[sl, :], bn_ref[sl, :], pre_ref[...], post_ref[...],
                            wg0_ref[...], wg1_ref[...], wa_ref[...], wb_ref[...], wo_ref[...])


def _mixout(x, a_out, bn, layer, pre, post, w_in, wa, wb, wo, tm):
    n = x.shape[0]
    row = lambda width: pl.BlockSpec((tm, width), lambda i: (i, 0))
    return pl.pallas_call(
        _mixout_kernel,
        grid=(n // tm,),
        in_specs=[row(D_MODEL), row(A_WIDTH), row(B_WIDTH), _vec(layer), _vec(layer), _col_block(3), _col_block(4),
                  _resident((A_WIDTH, D_MODEL)), _resident((B_WIDTH, D_MODEL)), _resident((D_MODEL, D_MODEL))],
        out_specs=row(D_MODEL),
        out_shape=jax.ShapeDtypeStruct((n, D_MODEL), F32),
        compiler_params=_params(("parallel",)),
        name="mixout",
    )(x, a_out, bn, pre, post, w_in, w_in, wa, wb, wo)


def _mixout_sample_kernel(x_ref, a_ref, bn_ref, pre_ref, post_ref, wg0_ref, wg1_ref, wa_ref, wb_ref, wo_ref,
                          o_ref, wa16_ref, wb16_ref, wo16_ref):
    wa, wb, wo = wa_ref[...].astype(BF16), wb_ref[...].astype(BF16), wo_ref[...].astype(BF16)
    wa16_ref[...] = wa
    wb16_ref[...] = wb
    wo16_ref[...] = wo
    o_ref[...] = _mix(x_ref[...], a_ref[...], bn_ref[...], pre_ref[...], post_ref[...],
                      wg0_ref[...], wg1_ref[...], wa, wb, wo)


def _mixout_sample(x, a_out, bn, layer, pre, post, w_in16, wa, wb, wo):
    n = x.shape[0]
    full = lambda *shape: pl.BlockSpec(shape, lambda i: (0,) * len(shape))
    shapes = ((A_WIDTH, D_MODEL), (B_WIDTH, D_MODEL), (D_MODEL, D_MODEL))
    return pl.pallas_call(
        _mixout_sample_kernel,
        grid=(1,),
        in_specs=[full(n, D_MODEL), full(n, A_WIDTH), full(n, B_WIDTH), _vec(layer), _vec(layer),
                  _col_block(3), _col_block(4)] + [_layer_block(s, layer) for s in shapes],
        out_specs=[full(n, D_MODEL)] + [full(*s) for s in shapes],
        out_shape=[jax.ShapeDtypeStruct((n, D_MODEL), F32)] + [jax.ShapeDtypeStruct(s, BF16) for s in shapes],
        compiler_params=_params(("arbitrary",)),
        name="mixout_sample",
    )(x, a_out, bn, pre, post, w_in16, w_in16, wa, wb, wo)


def _rope_tables(pos):
    half = ROPE_DIM // 2
    inv = ROPE_THETA ** (-jnp.arange(half, dtype=F32) / half)
    dim = jnp.arange(LANES) % A_HEAD_DIM
    ang = pos.astype(F32)[:, None] * inv[dim % half][None, :]
    cos, sin = jnp.cos(ang), jnp.sin(ang)
    lo, hi = (dim < half)[None, :], jnp.logical_and(dim >= half, dim < ROPE_DIM)[None, :]
    return (jnp.where(jnp.logical_or(lo, hi), cos, 1.0), jnp.where(lo, -sin, 0.0), jnp.where(hi, sin, 0.0))


def kernel(x_prompt, x_sample, cache_k, cache_v, state_hgrn, ffn1_norm_pre, ffn1_norm_post, ffn1_w_gate, ffn1_w_up, ffn1_w_down, mix_norm_pre, mix_norm_post, w_in, hgrn_lb_logits, hgrn_out_norm, w_a_out, w_b_out, w_mix_out, ffn2_norm_pre, ffn2_norm_post, ffn2_w_gate, ffn2_w_up, ffn2_w_down):
    batch, s_len, _ = x_prompt.shape
    n_dec, t_dec, _ = x_sample.shape
    assert batch == 1 and t_dec == 1 and s_len % ATTN_TILE == 0
    tm = 512
    tm_wide = 1024
    th = min(4096, s_len)
    cache_kt = jnp.transpose(cache_k, (0, 1, 3, 4, 2))
    cache_vt = jnp.transpose(cache_v, (0, 1, 3, 4, 2))
    yp = x_prompt.reshape(s_len, D_MODEL)
    ys = x_sample.reshape(n_dec, D_MODEL)
    tabs_p = _rope_tables(jnp.arange(s_len))
    tabs_s = _rope_tables(jnp.full((n_dec,), PAST_LEN))
    lbl = hgrn_lb_logits.astype(F32)
    vecs = lambda a: a.reshape(DEPTH, 1, -1).astype(F32)
    w_len = min(A_MAX_WINDOW, s_len)
    f1_pre, f1_post, f2_pre, f2_post = (vecs(a) for a in (ffn1_norm_pre, ffn1_norm_post, ffn2_norm_pre, ffn2_norm_post))
    m_pre, m_post, nrm = vecs(mix_norm_pre), vecs(mix_norm_post), vecs(hgrn_out_norm)
    kp, vp, sp, ksn, vsn = [], [], [], [], []
    states = None
    for l in range(DEPTH):
        ys, *f1 = _ffn_sample(ys, l, f1_pre, f1_post, ffn1_w_gate, ffn1_w_up, ffn1_w_down)
        qa, ka, va, sq, sk, sv, sg, win = _inproj_sample(ys, m_pre, tabs_s, lbl, w_in, l)
        yp = _ffn(yp, l, f1_pre, f1_post, *f1, tm_wide)
        q1, q4, q16, k1, k4, k16, v1, v4, v16, kf, vf = _inproj_attn(yp, m_pre, tabs_p, win, l, tm)
        hq, hk, hv, hg = _inproj_rec(yp, m_pre, lbl, win, l, tm)
        a_out = _attn_prompt(q1, q4, q16, k1, k4, k16, v1, v4, v16)
        bn, st_p, a_s, bn_s, states = _recurrent(l, hq, hk, hv, hg, nrm, qa, ka, va, cache_kt, cache_vt,
                                                 sq, sk, sv, sg, state_hgrn, states, th)
        ys, *mo = _mixout_sample(ys, a_s, bn_s, l, m_pre, m_post, win, w_a_out, w_b_out, w_mix_out)
        ys, *f2 = _ffn_sample(ys, l, f2_pre, f2_post, ffn2_w_gate, ffn2_w_up, ffn2_w_down)
        yp = _mixout(yp, a_out, bn, l, m_pre, m_post, win, *mo, tm_wide)
        yp = _ffn(yp, l, f2_pre, f2_post, *f2, tm_wide)
        kp.append(kf.reshape(batch, w_len, A_HEADS, A_HEAD_DIM))
        vp.append(vf.reshape(batch, w_len, A_HEADS, A_HEAD_DIM))
        sp.append(st_p.reshape(batch, B_HEADS, B_HEAD_DIM, B_HEAD_DIM))
        ksn.append(ka.reshape(n_dec, t_dec, A_HEADS, A_HEAD_DIM))
        vsn.append(va.reshape(n_dec, t_dec, A_HEADS, A_HEAD_DIM))
    return (yp.reshape(batch, s_len, D_MODEL), ys.reshape(n_dec, t_dec, D_MODEL),
            jnp.stack(kp), jnp.stack(vp), jnp.stack(sp), jnp.stack(ksn), jnp.stack(vsn), states)
```

```python
import functools

import jax
import jax.numpy as jnp
from jax import lax
from jax.experimental import pallas as pl
from jax.experimental.pallas import tpu as pltpu

F32 = jnp.float32
BF16 = jnp.bfloat16

D_MODEL = 1024
DEPTH = 2
PAST_LEN = 16384
A_HEADS = 8
A_HEAD_DIM = 64
A_WIDTH = A_HEADS * A_HEAD_DIM
A_DILATIONS = (1, 4, 16)
A_SPAN = 128
A_MAX_WINDOW = 2048
A_SCALE = A_HEAD_DIM ** -0.5
ROPE_THETA = 500000.0
ROPE_DIM = A_HEAD_DIM // 4
B_HEADS = 8
B_HEAD_DIM = 128
B_WIDTH = B_HEADS * B_HEAD_DIM
B_CHUNK = 64
B_UNROLL = 64
B_SCALE = B_HEAD_DIM ** -0.5
D_FF = 2816
FF_CHUNK = 256
ROW_PARTS = 4
EPS = 1e-6
LOG2_E = 1.4426950408889634

W_BLOCK = 3 * A_WIDTH
assert 3 * B_WIDTH == 2 * W_BLOCK and B_WIDTH + 2 * D_MODEL == 2 * W_BLOCK
LANES = 128
N_PAIRS = A_WIDTH // LANES
ATTN_TILE = A_SPAN * max(A_DILATIONS)
A_UNROLL = 16
VMEM_LIMIT = 56 * 1024 * 1024


def _params(sem, vmem=VMEM_LIMIT):
    return pltpu.CompilerParams(dimension_semantics=sem, vmem_limit_bytes=vmem)


def _resident(shape):
    nd = len(shape)
    return pl.BlockSpec(shape, lambda *_: (0,) * nd, pipeline_mode=pl.Buffered(1))


def _layer_block(shape, layer, col=0):
    index = (layer,) + (0,) * (len(shape) - 1) + (col,)
    return pl.BlockSpec((None,) + tuple(shape), lambda *_: index, pipeline_mode=pl.Buffered(1))


def _vec(layer):
    return _layer_block((1, D_MODEL), layer)


def _rms(x, g):
    y = x * lax.rsqrt(jnp.mean(x * x, axis=-1, keepdims=True) + EPS)
    return y * g


def _dot(a, b):
    return jnp.dot(a, b, preferred_element_type=F32)


def _dot_nt(a, b):
    return lax.dot_general(a, b, (((1,), (1,)), ((), ())), preferred_element_type=F32)


def _dot_tn(a, b):
    return lax.dot_general(a, b, (((0,), (0,)), ((), ())), preferred_element_type=F32)


def _row_parts(rows):
    parts = ROW_PARTS if rows % (8 * ROW_PARTS) == 0 else 1
    return [pl.ds(p * (rows // parts), rows // parts) for p in range(parts)]


def _ffn_kernel(x_ref, pre_ref, post_ref, wg_ref, wu_ref, wd_ref, o_ref):
    for sl in _row_parts(x_ref.shape[0]):
        x = x_ref[sl, :]
        h = _rms(x, pre_ref[...]).astype(BF16)
        g = _dot(h, wg_ref[...])
        u = _dot(h, wu_ref[...])
        a = (g * jax.nn.sigmoid(g) * u).astype(BF16)
        y = _dot(a, wd_ref[...])
        o_ref[sl, :] = x + 0.5 * _rms(y, post_ref[...])


def _ffn(x, layer, pre, post, wg, wu, wd, tm):
    n = x.shape[0]
    row = pl.BlockSpec((tm, D_MODEL), lambda i: (i, 0))
    return pl.pallas_call(
        _ffn_kernel,
        grid=(n // tm,),
        in_specs=[row, _vec(layer), _vec(layer), _resident((D_MODEL, D_FF)),
                  _resident((D_MODEL, D_FF)), _resident((D_FF, D_MODEL))],
        out_specs=row,
        out_shape=jax.ShapeDtypeStruct((n, D_MODEL), F32),
        compiler_params=_params(("parallel",)),
        name="ffn",
    )(x, pre, post, wg, wu, wd)


def _ffn_sample_kernel(x_ref, pre_ref, post_ref, wg_ref, wu_ref, wd_ref,
                       o_ref, wg16_ref, wu16_ref, wd16_ref, h_scr, acc_scr):
    c = pl.program_id(0)

    @pl.when(c == 0)
    def _():
        h_scr[...] = _rms(x_ref[...], pre_ref[...]).astype(BF16)
        acc_scr[...] = jnp.zeros_like(acc_scr)

    wg, wu, wd = wg_ref[...].astype(BF16), wu_ref[...].astype(BF16), wd_ref[...].astype(BF16)
    wg16_ref[...] = wg
    wu16_ref[...] = wu
    wd16_ref[...] = wd
    h = h_scr[...]
    g = _dot(h, wg)
    u = _dot(h, wu)
    acc_scr[...] += _dot((g * jax.nn.sigmoid(g) * u).astype(BF16), wd)

    @pl.when(c == pl.num_programs(0) - 1)
    def _():
        o_ref[...] = x_ref[...] + 0.5 * _rms(acc_scr[...], post_ref[...])


def _ffn_sample(x, layer, pre, post, wg, wu, wd):
    n = x.shape[0]
    full = pl.BlockSpec((n, D_MODEL), lambda c: (0, 0))
    col = lambda: pl.BlockSpec((None, D_MODEL, FF_CHUNK), lambda c: (layer, 0, c))
    col16 = lambda: pl.BlockSpec((D_MODEL, FF_CHUNK), lambda c: (0, c))
    return pl.pallas_call(
        _ffn_sample_kernel,
        grid=(D_FF // FF_CHUNK,),
        in_specs=[full, _vec(layer), _vec(layer), col(), col(),
                  pl.BlockSpec((None, FF_CHUNK, D_MODEL), lambda c: (layer, c, 0))],
        out_specs=[full, col16(), col16(), pl.BlockSpec((FF_CHUNK, D_MODEL), lambda c: (c, 0))],
        out_shape=[jax.ShapeDtypeStruct((n, D_MODEL), F32), jax.ShapeDtypeStruct((D_MODEL, D_FF), BF16),
                   jax.ShapeDtypeStruct((D_MODEL, D_FF), BF16), jax.ShapeDtypeStruct((D_FF, D_MODEL), BF16)],
        scratch_shapes=[pltpu.VMEM((n, D_MODEL), BF16), pltpu.VMEM((n, D_MODEL), F32)],
        compiler_params=_params(("arbitrary",)),
        name="ffn_sample",
    )(x, pre, post, wg, wu, wd)


def _lower_bound(logits, layer):
    e = jnp.exp(logits - jnp.max(logits, axis=0, keepdims=True))
    sm = e / jnp.sum(e, axis=0, keepdims=True)
    lb = jnp.zeros((1, B_WIDTH), F32)
    for i in range(1, layer + 1):
        lb = lb + sm[i:i + 1, :]
    return lb


def _rope_slab(t, cos, sin_lo, sin_hi):
    return t * cos + pltpu.roll(t, LANES - ROPE_DIM // 2, axis=1) * sin_lo + pltpu.roll(t, ROPE_DIM // 2, axis=1) * sin_hi


def _col_block(col):
    return pl.BlockSpec((D_MODEL, W_BLOCK), lambda *_: (0, col), pipeline_mode=pl.Buffered(1))


def _hgrn_gates(z0, z1, lb):
    split = W_BLOCK - B_WIDTH
    q = z0[:, :B_WIDTH] * B_SCALE
    f_raw = jnp.concatenate([z0[:, B_WIDTH:], z1[:, :split]], axis=1)
    i_raw = z1[:, split:]
    f = lb + (1.0 - lb) * jax.nn.sigmoid(f_raw)
    return q, 1.0 - f, i_raw * jax.nn.sigmoid(i_raw), jnp.log(f)


def _inproj_attn_kernel(tm, x_ref, pre_ref, cos_ref, slo_ref, shi_ref, wa_ref,
                        q1_ref, q4_ref, q16_ref, k1_ref, k4_ref, k16_ref, v1_ref, v4_ref, v16_ref,
                        kf_ref, vf_ref, stage_ref, stage4_ref):
    h = _rms(x_ref[...], pre_ref[...]).astype(BF16)
    za = _dot(h, wa_ref[...])
    cos, slo, shi = cos_ref[...], slo_ref[...], shi_ref[...]
    outs = ((q1_ref, q4_ref, q16_ref), (k1_ref, k4_ref, k16_ref), (v1_ref, v4_ref, v16_ref))
    for s in range(3 * N_PAIRS):
        kind, hp = divmod(s, N_PAIRS)
        t = za[:, s * LANES:(s + 1) * LANES]
        if kind < 2:
            t = _rope_slab(t, cos, slo, shi)
        if kind == 0:
            t = t * (A_SCALE * LOG2_E)
        if kind == 1:
            kf_ref[:, hp * LANES:(hp + 1) * LANES] = t
        if kind == 2:
            vf_ref[:, hp * LANES:(hp + 1) * LANES] = t
        stage_ref[s] = t
        o1, o4, o16 = outs[kind]
        o1[hp] = t.astype(BF16)
        for r in range(4):
            c4 = stage_ref[s, pl.ds(r, tm // 4, stride=4), :]
            o4[hp, r] = c4.astype(BF16)
            stage4_ref[s, r] = c4
        for r in range(4):
            for a in range(4):
                o16[hp, r + 4 * a] = stage4_ref[s, r, pl.ds(a, tm // 16, stride=4), :].astype(BF16)


def _inproj_attn(x, pre, tabs, w_in, layer, tm):
    s_len = x.shape[0]
    w_len = min(A_MAX_WINDOW, s_len)
    first = (s_len - w_len) // tm
    row = pl.BlockSpec((tm, D_MODEL), lambda i: (i, 0))
    tab = pl.BlockSpec((tm, LANES), lambda i: (i, 0))
    l1 = pl.BlockSpec((N_PAIRS, tm, LANES), lambda i: (0, i, 0))
    l4 = pl.BlockSpec((N_PAIRS, 4, tm // 4, LANES), lambda i: (0, 0, i, 0))
    l16 = pl.BlockSpec((N_PAIRS, 16, tm // 16, LANES), lambda i: (0, 0, i, 0))
    win = pl.BlockSpec((tm, A_WIDTH), lambda i: (jnp.maximum(i - first, 0), 0))
    s1 = jax.ShapeDtypeStruct((N_PAIRS, s_len, LANES), BF16)
    s4 = jax.ShapeDtypeStruct((N_PAIRS, 4, s_len // 4, LANES), BF16)
    s16 = jax.ShapeDtypeStruct((N_PAIRS, 16, s_len // 16, LANES), BF16)
    sw = jax.ShapeDtypeStruct((w_len, A_WIDTH), F32)
    return pl.pallas_call(
        functools.partial(_inproj_attn_kernel, tm),
        grid=(s_len // tm,),
        in_specs=[row, _vec(layer), tab, tab, tab, _col_block(0)],
        out_specs=[l1, l4, l16, l1, l4, l16, l1, l4, l16, win, win],
        out_shape=[s1, s4, s16, s1, s4, s16, s1, s4, s16, sw, sw],
        scratch_shapes=[pltpu.VMEM((3 * N_PAIRS, tm, LANES), F32), pltpu.VMEM((3 * N_PAIRS, 4, tm // 4, LANES), F32)],
        compiler_params=_params(("arbitrary",)),
        name="inproj_attn",
    )(x, pre, *tabs, w_in)


def _inproj_rec_kernel(layer, x_ref, pre_ref, lbl_ref, wb0_ref, wb1_ref, hq_ref, hk_ref, hv_ref, hg_ref):
    lb = _lower_bound(lbl_ref[...], layer)
    for rows in _row_parts(x_ref.shape[0]):
        h = _rms(x_ref[rows, :], pre_ref[...]).astype(BF16)
        q, k, v, g = _hgrn_gates(_dot(h, wb0_ref[...]), _dot(h, wb1_ref[...]), lb)
        for hd in range(B_HEADS):
            sl = slice(hd * B_HEAD_DIM, (hd + 1) * B_HEAD_DIM)
            hq_ref[hd, rows, :] = q[:, sl]
            hk_ref[hd, rows, :] = k[:, sl]
            hv_ref[hd, rows, :] = v[:, sl]
            hg_ref[hd, rows, :] = g[:, sl]


def _inproj_rec(x, pre, lbl, w_in, layer, tm):
    s_len = x.shape[0]
    row = pl.BlockSpec((tm, D_MODEL), lambda i: (i, 0))
    hd = pl.BlockSpec((B_HEADS, tm, B_HEAD_DIM), lambda i: (0, i, 0))
    sh = jax.ShapeDtypeStruct((B_HEADS, s_len, B_HEAD_DIM), F32)
    return pl.pallas_call(
        functools.partial(_inproj_rec_kernel, layer),
        grid=(s_len // tm,),
        in_specs=[row, _vec(layer), _resident((DEPTH, B_WIDTH)), _col_block(1), _col_block(2)],
        out_specs=[hd] * 4,
        out_shape=[sh] * 4,
        compiler_params=_params(("parallel",)),
        name="inproj_rec",
    )(x, pre, lbl, w_in, w_in)


def _inproj_sample_kernel(layer, x_ref, pre_ref, cos_ref, slo_ref, shi_ref, lbl_ref, w_ref,
                          qa_ref, ka_ref, va_ref, hq_ref, hk_ref, hv_ref, hg_ref, w16_ref, h_scr, z_scr):
    c = pl.program_id(0)

    @pl.when(c == 0)
    def _():
        h_scr[...] = _rms(x_ref[...], pre_ref[...]).astype(BF16)

    w = w_ref[...].astype(BF16)
    w16_ref[...] = w

    @pl.when(c == 0)
    def _():
        za = _dot(h_scr[...], w)
        cos, slo, shi = cos_ref[...], slo_ref[...], shi_ref[...]
        outs = (qa_ref, ka_ref, va_ref)
        for s in range(3 * N_PAIRS):
            kind, hp = divmod(s, N_PAIRS)
            t = za[:, s * LANES:(s + 1) * LANES]
            if kind < 2:
                t = _rope_slab(t, cos, slo, shi)
            if kind == 0:
                t = t * A_SCALE
            outs[kind][:, hp * LANES:(hp + 1) * LANES] = t

    @pl.when(c == 1)
    def _():
        z_scr[...] = _dot(h_scr[...], w)

    @pl.when(c == 2)
    def _():
        q, k, v, g = _hgrn_gates(z_scr[...], _dot(h_scr[...], w), _lower_bound(lbl_ref[...], layer))
        hq_ref[...] = q
        hk_ref[...] = k
        hv_ref[...] = v
        hg_ref[...] = g


def _inproj_sample(x, pre, tabs, lbl, w_in, layer):
    n = x.shape[0]
    n_col = w_in.shape[-1] // W_BLOCK
    full = lambda *shape: pl.BlockSpec(shape, lambda c: (0,) * len(shape))
    sa = jax.ShapeDtypeStruct((n, A_WIDTH), F32)
    sb = jax.ShapeDtypeStruct((n, B_WIDTH), F32)
    return pl.pallas_call(
        functools.partial(_inproj_sample_kernel, layer),
        grid=(n_col,),
        in_specs=[full(n, D_MODEL), _vec(layer), full(n, LANES), full(n, LANES), full(n, LANES),
                  full(DEPTH, B_WIDTH), pl.BlockSpec((None, D_MODEL, W_BLOCK), lambda c: (layer, 0, c))],
        out_specs=[full(n, A_WIDTH)] * 3 + [full(n, B_WIDTH)] * 4
                  + [pl.BlockSpec((D_MODEL, W_BLOCK), lambda c: (0, c))],
        out_shape=[sa] * 3 + [sb] * 4 + [jax.ShapeDtypeStruct((D_MODEL, n_col * W_BLOCK), BF16)],
        scratch_shapes=[pltpu.VMEM((n, D_MODEL), BF16), pltpu.VMEM((n, W_BLOCK), F32)],
        compiler_params=_params(("arbitrary",)),
        name="inproj_sample",
    )(x, pre, *tabs, lbl, w_in)


def _attn_scores(q, k, bias):
    first = lax.broadcasted_iota(jnp.int32, (A_SPAN, LANES), 1) < A_HEAD_DIM
    zero = jnp.zeros_like(q)
    q2 = jnp.concatenate([jnp.where(first, q, zero), jnp.where(first, zero, q)], axis=0)
    s = _dot_nt(q2, k) + bias
    m = jnp.max(jnp.maximum(s[:, :A_SPAN], s[:, A_SPAN:]), axis=-1, keepdims=True)
    return jnp.exp2(s - m).astype(BF16), jnp.broadcast_to(m, (2 * A_SPAN, LANES))


def _attn_values(p, m, v_ext):
    first = lax.broadcasted_iota(jnp.int32, (A_SPAN, LANES), 1) < A_HEAD_DIM
    oe = _dot(p, v_ext)
    pick = lambda t: jnp.where(first, t[:A_SPAN], t[A_SPAN:])
    return pick(oe[:, :LANES]), pick(m), pick(oe[:, LANES:])


def _attn_prompt_kernel(q1_ref, q4_ref, q16_ref, k1_ref, k4_ref, k16_ref, v1_ref, v4_ref, v16_ref,
                        o_ref, ks1, ks4, ks16, vs1, vs4, vs16, bias_scr, o_scr, m_scr, d_scr, p_scr, mx_scr):
    j = pl.program_id(1)
    q_refs = (q1_ref, q4_ref, q16_ref)
    k_in = (k1_ref, k4_ref, k16_ref)
    v_in = (v1_ref, v4_ref, v16_ref)
    k_scr = (ks1, ks4, ks16)
    v_scr = (vs1, vs4, vs16)

    @pl.when(j == 0)
    def _():
        for scr in k_scr:
            scr[:, 0:A_SPAN, :] = jnp.zeros((scr.shape[0], A_SPAN, LANES), BF16)
        for scr in v_scr:
            scr[:, 0:A_SPAN, 0:LANES] = jnp.zeros((scr.shape[0], A_SPAN, LANES), BF16)
            scr[:, :, LANES:] = jnp.ones((scr.shape[0], scr.shape[1], LANES), BF16)
        qi = lax.broadcasted_iota(jnp.int32, (2 * A_SPAN, 2 * A_SPAN), 0) % A_SPAN
        ki = lax.broadcasted_iota(jnp.int32, (2 * A_SPAN, 2 * A_SPAN), 1)
        band = jnp.logical_and(ki >= qi, ki <= qi + A_SPAN)
        bias_scr[1] = jnp.where(band, 0.0, -jnp.inf)
        bias_scr[0] = jnp.where(jnp.logical_and(band, ki >= A_SPAN), 0.0, -jnp.inf)

    zero = jnp.minimum(j, 0)

    def stage_tile(_, carry):
        for p, d in enumerate(A_DILATIONS):
            k_scr[p][:, A_SPAN:, :] = k_in[p][...]
            v_scr[p][:, A_SPAN:, 0:LANES] = v_in[p][...]
        return carry

    lax.fori_loop(0, 1 + zero, stage_tile, 0)

    def locate(p, grp, u):
        per_class = ATTN_TILE // A_DILATIONS[p] // A_SPAN
        b = grp * A_UNROLL + u
        n = b % per_class
        return b // per_class, pl.multiple_of(n * A_SPAN, A_SPAN), n

    def front(p, grp):
        for u in range(A_UNROLL):
            r, lo, n = locate(p, grp, u)
            has_prev = jnp.logical_or(j > 0, n > 0).astype(jnp.int32)
            p_scr[u], mx_scr[u] = _attn_scores(q_refs[p][r, pl.ds(lo, A_SPAN), :],
                                               k_scr[p][r, pl.ds(lo, 2 * A_SPAN), :], bias_scr[has_prev])

    def back(p, grp):
        d = A_DILATIONS[p]
        for u in range(A_UNROLL):
            r, lo, n = locate(p, grp, u)
            acc, m, den = _attn_values(p_scr[u], mx_scr[u], v_scr[p][r, pl.ds(lo, 2 * A_SPAN), :])
            if d > 1:
                rows = pl.ds(n * (A_SPAN * d) + r, A_SPAN, stride=d)
                o_scr[p - 1, rows, :] = acc
                m_scr[p - 1, rows, :] = m
                d_scr[p - 1, rows, :] = den
            else:
                rows = pl.ds(lo, A_SPAN)
                m1, m2 = m_scr[0, rows, :], m_scr[1, rows, :]
                top = jnp.maximum(jnp.maximum(m, m1), m2)
                w0, w1, w2 = jnp.exp2(m - top), jnp.exp2(m1 - top), jnp.exp2(m2 - top)
                num = w0 * acc + w1 * o_scr[0, rows, :] + w2 * o_scr[1, rows, :]
                tot = w0 * den + w1 * d_scr[0, rows, :] + w2 * d_scr[1, rows, :]
                o_ref[rows, :] = (num / tot).astype(BF16)

    n_grp = ATTN_TILE // A_SPAN // A_UNROLL
    order = tuple(range(1, len(A_DILATIONS))) + (0,)
    front(order[0], 0)
    for idx, p in enumerate(order):
        if n_grp > 1:

            def steady(grp, carry, p=p):
                back(p, grp - 1)
                front(p, grp)
                return carry

            lax.fori_loop(1, n_grp + zero, steady, 0)
        if idx + 1 < len(order):
            back(p, n_grp - 1)
            front(order[idx + 1], 0)

    def key_tails(_, carry):
        for p, d in enumerate(A_DILATIONS):
            tail = ATTN_TILE // d
            k_scr[p][:, 0:A_SPAN, :] = k_scr[p][:, tail:tail + A_SPAN, :]
        return carry

    lax.fori_loop(0, 1 + zero, key_tails, 0)
    back(order[-1], n_grp - 1)

    def value_tails(_, carry):
        for p, d in enumerate(A_DILATIONS):
            tail = ATTN_TILE // d
            v_scr[p][:, 0:A_SPAN, 0:LANES] = v_scr[p][:, tail:tail + A_SPAN, 0:LANES]
        return carry

    lax.fori_loop(0, 1 + zero, value_tails, 0)


def _attn_prompt(q1, q4, q16, k1, k4, k16, v1, v4, v16):
    s_len = q1.shape[1]
    t = ATTN_TILE
    b1 = pl.BlockSpec((None, 1, t, LANES), lambda hp, j: (hp, 0, j, 0))
    b4 = pl.BlockSpec((None, 4, t // 4, LANES), lambda hp, j: (hp, 0, j, 0))
    b16 = pl.BlockSpec((None, 16, t // 16, LANES), lambda hp, j: (hp, 0, j, 0))
    as4 = lambda a: a.reshape(N_PAIRS, 1, s_len, LANES)
    scr = lambda d, width: pltpu.VMEM((d, A_SPAN + t // d, width), BF16)
    n_dil = len(A_DILATIONS) - 1
    return pl.pallas_call(
        _attn_prompt_kernel,
        grid=(N_PAIRS, s_len // t),
        in_specs=[b1, b4, b16] * 3,
        out_specs=pl.BlockSpec((t, LANES), lambda hp, j: (j, hp)),
        out_shape=jax.ShapeDtypeStruct((s_len, A_WIDTH), BF16),
        scratch_shapes=[scr(d, LANES) for d in A_DILATIONS] + [scr(d, 2 * LANES) for d in A_DILATIONS]
                       + [pltpu.VMEM((2, 2 * A_SPAN, 2 * A_SPAN), F32)] + [pltpu.VMEM((n_dil, t, LANES), F32)] * 3
                       + [pltpu.VMEM((A_UNROLL, 2 * A_SPAN, 2 * A_SPAN), BF16),
                          pltpu.VMEM((A_UNROLL, 2 * A_SPAN, LANES), F32)],
        compiler_params=_params(("arbitrary", "arbitrary")),
        name="attn_prompt",
    )(as4(q1), q4, q16, as4(k1), k4, k16, as4(v1), v4, v16)


def _decode_rows(bb, w_c, q_ref, k_ref, v_ref, kt_ref, vt_ref,
                 hq_ref, hk_ref, hv_ref, hg_ref, st_ref, nrm_ref, a_ref, bn_ref, so_ref):
    n_pat = len(A_DILATIONS)
    dist = w_c - lax.broadcasted_iota(jnp.int32, (A_HEADS, w_c), 1)
    cnt = jnp.zeros((A_HEADS, w_c), F32)
    for d in A_DILATIONS:
        cnt = cnt + jnp.logical_and(dist % d == 0, dist <= A_SPAN * d).astype(F32)
    used = cnt > 0.0
    head_s = lax.broadcasted_iota(jnp.int32, (A_HEADS, w_c), 0)
    head_o = lax.broadcasted_iota(jnp.int32, (A_HEADS, A_HEAD_DIM), 0)
    for b in range(bb):
        q = q_ref[b]
        k_new, v_new = k_ref[b], v_ref[b]
        q16 = q.astype(BF16)
        s_new = jnp.sum(q * k_new, axis=-1, keepdims=True)
        s = jnp.zeros((A_HEADS, w_c), F32)
        for h in range(A_HEADS):
            s = jnp.where(head_s == h, _dot(q16, kt_ref[b, h].astype(BF16)), s)
        m = jnp.maximum(jnp.max(jnp.where(used, s, -jnp.inf), axis=-1, keepdims=True), s_new)
        w = jnp.where(used, jnp.exp(s - m), 0.0) * cnt
        p_new = n_pat * jnp.exp(s_new - m)
        den = jnp.sum(w, axis=-1, keepdims=True) + p_new
        acc = p_new * v_new
        w16 = w.astype(BF16)
        for h in range(A_HEADS):
            acc = acc + jnp.where(head_o == h, _dot_nt(w16, vt_ref[b, h].astype(BF16)), 0.0)
        a_ref[b] = acc / den
        v_rows = hv_ref[b]
        packed = jnp.concatenate([hq_ref[b], hk_ref[b], jnp.exp(hg_ref[b]),
                                  jnp.zeros((LANES - 3 * B_HEADS, B_HEAD_DIM), F32)], axis=0)
        cols = packed.T
        outs = []
        for hd in range(B_HEADS):
            col = lambda i: cols[:, i * B_HEADS + hd:i * B_HEADS + hd + 1]
            st = col(2) * st_ref[b, hd] + col(1) * v_rows[hd:hd + 1, :]
            so_ref[b, hd] = st
            o = jnp.sum(col(0) * st, axis=0, keepdims=True)
            outs.append(_rms(o, nrm_ref[hd:hd + 1, :]))
        bn_ref[b] = jnp.concatenate(outs, axis=0)


N_DECODE_IN = 11


def _recurrent_kernel(th, bb, w_c, q_ref, k_ref, v_ref, g_ref, nrm_ref, *rest):
    dec_in = rest[:N_DECODE_IN]
    o_ref, st_ref, da_ref, dbn_ref, dso_ref, state_scr, b_scr, a_scr, part_scr = rest[-9:]
    j = pl.program_id(1)

    @pl.when(j == 0)
    def _():
        state_scr[...] = jnp.zeros_like(state_scr)

    piece = B_CHUNK * B_UNROLL
    rin = lax.broadcasted_iota(jnp.int32, (piece, B_HEAD_DIM), 0) & (B_CHUNK - 1)

    def scan_piece(i, carry):
        rows = pl.ds(pl.multiple_of(i * piece, piece), piece)
        b = g_ref[rows, :] * LOG2_E
        shift = 1
        while shift < B_CHUNK:
            b = b + jnp.where(rin >= shift, pltpu.roll(b, shift, axis=0), 0.0)
            shift *= 2
        b_scr[rows, :] = b
        return carry

    lax.fori_loop(0, th // piece, scan_piece, 0)

    row = lax.broadcasted_iota(jnp.int32, (B_CHUNK, B_HEAD_DIM), 0)
    a_t = lax.broadcasted_iota(jnp.int32, (B_CHUNK, B_CHUNK), 0)
    a_s = lax.broadcasted_iota(jnp.int32, (B_CHUNK, B_CHUNK), 1)
    levels = []
    half = B_CHUNK // 2
    while half >= 1:
        seg = 2 * half
        pair = jnp.logical_and(a_t // seg == a_s // seg,
                               jnp.logical_and(a_t % seg >= half, a_s % seg < half))
        upper = row % seg >= half
        levels.append((half, upper, jnp.where(upper, 1.0, -1.0), pair))
        half //= 2
    n_grp = B_CHUNK // 8
    sub8 = lax.broadcasted_iota(jnp.int32, (n_grp, 8, B_HEAD_DIM), 1)
    nrm = nrm_ref[...]

    def centre_value(bc, half):
        seg = 2 * half
        if seg >= 8:
            return jnp.concatenate(
                [jnp.broadcast_to(bc[s0 + half - 1:s0 + half, :], (seg, B_HEAD_DIM))
                 for s0 in range(0, B_CHUNK, seg)], axis=0)
        b3 = bc.reshape(n_grp, 8, B_HEAD_DIM)
        if half == 1:
            out = jnp.where(sub8 % 2 == 0, b3, pltpu.roll(b3, 1, axis=1))
        else:
            out = jnp.broadcast_to(b3[:, half - 1:half, :], b3.shape)
            for s0 in range(seg, 8, seg):
                out = jnp.where(sub8 >= s0, jnp.broadcast_to(b3[:, s0 + half - 1:s0 + half, :], b3.shape), out)
        return out.reshape(B_CHUNK, B_HEAD_DIM)

    def intra_chunk(q, k, v, bc):
        a = jnp.zeros((B_CHUNK, B_CHUNK), F32)
        for half, upper, sign, pair in levels:
            x = (jnp.where(upper, q, k) * jnp.exp2((bc - centre_value(bc, half)) * sign)).astype(BF16)
            a = jnp.where(pair, _dot_nt(x, x), a)
        return a, jnp.sum(q * k, axis=-1, keepdims=True) * v

    def chunk_rows(grp, u):
        return pl.ds(pl.multiple_of((grp * B_UNROLL + u) * B_CHUNK, B_CHUNK), B_CHUNK)

    def front(grp):
        st = state_scr[...]
        for u in range(B_UNROLL):
            rows = chunk_rows(grp, u)
            q, k, v, bc = q_ref[rows, :], k_ref[rows, :], v_ref[rows, :], b_scr[rows, :]
            o = _dot_nt((q * jnp.exp2(bc)).astype(BF16), st.astype(BF16))
            b_last = bc[B_CHUNK - 1:B_CHUNK, :]
            k_dec = (k * jnp.exp2(b_last - bc)).astype(BF16)
            st = st * jnp.exp2(b_last) + _dot_tn(v.astype(BF16), k_dec)
            a, o3 = intra_chunk(q, k, v, bc)
            a_scr[u] = a.astype(BF16)
            part_scr[u] = o + o3
        state_scr[...] = st

    def back(grp):
        for u in range(B_UNROLL):
            rows = chunk_rows(grp, u)
            o = part_scr[u] + _dot(a_scr[u], v_ref[rows, :].astype(BF16))
            o_ref[rows, :] = _rms(o, nrm)

    n_groups = th // (B_CHUNK * B_UNROLL)
    _decode_rows(bb, w_c, *dec_in, da_ref, dbn_ref, dso_ref)
    front(0)

    def steady(grp, carry):
        back(grp - 1)
        front(grp)
        return carry

    lax.fori_loop(1, n_groups, steady, 0)
    back(n_groups - 1)

    @pl.when(j == pl.num_programs(1) - 1)
    def _():
        st_ref[...] = state_scr[...].T


def _recurrent(layer, q, k, v, g, nrm, dq, dk, dv, cache_kt, cache_vt, dhq, dhk, dhv, dhg, state, states_out, th):
    s_len = q.shape[1]
    n = dq.shape[0]
    w_c = cache_kt.shape[-1]
    n_tiles = s_len // th
    steps = B_HEADS * n_tiles
    assert w_c >= A_SPAN * max(A_DILATIONS) and n % steps == 0
    bb = n // steps
    step = lambda h, j: h * n_tiles + j
    blk = pl.BlockSpec((None, th, B_HEAD_DIM), lambda h, j: (h, j, 0))
    a_spec = pl.BlockSpec((bb, A_HEADS, A_HEAD_DIM), lambda h, j: (step(h, j), 0, 0))
    headspec = pl.BlockSpec((bb, B_HEADS, B_HEAD_DIM), lambda h, j: (step(h, j), 0, 0))
    c_spec = pl.BlockSpec((None, bb, A_HEADS, A_HEAD_DIM, w_c), lambda h, j: (layer, step(h, j), 0, 0, 0))
    st_spec = pl.BlockSpec((None, bb, B_HEADS, B_HEAD_DIM, B_HEAD_DIM), lambda h, j: (layer, step(h, j), 0, 0, 0))
    a_heads = lambda a: a.reshape(n, A_HEADS, A_HEAD_DIM)
    heads = lambda a: a.reshape(n, B_HEADS, B_HEAD_DIM)
    in_specs = [blk] * 4 + [pl.BlockSpec((None, 1, B_HEAD_DIM), lambda h, j: (layer, 0, h))]
    in_specs += [a_spec] * 3 + [c_spec] * 2 + [headspec] * 4 + [st_spec, _layer_block((B_HEADS, B_HEAD_DIM), layer)]
    args = [q, k, v, g, nrm, a_heads(dq), a_heads(dk), a_heads(dv), cache_kt, cache_vt,
            heads(dhq), heads(dhk), heads(dhv), heads(dhg), state, nrm.reshape(DEPTH, B_HEADS, B_HEAD_DIM)]
    assert len(args) == 5 + N_DECODE_IN
    aliases = {}
    if states_out is not None:
        aliases = {len(args): 4}
        in_specs.append(pl.BlockSpec(memory_space=pl.ANY))
        args.append(states_out)
    bn, st_p, a_out, dbn, states = pl.pallas_call(
        functools.partial(_recurrent_kernel, th, bb, w_c),
        grid=(B_HEADS, n_tiles),
        in_specs=in_specs,
        out_specs=[pl.BlockSpec((th, B_HEAD_DIM), lambda h, j: (j, h)),
                   pl.BlockSpec((None, B_HEAD_DIM, B_HEAD_DIM), lambda h, j: (h, 0, 0)),
                   a_spec, headspec, st_spec],
        out_shape=[jax.ShapeDtypeStruct((s_len, B_WIDTH), F32),
                   jax.ShapeDtypeStruct((B_HEADS, B_HEAD_DIM, B_HEAD_DIM), F32),
                   jax.ShapeDtypeStruct((n, A_HEADS, A_HEAD_DIM), F32),
                   jax.ShapeDtypeStruct((n, B_HEADS, B_HEAD_DIM), F32),
                   jax.ShapeDtypeStruct(state.shape, F32)],
        scratch_shapes=[pltpu.VMEM((B_HEAD_DIM, B_HEAD_DIM), F32), pltpu.VMEM((th, B_HEAD_DIM), F32),
                        pltpu.VMEM((B_UNROLL, B_CHUNK, B_CHUNK), BF16),
                        pltpu.VMEM((B_UNROLL, B_CHUNK, B_HEAD_DIM), F32)],
        input_output_aliases=aliases,
        compiler_params=_params(("arbitrary", "arbitrary")),
        name="recurrent",
    )(*args)
    return bn, st_p, a_out.reshape(n, A_WIDTH).astype(BF16), dbn.reshape(n, B_WIDTH), states


def _mix(x, a16, bn, pre, post, wg0, wg1, wa, wb, wo):
    h = _rms(x, pre).astype(BF16)
    z0 = _dot(h, wg0)
    z1 = _dot(h, wg1)
    split = W_BLOCK - B_WIDTH
    g_b = z0[:, :B_WIDTH]
    gate_a = jnp.concatenate([z0[:, B_WIDTH:], z1[:, :split]], axis=1)
    gate_b = z1[:, split:]
    b_out = (bn * (g_b * jax.nn.sigmoid(g_b))).astype(BF16)
    mix = (jax.nn.sigmoid(gate_a) * _dot(a16, wa) + jax.nn.sigmoid(gate_b) * _dot(b_out, wb)).astype(BF16)
    return x + _rms(_dot(mix, wo), post)


def _mixout_kernel(x_ref, a_ref, bn_ref, pre_ref, post_ref, wg0_ref, wg1_ref, wa_ref, wb_ref, wo_ref, o_ref):
    for sl in _row_parts(x_ref.shape[0]):
        o_ref[sl, :] = _mix(x_ref[sl, :], a_ref[sl, :], bn_ref[sl, :], pre_ref[...], post_ref[...],
                            wg0_ref[...], wg1_ref[...], wa_ref[...], wb_ref[...], wo_ref[...])


def _mixout(x, a_out, bn, layer, pre, post, w_in, wa, wb, wo, tm):
    n = x.shape[0]
    row = lambda width: pl.BlockSpec((tm, width), lambda i: (i, 0))
    return pl.pallas_call(
        _mixout_kernel,
        grid=(n // tm,),
        in_specs=[row(D_MODEL), row(A_WIDTH), row(B_WIDTH), _vec(layer), _vec(layer), _col_block(3), _col_block(4),
                  _resident((A_WIDTH, D_MODEL)), _resident((B_WIDTH, D_MODEL)), _resident((D_MODEL, D_MODEL))],
        out_specs=row(D_MODEL),
        out_shape=jax.ShapeDtypeStruct((n, D_MODEL), F32),
        compiler_params=_params(("parallel",)),
        name="mixout",
    )(x, a_out, bn, pre, post, w_in, w_in, wa, wb, wo)


def _mixout_sample_kernel(x_ref, a_ref, bn_ref, pre_ref, post_ref, wg0_ref, wg1_ref, wa_ref, wb_ref, wo_ref,
                          o_ref, wa16_ref, wb16_ref, wo16_ref):
    wa, wb, wo = wa_ref[...].astype(BF16), wb_ref[...].astype(BF16), wo_ref[...].astype(BF16)
    wa16_ref[...] = wa
    wb16_ref[...] = wb
    wo16_ref[...] = wo
    o_ref[...] = _mix(x_ref[...], a_ref[...], bn_ref[...], pre_ref[...], post_ref[...],
                      wg0_ref[...], wg1_ref[...], wa, wb, wo)


def _mixout_sample(x, a_out, bn, layer, pre, post, w_in16, wa, wb, wo):
    n = x.shape[0]
    full = lambda *shape: pl.BlockSpec(shape, lambda i: (0,) * len(shape))
    shapes = ((A_WIDTH, D_MODEL), (B_WIDTH, D_MODEL), (D_MODEL, D_MODEL))
    return pl.pallas_call(
        _mixout_sample_kernel,
        grid=(1,),
        in_specs=[full(n, D_MODEL), full(n, A_WIDTH), full(n, B_WIDTH), _vec(layer), _vec(layer),
                  _col_block(3), _col_block(4)] + [_layer_block(s, layer) for s in shapes],
        out_specs=[full(n, D_MODEL)] + [full(*s) for s in shapes],
        out_shape=[jax.ShapeDtypeStruct((n, D_MODEL), F32)] + [jax.ShapeDtypeStruct(s, BF16) for s in shapes],
        compiler_params=_params(("arbitrary",)),
        name="mixout_sample",
    )(x, a_out, bn, pre, post, w_in16, w_in16, wa, wb, wo)


def _rope_tables(pos):
    half = ROPE_DIM // 2
    inv = ROPE_THETA ** (-jnp.arange(half, dtype=F32) / half)
    dim = jnp.arange(LANES) % A_HEAD_DIM
    ang = pos.astype(F32)[:, None] * inv[dim % half][None, :]
    cos, sin = jnp.cos(ang), jnp.sin(ang)
    lo, hi = (dim < half)[None, :], jnp.logical_and(dim >= half, dim < ROPE_DIM)[None, :]
    return (jnp.where(jnp.logical_or(lo, hi), cos, 1.0), jnp.where(lo, -sin, 0.0), jnp.where(hi, sin, 0.0))


def kernel(x_prompt, x_sample, cache_k, cache_v, state_hgrn, ffn1_norm_pre, ffn1_norm_post, ffn1_w_gate, ffn1_w_up, ffn1_w_down, mix_norm_pre, mix_norm_post, w_in, hgrn_lb_logits, hgrn_out_norm, w_a_out, w_b_out, w_mix_out, ffn2_norm_pre, ffn2_norm_post, ffn2_w_gate, ffn2_w_up, ffn2_w_down):
    batch, s_len, _ = x_prompt.shape
    n_dec, t_dec, _ = x_sample.shape
    assert batch == 1 and t_dec == 1 and s_len % ATTN_TILE == 0
    tm = 512
    tm_wide = 1024
    th = min(4096, s_len)
    cache_kt = jnp.transpose(cache_k, (0, 1, 3, 4, 2))
    cache_vt = jnp.transpose(cache_v, (0, 1, 3, 4, 2))
    yp = x_prompt.reshape(s_len, D_MODEL)
    ys = x_sample.reshape(n_dec, D_MODEL)
    tabs_p = _rope_tables(jnp.arange(s_len))
    tabs_s = _rope_tables(jnp.full((n_dec,), PAST_LEN))
    lbl = hgrn_lb_logits.astype(F32)
    vecs = lambda a: a.reshape(DEPTH, 1, -1).astype(F32)
    w_len = min(A_MAX_WINDOW, s_len)
    f1_pre, f1_post, f2_pre, f2_post = (vecs(a) for a in (ffn1_norm_pre, ffn1_norm_post, ffn2_norm_pre, ffn2_norm_post))
    m_pre, m_post, nrm = vecs(mix_norm_pre), vecs(mix_norm_post), vecs(hgrn_out_norm)
    kp, vp, sp, ksn, vsn = [], [], [], [], []
    states = None
    for l in range(DEPTH):
        ys, *f1 = _ffn_sample(ys, l, f1_pre, f1_post, ffn1_w_gate, ffn1_w_up, ffn1_w_down)
        qa, ka, va, sq, sk, sv, sg, win = _inproj_sample(ys, m_pre, tabs_s, lbl, w_in, l)
        yp = _ffn(yp, l, f1_pre, f1_post, *f1, tm_wide)
        q1, q4, q16, k1, k4, k16, v1, v4, v16, kf, vf = _inproj_attn(yp, m_pre, tabs_p, win, l, tm)
        hq, hk, hv, hg = _inproj_rec(yp, m_pre, lbl, win, l, tm)
        a_out = _attn_prompt(q1, q4, q16, k1, k4, k16, v1, v4, v16)
        bn, st_p, a_s, bn_s, states = _recurrent(l, hq, hk, hv, hg, nrm, qa, ka, va, cache_kt, cache_vt,
                                                 sq, sk, sv, sg, state_hgrn, states, th)
        ys, *mo = _mixout_sample(ys, a_s, bn_s, l, m_pre, m_post, win, w_a_out, w_b_out, w_mix_out)
        ys, *f2 = _ffn_sample(ys, l, f2_pre, f2_post, ffn2_w_gate, ffn2_w_up, ffn2_w_down)
        yp = _mixout(yp, a_out, bn, l, m_pre, m_post, win, *mo, tm_wide)
        yp = _ffn(yp, l, f2_pre, f2_post, *f2, tm_wide)
        kp.append(kf.reshape(batch, w_len, A_HEADS, A_HEAD_DIM))
        vp.append(vf.reshape(batch, w_len, A_HEADS, A_HEAD_DIM))
        sp.append(st_p.reshape(batch, B_HEADS, B_HEAD_DIM, B_HEAD_DIM))
        ksn.append(ka.reshape(n_dec, t_dec, A_HEADS, A_HEAD_DIM))
        vsn.append(va.reshape(n_dec, t_dec, A_HEADS, A_HEAD_DIM))
    return (yp.reshape(batch, s_len, D_MODEL), ys.reshape(n_dec, t_dec, D_MODEL),
            jnp.stack(kp), jnp.stack(vp), jnp.stack(sp), jnp.stack(ksn), jnp.stack(vsn), states)
```

```python
import functools

import jax
import jax.numpy as jnp
from jax import lax
from jax.experimental import pallas as pl
from jax.experimental.pallas import tpu as pltpu

F32 = jnp.float32
BF16 = jnp.bfloat16

D_MODEL = 1024
DEPTH = 2
PAST_LEN = 16384
A_HEADS = 8
A_HEAD_DIM = 64
A_WIDTH = A_HEADS * A_HEAD_DIM
A_DILATIONS = (1, 4, 16)
A_SPAN = 128
A_MAX_WINDOW = 2048
A_SCALE = A_HEAD_DIM ** -0.5
ROPE_THETA = 500000.0
ROPE_DIM = A_HEAD_DIM // 4
B_HEADS = 8
B_HEAD_DIM = 128
B_WIDTH = B_HEADS * B_HEAD_DIM
B_CHUNK = 128
B_UNROLL = 32
B_SCALE = B_HEAD_DIM ** -0.5
D_FF = 2816
FF_CHUNK = 256
ROW_PARTS = 4
EPS = 1e-6
LOG2_E = 1.4426950408889634

W_BLOCK = 3 * A_WIDTH
assert 3 * B_WIDTH == 2 * W_BLOCK and B_WIDTH + 2 * D_MODEL == 2 * W_BLOCK
LANES = 128
N_PAIRS = A_WIDTH // LANES
ATTN_TILE = A_SPAN * max(A_DILATIONS)
A_UNROLL = 16
VMEM_LIMIT = 56 * 1024 * 1024


def _params(sem, vmem=VMEM_LIMIT):
    return pltpu.CompilerParams(dimension_semantics=sem, vmem_limit_bytes=vmem)


def _resident(shape):
    nd = len(shape)
    return pl.BlockSpec(shape, lambda *_: (0,) * nd, pipeline_mode=pl.Buffered(1))


def _layer_block(shape, layer, col=0):
    index = (layer,) + (0,) * (len(shape) - 1) + (col,)
    return pl.BlockSpec((None,) + tuple(shape), lambda *_: index, pipeline_mode=pl.Buffered(1))


def _vec(layer):
    return _layer_block((1, D_MODEL), layer)


def _rms(x, g):
    y = x * lax.rsqrt(jnp.mean(x * x, axis=-1, keepdims=True) + EPS)
    return y * g


def _dot(a, b):
    return jnp.dot(a, b, preferred_element_type=F32)


def _dot_nt(a, b):
    return lax.dot_general(a, b, (((1,), (1,)), ((), ())), preferred_element_type=F32)


def _dot_tn(a, b):
    return lax.dot_general(a, b, (((0,), (0,)), ((), ())), preferred_element_type=F32)


def _row_parts(rows):
    parts = ROW_PARTS if rows % (8 * ROW_PARTS) == 0 else 1
    return [pl.ds(p * (rows // parts), rows // parts) for p in range(parts)]


def _ffn_kernel(x_ref, pre_ref, post_ref, wg_ref, wu_ref, wd_ref, o_ref):
    for sl in _row_parts(x_ref.shape[0]):
        x = x_ref[sl, :]
        h = _rms(x, pre_ref[...]).astype(BF16)
        g = _dot(h, wg_ref[...])
        u = _dot(h, wu_ref[...])
        a = (g * jax.nn.sigmoid(g) * u).astype(BF16)
        y = _dot(a, wd_ref[...])
        o_ref[sl, :] = x + 0.5 * _rms(y, post_ref[...])


def _ffn(x, layer, pre, post, wg, wu, wd, tm):
    n = x.shape[0]
    row = pl.BlockSpec((tm, D_MODEL), lambda i: (i, 0))
    return pl.pallas_call(
        _ffn_kernel,
        grid=(n // tm,),
        in_specs=[row, _vec(layer), _vec(layer), _resident((D_MODEL, D_FF)),
                  _resident((D_MODEL, D_FF)), _resident((D_FF, D_MODEL))],
        out_specs=row,
        out_shape=jax.ShapeDtypeStruct((n, D_MODEL), F32),
        compiler_params=_params(("parallel",)),
        name="ffn",
    )(x, pre, post, wg, wu, wd)


def _ffn_sample_kernel(x_ref, pre_ref, post_ref, wg_ref, wu_ref, wd_ref,
                       o_ref, wg16_ref, wu16_ref, wd16_ref, h_scr, acc_scr):
    c = pl.program_id(0)

    @pl.when(c == 0)
    def _():
        h_scr[...] = _rms(x_ref[...], pre_ref[...]).astype(BF16)
        acc_scr[...] = jnp.zeros_like(acc_scr)

    wg, wu, wd = wg_ref[...].astype(BF16), wu_ref[...].astype(BF16), wd_ref[...].astype(BF16)
    wg16_ref[...] = wg
    wu16_ref[...] = wu
    wd16_ref[...] = wd
    h = h_scr[...]
    g = _dot(h, wg)
    u = _dot(h, wu)
    acc_scr[...] += _dot((g * jax.nn.sigmoid(g) * u).astype(BF16), wd)

    @pl.when(c == pl.num_programs(0) - 1)
    def _():
        o_ref[...] = x_ref[...] + 0.5 * _rms(acc_scr[...], post_ref[...])


def _ffn_sample(x, layer, pre, post, wg, wu, wd):
    n = x.shape[0]
    full = pl.BlockSpec((n, D_MODEL), lambda c: (0, 0))
    col = lambda: pl.BlockSpec((None, D_MODEL, FF_CHUNK), lambda c: (layer, 0, c))
    col16 = lambda: pl.BlockSpec((D_MODEL, FF_CHUNK), lambda c: (0, c))
    return pl.pallas_call(
        _ffn_sample_kernel,
        grid=(D_FF // FF_CHUNK,),
        in_specs=[full, _vec(layer), _vec(layer), col(), col(),
                  pl.BlockSpec((None, FF_CHUNK, D_MODEL), lambda c: (layer, c, 0))],
        out_specs=[full, col16(), col16(), pl.BlockSpec((FF_CHUNK, D_MODEL), lambda c: (c, 0))],
        out_shape=[jax.ShapeDtypeStruct((n, D_MODEL), F32), jax.ShapeDtypeStruct((D_MODEL, D_FF), BF16),
                   jax.ShapeDtypeStruct((D_MODEL, D_FF), BF16), jax.ShapeDtypeStruct((D_FF, D_MODEL), BF16)],
        scratch_shapes=[pltpu.VMEM((n, D_MODEL), BF16), pltpu.VMEM((n, D_MODEL), F32)],
        compiler_params=_params(("arbitrary",)),
        name="ffn_sample",
    )(x, pre, post, wg, wu, wd)


def _lower_bound(logits, layer):
    e = jnp.exp(logits - jnp.max(logits, axis=0, keepdims=True))
    sm = e / jnp.sum(e, axis=0, keepdims=True)
    lb = jnp.zeros((1, B_WIDTH), F32)
    for i in range(1, layer + 1):
        lb = lb + sm[i:i + 1, :]
    return lb


def _rope_slab(t, cos, sin_lo, sin_hi):
    return t * cos + pltpu.roll(t, LANES - ROPE_DIM // 2, axis=1) * sin_lo + pltpu.roll(t, ROPE_DIM // 2, axis=1) * sin_hi


def _col_block(col):
    return pl.BlockSpec((D_MODEL, W_BLOCK), lambda *_: (0, col), pipeline_mode=pl.Buffered(1))


def _hgrn_gates(z0, z1, lb):
    split = W_BLOCK - B_WIDTH
    q = z0[:, :B_WIDTH] * B_SCALE
    f_raw = jnp.concatenate([z0[:, B_WIDTH:], z1[:, :split]], axis=1)
    i_raw = z1[:, split:]
    f = lb + (1.0 - lb) * jax.nn.sigmoid(f_raw)
    return q, 1.0 - f, i_raw * jax.nn.sigmoid(i_raw), jnp.log(f)


def _inproj_attn_kernel(tm, x_ref, pre_ref, cos_ref, slo_ref, shi_ref, wa_ref,
                        q1_ref, q4_ref, q16_ref, k1_ref, k4_ref, k16_ref, v1_ref, v4_ref, v16_ref,
                        kf_ref, vf_ref, stage_ref, stage4_ref):
    h = _rms(x_ref[...], pre_ref[...]).astype(BF16)
    za = _dot(h, wa_ref[...])
    cos, slo, shi = cos_ref[...], slo_ref[...], shi_ref[...]
    outs = ((q1_ref, q4_ref, q16_ref), (k1_ref, k4_ref, k16_ref), (v1_ref, v4_ref, v16_ref))
    for s in range(3 * N_PAIRS):
        kind, hp = divmod(s, N_PAIRS)
        t = za[:, s * LANES:(s + 1) * LANES]
        if kind < 2:
            t = _rope_slab(t, cos, slo, shi)
        if kind == 0:
            t = t * (A_SCALE * LOG2_E)
        if kind == 1:
            kf_ref[:, hp * LANES:(hp + 1) * LANES] = t
        if kind == 2:
            vf_ref[:, hp * LANES:(hp + 1) * LANES] = t
        stage_ref[s] = t
        o1, o4, o16 = outs[kind]
        o1[hp] = t.astype(BF16)
        for r in range(4):
            c4 = stage_ref[s, pl.ds(r, tm // 4, stride=4), :]
            o4[hp, r] = c4.astype(BF16)
            stage4_ref[s, r] = c4
        for r in range(4):
            for a in range(4):
                o16[hp, r + 4 * a] = stage4_ref[s, r, pl.ds(a, tm // 16, stride=4), :].astype(BF16)


def _inproj_attn(x, pre, tabs, w_in, layer, tm):
    s_len = x.shape[0]
    w_len = min(A_MAX_WINDOW, s_len)
    first = (s_len - w_len) // tm
    row = pl.BlockSpec((tm, D_MODEL), lambda i: (i, 0))
    tab = pl.BlockSpec((tm, LANES), lambda i: (i, 0))
    l1 = pl.BlockSpec((N_PAIRS, tm, LANES), lambda i: (0, i, 0))
    l4 = pl.BlockSpec((N_PAIRS, 4, tm // 4, LANES), lambda i: (0, 0, i, 0))
    l16 = pl.BlockSpec((N_PAIRS, 16, tm // 16, LANES), lambda i: (0, 0, i, 0))
    win = pl.BlockSpec((tm, A_WIDTH), lambda i: (jnp.maximum(i - first, 0), 0))
    s1 = jax.ShapeDtypeStruct((N_PAIRS, s_len, LANES), BF16)
    s4 = jax.ShapeDtypeStruct((N_PAIRS, 4, s_len // 4, LANES), BF16)
    s16 = jax.ShapeDtypeStruct((N_PAIRS, 16, s_len // 16, LANES), BF16)
    sw = jax.ShapeDtypeStruct((w_len, A_WIDTH), F32)
    return pl.pallas_call(
        functools.partial(_inproj_attn_kernel, tm),
        grid=(s_len // tm,),
        in_specs=[row, _vec(layer), tab, tab, tab, _col_block(0)],
        out_specs=[l1, l4, l16, l1, l4, l16, l1, l4, l16, win, win],
        out_shape=[s1, s4, s16, s1, s4, s16, s1, s4, s16, sw, sw],
        scratch_shapes=[pltpu.VMEM((3 * N_PAIRS, tm, LANES), F32), pltpu.VMEM((3 * N_PAIRS, 4, tm // 4, LANES), F32)],
        compiler_params=_params(("arbitrary",)),
        name="inproj_attn",
    )(x, pre, *tabs, w_in)


def _inproj_rec_kernel(layer, x_ref, pre_ref, lbl_ref, wb0_ref, wb1_ref, hq_ref, hk_ref, hv_ref, hg_ref):
    lb = _lower_bound(lbl_ref[...], layer)
    for rows in _row_parts(x_ref.shape[0]):
        h = _rms(x_ref[rows, :], pre_ref[...]).astype(BF16)
        q, k, v, g = _hgrn_gates(_dot(h, wb0_ref[...]), _dot(h, wb1_ref[...]), lb)
        for hd in range(B_HEADS):
            sl = slice(hd * B_HEAD_DIM, (hd + 1) * B_HEAD_DIM)
            hq_ref[hd, rows, :] = q[:, sl]
            hk_ref[hd, rows, :] = k[:, sl]
            hv_ref[hd, rows, :] = v[:, sl]
            hg_ref[hd, rows, :] = g[:, sl]


def _inproj_rec(x, pre, lbl, w_in, layer, tm):
    s_len = x.shape[0]
    row = pl.BlockSpec((tm, D_MODEL), lambda i: (i, 0))
    hd = pl.BlockSpec((B_HEADS, tm, B_HEAD_DIM), lambda i: (0, i, 0))
    sh = jax.ShapeDtypeStruct((B_HEADS, s_len, B_HEAD_DIM), F32)
    return pl.pallas_call(
        functools.partial(_inproj_rec_kernel, layer),
        grid=(s_len // tm,),
        in_specs=[row, _vec(layer), _resident((DEPTH, B_WIDTH)), _col_block(1), _col_block(2)],
        out_specs=[hd] * 4,
        out_shape=[sh] * 4,
        compiler_params=_params(("parallel",)),
        name="inproj_rec",
    )(x, pre, lbl, w_in, w_in)


def _inproj_sample_kernel(layer, x_ref, pre_ref, cos_ref, slo_ref, shi_ref, lbl_ref, w_ref,
                          qa_ref, ka_ref, va_ref, hq_ref, hk_ref, hv_ref, hg_ref, w16_ref, h_scr, z_scr):
    c = pl.program_id(0)

    @pl.when(c == 0)
    def _():
        h_scr[...] = _rms(x_ref[...], pre_ref[...]).astype(BF16)

    w = w_ref[...].astype(BF16)
    w16_ref[...] = w

    @pl.when(c == 0)
    def _():
        za = _dot(h_scr[...], w)
        cos, slo, shi = cos_ref[...], slo_ref[...], shi_ref[...]
        outs = (qa_ref, ka_ref, va_ref)
        for s in range(3 * N_PAIRS):
            kind, hp = divmod(s, N_PAIRS)
            t = za[:, s * LANES:(s + 1) * LANES]
            if kind < 2:
                t = _rope_slab(t, cos, slo, shi)
            if kind == 0:
                t = t * A_SCALE
            outs[kind][:, hp * LANES:(hp + 1) * LANES] = t

    @pl.when(c == 1)
    def _():
        z_scr[...] = _dot(h_scr[...], w)

    @pl.when(c == 2)
    def _():
        q, k, v, g = _hgrn_gates(z_scr[...], _dot(h_scr[...], w), _lower_bound(lbl_ref[...], layer))
        hq_ref[...] = q
        hk_ref[...] = k
        hv_ref[...] = v
        hg_ref[...] = g


def _inproj_sample(x, pre, tabs, lbl, w_in, layer):
    n = x.shape[0]
    n_col = w_in.shape[-1] // W_BLOCK
    full = lambda *shape: pl.BlockSpec(shape, lambda c: (0,) * len(shape))
    sa = jax.ShapeDtypeStruct((n, A_WIDTH), F32)
    sb = jax.ShapeDtypeStruct((n, B_WIDTH), F32)
    return pl.pallas_call(
        functools.partial(_inproj_sample_kernel, layer),
        grid=(n_col,),
        in_specs=[full(n, D_MODEL), _vec(layer), full(n, LANES), full(n, LANES), full(n, LANES),
                  full(DEPTH, B_WIDTH), pl.BlockSpec((None, D_MODEL, W_BLOCK), lambda c: (layer, 0, c))],
        out_specs=[full(n, A_WIDTH)] * 3 + [full(n, B_WIDTH)] * 4
                  + [pl.BlockSpec((D_MODEL, W_BLOCK), lambda c: (0, c))],
        out_shape=[sa] * 3 + [sb] * 4 + [jax.ShapeDtypeStruct((D_MODEL, n_col * W_BLOCK), BF16)],
        scratch_shapes=[pltpu.VMEM((n, D_MODEL), BF16), pltpu.VMEM((n, W_BLOCK), F32)],
        compiler_params=_params(("arbitrary",)),
        name="inproj_sample",
    )(x, pre, *tabs, lbl, w_in)


def _attn_scores(q, k, bias):
    first = lax.broadcasted_iota(jnp.int32, (A_SPAN, LANES), 1) < A_HEAD_DIM
    zero = jnp.zeros_like(q)
    q2 = jnp.concatenate([jnp.where(first, q, zero), jnp.where(first, zero, q)], axis=0)
    s = _dot_nt(q2, k) + bias
    m = jnp.max(jnp.maximum(s[:, :A_SPAN], s[:, A_SPAN:]), axis=-1, keepdims=True)
    return jnp.exp2(s - m).astype(BF16), jnp.broadcast_to(m, (2 * A_SPAN, LANES))


def _attn_values(p, m, v_ext):
    first = lax.broadcasted_iota(jnp.int32, (A_SPAN, LANES), 1) < A_HEAD_DIM
    oe = _dot(p, v_ext)
    pick = lambda t: jnp.where(first, t[:A_SPAN], t[A_SPAN:])
    return pick(oe[:, :LANES]), pick(m), pick(oe[:, LANES:])


def _attn_prompt_kernel(q1_ref, q4_ref, q16_ref, k1_ref, k4_ref, k16_ref, v1_ref, v4_ref, v16_ref,
                        o_ref, ks1, ks4, ks16, vs1, vs4, vs16, bias_scr, o_scr, m_scr, d_scr, p_scr, mx_scr):
    j = pl.program_id(1)
    q_refs = (q1_ref, q4_ref, q16_ref)
    k_in = (k1_ref, k4_ref, k16_ref)
    v_in = (v1_ref, v4_ref, v16_ref)
    k_scr = (ks1, ks4, ks16)
    v_scr = (vs1, vs4, vs16)

    @pl.when(j == 0)
    def _():
        for scr in k_scr:
            scr[:, 0:A_SPAN, :] = jnp.zeros((scr.shape[0], A_SPAN, LANES), BF16)
        for scr in v_scr:
            scr[:, 0:A_SPAN, 0:LANES] = jnp.zeros((scr.shape[0], A_SPAN, LANES), BF16)
            scr[:, :, LANES:] = jnp.ones((scr.shape[0], scr.shape[1], LANES), BF16)
        qi = lax.broadcasted_iota(jnp.int32, (2 * A_SPAN, 2 * A_SPAN), 0) % A_SPAN
        ki = lax.broadcasted_iota(jnp.int32, (2 * A_SPAN, 2 * A_SPAN), 1)
        band = jnp.logical_and(ki >= qi, ki <= qi + A_SPAN)
        bias_scr[1] = jnp.where(band, 0.0, -jnp.inf)
        bias_scr[0] = jnp.where(jnp.logical_and(band, ki >= A_SPAN), 0.0, -jnp.inf)

    zero = jnp.minimum(j, 0)

    def stage_tile(_, carry):
        for p, d in enumerate(A_DILATIONS):
            k_scr[p][:, A_SPAN:, :] = k_in[p][...]
            v_scr[p][:, A_SPAN:, 0:LANES] = v_in[p][...]
        return carry

    lax.fori_loop(0, 1 + zero, stage_tile, 0)

    def locate(p, grp, u):
        per_class = ATTN_TILE // A_DILATIONS[p] // A_SPAN
        b = grp * A_UNROLL + u
        n = b % per_class
        return b // per_class, pl.multiple_of(n * A_SPAN, A_SPAN), n

    def front(p, grp):
        for u in range(A_UNROLL):
            r, lo, n = locate(p, grp, u)
            has_prev = jnp.logical_or(j > 0, n > 0).astype(jnp.int32)
            p_scr[u], mx_scr[u] = _attn_scores(q_refs[p][r, pl.ds(lo, A_SPAN), :],
                                               k_scr[p][r, pl.ds(lo, 2 * A_SPAN), :], bias_scr[has_prev])

    def back(p, grp):
        d = A_DILATIONS[p]
        for u in range(A_UNROLL):
            r, lo, n = locate(p, grp, u)
            acc, m, den = _attn_values(p_scr[u], mx_scr[u], v_scr[p][r, pl.ds(lo, 2 * A_SPAN), :])
            if d > 1:
                rows = pl.ds(n * (A_SPAN * d) + r, A_SPAN, stride=d)
                o_scr[p - 1, rows, :] = acc
                m_scr[p - 1, rows, :] = m
                d_scr[p - 1, rows, :] = den
            else:
                rows = pl.ds(lo, A_SPAN)
                m1, m2 = m_scr[0, rows, :], m_scr[1, rows, :]
                top = jnp.maximum(jnp.maximum(m, m1), m2)
                w0, w1, w2 = jnp.exp2(m - top), jnp.exp2(m1 - top), jnp.exp2(m2 - top)
                num = w0 * acc + w1 * o_scr[0, rows, :] + w2 * o_scr[1, rows, :]
                tot = w0 * den + w1 * d_scr[0, rows, :] + w2 * d_scr[1, rows, :]
                o_ref[rows, :] = (num / tot).astype(BF16)

    n_grp = ATTN_TILE // A_SPAN // A_UNROLL
    order = tuple(range(1, len(A_DILATIONS))) + (0,)
    front(order[0], 0)
    for idx, p in enumerate(order):
        if n_grp > 1:

            def steady(grp, carry, p=p):
                back(p, grp - 1)
                front(p, grp)
                return carry

            lax.fori_loop(1, n_grp + zero, steady, 0)
        if idx + 1 < len(order):
            back(p, n_grp - 1)
            front(order[idx + 1], 0)

    def key_tails(_, carry):
        for p, d in enumerate(A_DILATIONS):
            tail = ATTN_TILE // d
            k_scr[p][:, 0:A_SPAN, :] = k_scr[p][:, tail:tail + A_SPAN, :]
        return carry

    lax.fori_loop(0, 1 + zero, key_tails, 0)
    back(order[-1], n_grp - 1)

    def value_tails(_, carry):
        for p, d in enumerate(A_DILATIONS):
            tail = ATTN_TILE // d
            v_scr[p][:, 0:A_SPAN, 0:LANES] = v_scr[p][:, tail:tail + A_SPAN, 0:LANES]
        return carry

    lax.fori_loop(0, 1 + zero, value_tails, 0)


def _attn_prompt(q1, q4, q16, k1, k4, k16, v1, v4, v16):
    s_len = q1.shape[1]
    t = ATTN_TILE
    b1 = pl.BlockSpec((None, 1, t, LANES), lambda hp, j: (hp, 0, j, 0))
    b4 = pl.BlockSpec((None, 4, t // 4, LANES), lambda hp, j: (hp, 0, j, 0))
    b16 = pl.BlockSpec((None, 16, t // 16, LANES), lambda hp, j: (hp, 0, j, 0))
    as4 = lambda a: a.reshape(N_PAIRS, 1, s_len, LANES)
    scr = lambda d, width: pltpu.VMEM((d, A_SPAN + t // d, width), BF16)
    n_dil = len(A_DILATIONS) - 1
    return pl.pallas_call(
        _attn_prompt_kernel,
        grid=(N_PAIRS, s_len // t),
        in_specs=[b1, b4, b16] * 3,
        out_specs=pl.BlockSpec((t, LANES), lambda hp, j: (j, hp)),
        out_shape=jax.ShapeDtypeStruct((s_len, A_WIDTH), BF16),
        scratch_shapes=[scr(d, LANES) for d in A_DILATIONS] + [scr(d, 2 * LANES) for d in A_DILATIONS]
                       + [pltpu.VMEM((2, 2 * A_SPAN, 2 * A_SPAN), F32)] + [pltpu.VMEM((n_dil, t, LANES), F32)] * 3
                       + [pltpu.VMEM((A_UNROLL, 2 * A_SPAN, 2 * A_SPAN), BF16),
                          pltpu.VMEM((A_UNROLL, 2 * A_SPAN, LANES), F32)],
        compiler_params=_params(("arbitrary", "arbitrary")),
        name="attn_prompt",
    )(as4(q1), q4, q16, as4(k1), k4, k16, as4(v1), v4, v16)


def _decode_rows(bb, w_c, q_ref, k_ref, v_ref, kt_ref, vt_ref,
                 hq_ref, hk_ref, hv_ref, hg_ref, st_ref, nrm_ref, a_ref, bn_ref, so_ref):
    n_pat = len(A_DILATIONS)
    dist = w_c - lax.broadcasted_iota(jnp.int32, (A_HEADS, w_c), 1)
    cnt = jnp.zeros((A_HEADS, w_c), F32)
    for d in A_DILATIONS:
        cnt = cnt + jnp.logical_and(dist % d == 0, dist <= A_SPAN * d).astype(F32)
    used = cnt > 0.0
    head_s = lax.broadcasted_iota(jnp.int32, (A_HEADS, w_c), 0)
    head_o = lax.broadcasted_iota(jnp.int32, (A_HEADS, A_HEAD_DIM), 0)
    for b in range(bb):
        q = q_ref[b]
        k_new, v_new = k_ref[b], v_ref[b]
        q16 = q.astype(BF16)
        s_new = jnp.sum(q * k_new, axis=-1, keepdims=True)
        s = jnp.zeros((A_HEADS, w_c), F32)
        for h in range(A_HEADS):
            s = jnp.where(head_s == h, _dot(q16, kt_ref[b, h].astype(BF16)), s)
        m = jnp.maximum(jnp.max(jnp.where(used, s, -jnp.inf), axis=-1, keepdims=True), s_new)
        w = jnp.where(used, jnp.exp(s - m), 0.0) * cnt
        p_new = n_pat * jnp.exp(s_new - m)
        den = jnp.sum(w, axis=-1, keepdims=True) + p_new
        acc = p_new * v_new
        w16 = w.astype(BF16)
        for h in range(A_HEADS):
            acc = acc + jnp.where(head_o == h, _dot_nt(w16, vt_ref[b, h].astype(BF16)), 0.0)
        a_ref[b] = acc / den
        v_rows = hv_ref[b]
        packed = jnp.concatenate([hq_ref[b], hk_ref[b], jnp.exp(hg_ref[b]),
                                  jnp.zeros((LANES - 3 * B_HEADS, B_HEAD_DIM), F32)], axis=0)
        cols = packed.T
        outs = []
        for hd in range(B_HEADS):
            col = lambda i: cols[:, i * B_HEADS + hd:i * B_HEADS + hd + 1]
            st = col(2) * st_ref[b, hd] + col(1) * v_rows[hd:hd + 1, :]
            so_ref[b, hd] = st
            o = jnp.sum(col(0) * st, axis=0, keepdims=True)
            outs.append(_rms(o, nrm_ref[hd:hd + 1, :]))
        bn_ref[b] = jnp.concatenate(outs, axis=0)


N_DECODE_IN = 11


def _recurrent_kernel(th, bb, w_c, q_ref, k_ref, v_ref, g_ref, nrm_ref, *rest):
    dec_in = rest[:N_DECODE_IN]
    o_ref, st_ref, da_ref, dbn_ref, dso_ref, state_scr, b_scr, a_scr, part_scr = rest[-9:]
    j = pl.program_id(1)

    @pl.when(j == 0)
    def _():
        state_scr[...] = jnp.zeros_like(state_scr)

    piece = B_CHUNK * B_UNROLL
    rin = lax.broadcasted_iota(jnp.int32, (piece, B_HEAD_DIM), 0) & (B_CHUNK - 1)

    def scan_piece(i, carry):
        rows = pl.ds(pl.multiple_of(i * piece, piece), piece)
        b = g_ref[rows, :] * LOG2_E
        shift = 1
        while shift < B_CHUNK:
            b = b + jnp.where(rin >= shift, pltpu.roll(b, shift, axis=0), 0.0)
            shift *= 2
        b_scr[rows, :] = b
        return carry

    lax.fori_loop(0, th // piece, scan_piece, 0)

    row = lax.broadcasted_iota(jnp.int32, (B_CHUNK, B_HEAD_DIM), 0)
    a_t = lax.broadcasted_iota(jnp.int32, (B_CHUNK, B_CHUNK), 0)
    a_s = lax.broadcasted_iota(jnp.int32, (B_CHUNK, B_CHUNK), 1)
    levels = []
    half = B_CHUNK // 2
    while half >= 1:
        seg = 2 * half
        pair = jnp.logical_and(a_t // seg == a_s // seg,
                               jnp.logical_and(a_t % seg >= half, a_s % seg < half))
        upper = row % seg >= half
        levels.append((half, upper, jnp.where(upper, 1.0, -1.0), pair))
        half //= 2
    n_grp = B_CHUNK // 8
    sub8 = lax.broadcasted_iota(jnp.int32, (n_grp, 8, B_HEAD_DIM), 1)
    nrm = nrm_ref[...]

    def centre_value(bc, half):
        seg = 2 * half
        if seg >= 8:
            return jnp.concatenate(
                [jnp.broadcast_to(bc[s0 + half - 1:s0 + half, :], (seg, B_HEAD_DIM))
                 for s0 in range(0, B_CHUNK, seg)], axis=0)
        b3 = bc.reshape(n_grp, 8, B_HEAD_DIM)
        if half == 1:
            out = jnp.where(sub8 % 2 == 0, b3, pltpu.roll(b3, 1, axis=1))
        else:
            out = jnp.broadcast_to(b3[:, half - 1:half, :], b3.shape)
            for s0 in range(seg, 8, seg):
                out = jnp.where(sub8 >= s0, jnp.broadcast_to(b3[:, s0 + half - 1:s0 + half, :], b3.shape), out)
        return out.reshape(B_CHUNK, B_HEAD_DIM)

    def intra_chunk(q, k, v, bc):
        a = jnp.zeros((B_CHUNK, B_CHUNK), F32)
        for half, upper, sign, pair in levels:
            x = (jnp.where(upper, q, k) * jnp.exp2((bc - centre_value(bc, half)) * sign)).astype(BF16)
            a = jnp.where(pair, _dot_nt(x, x), a)
        return a, jnp.sum(q * k, axis=-1, keepdims=True) * v

    def chunk_rows(grp, u):
        return pl.ds(pl.multiple_of((grp * B_UNROLL + u) * B_CHUNK, B_CHUNK), B_CHUNK)

    def front(grp):
        st = state_scr[...]
        for u in range(B_UNROLL):
            rows = chunk_rows(grp, u)
            q, k, v, bc = q_ref[rows, :], k_ref[rows, :], v_ref[rows, :], b_scr[rows, :]
            o = _dot_nt((q * jnp.exp2(bc)).astype(BF16), st.astype(BF16))
            b_last = bc[B_CHUNK - 1:B_CHUNK, :]
            k_dec = (k * jnp.exp2(b_last - bc)).astype(BF16)
            st = st * jnp.exp2(b_last) + _dot_tn(v.astype(BF16), k_dec)
            a, o3 = intra_chunk(q, k, v, bc)
            a_scr[u] = a.astype(BF16)
            part_scr[u] = o + o3
        state_scr[...] = st

    def back(grp):
        for u in range(B_UNROLL):
            rows = chunk_rows(grp, u)
            o = part_scr[u] + _dot(a_scr[u], v_ref[rows, :].astype(BF16))
            o_ref[rows, :] = _rms(o, nrm)

    n_groups = th // (B_CHUNK * B_UNROLL)
    _decode_rows(bb, w_c, *dec_in, da_ref, dbn_ref, dso_ref)
    front(0)

    def steady(grp, carry):
        back(grp - 1)
        front(grp)
        return carry

    lax.fori_loop(1, n_groups, steady, 0)
    back(n_groups - 1)

    @pl.when(j == pl.num_programs(1) - 1)
    def _():
        st_ref[...] = state_scr[...].T


def _recurrent(layer, q, k, v, g, nrm, dq, dk, dv, cache_kt, cache_vt, dhq, dhk, dhv, dhg, state, states_out, th):
    s_len = q.shape[1]
    n = dq.shape[0]
    w_c = cache_kt.shape[-1]
    n_tiles = s_len // th
    steps = B_HEADS * n_tiles
    assert w_c >= A_SPAN * max(A_DILATIONS) and n % steps == 0
    bb = n // steps
    step = lambda h, j: h * n_tiles + j
    blk = pl.BlockSpec((None, th, B_HEAD_DIM), lambda h, j: (h, j, 0))
    a_spec = pl.BlockSpec((bb, A_HEADS, A_HEAD_DIM), lambda h, j: (step(h, j), 0, 0))
    headspec = pl.BlockSpec((bb, B_HEADS, B_HEAD_DIM), lambda h, j: (step(h, j), 0, 0))
    c_spec = pl.BlockSpec((None, bb, A_HEADS, A_HEAD_DIM, w_c), lambda h, j: (layer, step(h, j), 0, 0, 0))
    st_spec = pl.BlockSpec((None, bb, B_HEADS, B_HEAD_DIM, B_HEAD_DIM), lambda h, j: (layer, step(h, j), 0, 0, 0))
    a_heads = lambda a: a.reshape(n, A_HEADS, A_HEAD_DIM)
    heads = lambda a: a.reshape(n, B_HEADS, B_HEAD_DIM)
    in_specs = [blk] * 4 + [pl.BlockSpec((None, 1, B_HEAD_DIM), lambda h, j: (layer, 0, h))]
    in_specs += [a_spec] * 3 + [c_spec] * 2 + [headspec] * 4 + [st_spec, _layer_block((B_HEADS, B_HEAD_DIM), layer)]
    args = [q, k, v, g, nrm, a_heads(dq), a_heads(dk), a_heads(dv), cache_kt, cache_vt,
            heads(dhq), heads(dhk), heads(dhv), heads(dhg), state, nrm.reshape(DEPTH, B_HEADS, B_HEAD_DIM)]
    assert len(args) == 5 + N_DECODE_IN
    aliases = {}
    if states_out is not None:
        aliases = {len(args): 4}
        in_specs.append(pl.BlockSpec(memory_space=pl.ANY))
        args.append(states_out)
    bn, st_p, a_out, dbn, states = pl.pallas_call(
        functools.partial(_recurrent_kernel, th, bb, w_c),
        grid=(B_HEADS, n_tiles),
        in_specs=in_specs,
        out_specs=[pl.BlockSpec((th, B_HEAD_DIM), lambda h, j: (j, h)),
                   pl.BlockSpec((None, B_HEAD_DIM, B_HEAD_DIM), lambda h, j: (h, 0, 0)),
                   a_spec, headspec, st_spec],
        out_shape=[jax.ShapeDtypeStruct((s_len, B_WIDTH), F32),
                   jax.ShapeDtypeStruct((B_HEADS, B_HEAD_DIM, B_HEAD_DIM), F32),
                   jax.ShapeDtypeStruct((n, A_HEADS, A_HEAD_DIM), F32),
                   jax.ShapeDtypeStruct((n, B_HEADS, B_HEAD_DIM), F32),
                   jax.ShapeDtypeStruct(state.shape, F32)],
        scratch_shapes=[pltpu.VMEM((B_HEAD_DIM, B_HEAD_DIM), F32), pltpu.VMEM((th, B_HEAD_DIM), F32),
                        pltpu.VMEM((B_UNROLL, B_CHUNK, B_CHUNK), BF16),
                        pltpu.VMEM((B_UNROLL, B_CHUNK, B_HEAD_DIM), F32)],
        input_output_aliases=aliases,
        compiler_params=_params(("arbitrary", "arbitrary")),
        name="recurrent",
    )(*args)
    return bn, st_p, a_out.reshape(n, A_WIDTH).astype(BF16), dbn.reshape(n, B_WIDTH), states


def _mix(x, a16, bn, pre, post, wg0, wg1, wa, wb, wo):
    h = _rms(x, pre).astype(BF16)
    z0 = _dot(h, wg0)
    z1 = _dot(h, wg1)
    split = W_BLOCK - B_WIDTH
    g_b = z0[:, :B_WIDTH]
    gate_a = jnp.concatenate([z0[:, B_WIDTH:], z1[:, :split]], axis=1)
    gate_b = z1[:, split:]
    b_out = (bn * (g_b * jax.nn.sigmoid(g_b))).astype(BF16)
    mix = (jax.nn.sigmoid(gate_a) * _dot(a16, wa) + jax.nn.sigmoid(gate_b) * _dot(b_out, wb)).astype(BF16)
    return x + _rms(_dot(mix, wo), post)


def _mixout_kernel(x_ref, a_ref, bn_ref, pre_ref, post_ref, wg0_ref, wg1_ref, wa_ref, wb_ref, wo_ref, o_ref):
    for sl in _row_parts(x_ref.shape[0]):
        o_ref[sl, :] = _mix(x_ref[sl, :], a_ref[sl, :], bn_ref[sl, :], pre_ref[...], post_ref[...],
                            wg0_ref[...], wg1_ref[...], wa_ref[...], wb_ref[...], wo_ref[...])


def _mixout(x, a_out, bn, layer, pre, post, w_in, wa, wb, wo, tm):
    n = x.shape[0]
    row = lambda width: pl.BlockSpec((tm, width), lambda i: (i, 0))
    return pl.pallas_call(
        _mixout_kernel,
        grid=(n // tm,),
        in_specs=[row(D_MODEL), row(A_WIDTH), row(B_WIDTH), _vec(layer), _vec(layer), _col_block(3), _col_block(4),
                  _resident((A_WIDTH, D_MODEL)), _resident((B_WIDTH, D_MODEL)), _resident((D_MODEL, D_MODEL))],
        out_specs=row(D_MODEL),
        out_shape=jax.ShapeDtypeStruct((n, D_MODEL), F32),
        compiler_params=_params(("parallel",)),
        name="mixout",
    )(x, a_out, bn, pre, post, w_in, w_in, wa, wb, wo)


def _mixout_sample_kernel(x_ref, a_ref, bn_ref, pre_ref, post_ref, wg0_ref, wg1_ref, wa_ref, wb_ref, wo_ref,
                          o_ref, wa16_ref, wb16_ref, wo16_ref):
    wa, wb, wo = wa_ref[...].astype(BF16), wb_ref[...].astype(BF16), wo_ref[...].astype(BF16)
    wa16_ref[...] = wa
    wb16_ref[...] = wb
    wo16_ref[...] = wo
    o_ref[...] = _mix(x_ref[...], a_ref[...], bn_ref[...], pre_ref[...], post_ref[...],
                      wg0_ref[...], wg1_ref[...], wa, wb, wo)


def _mixout_sample(x, a_out, bn, layer, pre, post, w_in16, wa, wb, wo):
    n = x.shape[0]
    full = lambda *shape: pl.BlockSpec(shape, lambda i: (0,) * len(shape))
    shapes = ((A_WIDTH, D_MODEL), (B_WIDTH, D_MODEL), (D_MODEL, D_MODEL))
    return pl.pallas_call(
        _mixout_sample_kernel,
        grid=(1,),
        in_specs=[full(n, D_MODEL), full(n, A_WIDTH), full(n, B_WIDTH), _vec(layer), _vec(layer),
                  _col_block(3), _col_block(4)] + [_layer_block(s, layer) for s in shapes],
        out_specs=[full(n, D_MODEL)] + [full(*s) for s in shapes],
        out_shape=[jax.ShapeDtypeStruct((n, D_MODEL), F32)] + [jax.ShapeDtypeStruct(s, BF16) for s in shapes],
        compiler_params=_params(("arbitrary",)),
        name="mixout_sample",
    )(x, a_out, bn, pre, post, w_in16, w_in16, wa, wb, wo)


def _rope_tables(pos):
    half = ROPE_DIM // 2
    inv = ROPE_THETA ** (-jnp.arange(half, dtype=F32) / half)
    dim = jnp.arange(LANES) % A_HEAD_DIM
    ang = pos.astype(F32)[:, None] * inv[dim % half][None, :]
    cos, sin = jnp.cos(ang), jnp.sin(ang)
    lo, hi = (dim < half)[None, :], jnp.logical_and(dim >= half, dim < ROPE_DIM)[None, :]
    return (jnp.where(jnp.logical_or(lo, hi), cos, 1.0), jnp.where(lo, -sin, 0.0), jnp.where(hi, sin, 0.0))


def kernel(x_prompt, x_sample, cache_k, cache_v, state_hgrn, ffn1_norm_pre, ffn1_norm_post, ffn1_w_gate, ffn1_w_up, ffn1_w_down, mix_norm_pre, mix_norm_post, w_in, hgrn_lb_logits, hgrn_out_norm, w_a_out, w_b_out, w_mix_out, ffn2_norm_pre, ffn2_norm_post, ffn2_w_gate, ffn2_w_up, ffn2_w_down):
    batch, s_len, _ = x_prompt.shape
    n_dec, t_dec, _ = x_sample.shape
    assert batch == 1 and t_dec == 1 and s_len % ATTN_TILE == 0
    tm = 512
    tm_wide = 1024
    th = min(4096, s_len)
    cache_kt = jnp.transpose(cache_k, (0, 1, 3, 4, 2))
    cache_vt = jnp.transpose(cache_v, (0, 1, 3, 4, 2))
    yp = x_prompt.reshape(s_len, D_MODEL)
    ys = x_sample.reshape(n_dec, D_MODEL)
    tabs_p = _rope_tables(jnp.arange(s_len))
    tabs_s = _rope_tables(jnp.full((n_dec,), PAST_LEN))
    lbl = hgrn_lb_logits.astype(F32)
    vecs = lambda a: a.reshape(DEPTH, 1, -1).astype(F32)
    w_len = min(A_MAX_WINDOW, s_len)
    f1_pre, f1_post, f2_pre, f2_post = (vecs(a) for a in (ffn1_norm_pre, ffn1_norm_post, ffn2_norm_pre, ffn2_norm_post))
    m_pre, m_post, nrm = vecs(mix_norm_pre), vecs(mix_norm_post), vecs(hgrn_out_norm)
    kp, vp, sp, ksn, vsn = [], [], [], [], []
    states = None
    for l in range(DEPTH):
        ys, *f1 = _ffn_sample(ys, l, f1_pre, f1_post, ffn1_w_gate, ffn1_w_up, ffn1_w_down)
        qa, ka, va, sq, sk, sv, sg, win = _inproj_sample(ys, m_pre, tabs_s, lbl, w_in, l)
        yp = _ffn(yp, l, f1_pre, f1_post, *f1, tm_wide)
        q1, q4, q16, k1, k4, k16, v1, v4, v16, kf, vf = _inproj_attn(yp, m_pre, tabs_p, win, l, tm)
        hq, hk, hv, hg = _inproj_rec(yp, m_pre, lbl, win, l, tm)
        a_out = _attn_prompt(q1, q4, q16, k1, k4, k16, v1, v4, v16)
        bn, st_p, a_s, bn_s, states = _recurrent(l, hq, hk, hv, hg, nrm, qa, ka, va, cache_kt, cache_vt,
                                                 sq, sk, sv, sg, state_hgrn, states, th)
        ys, *mo = _mixout_sample(ys, a_s, bn_s, l, m_pre, m_post, win, w_a_out, w_b_out, w_mix_out)
        ys, *f2 = _ffn_sample(ys, l, f2_pre, f2_post, ffn2_w_gate, ffn2_w_up, ffn2_w_down)
        yp = _mixout(yp, a_out, bn, l, m_pre, m_post, win, *mo, tm_wide)
        yp = _ffn(yp, l, f2_pre, f2_post, *f2, tm_wide)
        kp.append(kf.reshape(batch, w_len, A_HEADS, A_HEAD_DIM))
        vp.append(vf.reshape(batch, w_len, A_HEADS, A_HEAD_DIM))
        sp.append(st_p.reshape(batch, B_HEADS, B_HEAD_DIM, B_HEAD_DIM))
        ksn.append(ka.reshape(n_dec, t_dec, A_HEADS, A_HEAD_DIM))
        vsn.append(va.reshape(n_dec, t_dec, A_HEADS, A_HEAD_DIM))
    return (yp.reshape(batch, s_len, D_MODEL), ys.reshape(n_dec, t_dec, D_MODEL),
            jnp.stack(kp), jnp.stack(vp), jnp.stack(sp), jnp.stack(ksn), jnp.stack(vsn), states)
```
